```python
import jax, jax.numpy as jnp
from jax import lax
import numpy as np

D_MODEL = 1024
BATCH = 8
SEQ = 2048
DEPTH = 2

CHUNK = 128
RET_HEADS = 4
RET_QK_DIM = 128
RET_V_DIM = 256
RET_QK_WIDTH = RET_HEADS * RET_QK_DIM
RET_V_WIDTH = RET_HEADS * RET_V_DIM
GMLP_GROUPS = 4
GMLP_WIDTH = D_MODEL
GMLP_GROUP_DIM = GMLP_WIDTH // GMLP_GROUPS
POOL_WINDOWS = (2, 4, 8, 16)
POOL_GROUPS = 4
POOL_WIDTH = D_MODEL
POOL_GROUP_DIM = POOL_WIDTH // POOL_GROUPS
N_BRANCH = 3
BRANCH_WIDTH = D_MODEL
D_FF = 4 * D_MODEL
N_MOD = 6
ROPE_BASE = 10000.0
EPS = 1e-6
SPLITS = (RET_QK_WIDTH, RET_QK_WIDTH, RET_V_WIDTH, RET_V_WIDTH,
          GMLP_WIDTH, GMLP_WIDTH, POOL_WIDTH, N_BRANCH * D_MODEL)
D_IN = 2 * RET_QK_WIDTH + 2 * RET_V_WIDTH + 2 * GMLP_WIDTH + POOL_WIDTH + N_BRANCH * D_MODEL

kernel_name = "hybrid_retention_gmlp_pool_adaln"


def rms_norm(x, gain=None):
    xf = x.astype(jnp.float32)
    y = xf * lax.rsqrt(jnp.mean(xf * xf, axis=-1, keepdims=True) + EPS)
    if gain is not None:
        y = y * gain.astype(jnp.float32)
    return y.astype(x.dtype)


def split_columns(proj):
    pieces, start = [], 0
    for w in SPLITS:
        pieces.append(proj[..., start:start + w])
        start += w
    return pieces


def rotary(x, positions):
    half = x.shape[-1] // 2
    inv_freq = ROPE_BASE ** (-jnp.arange(half, dtype=jnp.float32) / half)
    ang = positions.astype(jnp.float32)[:, :, None] * inv_freq
    cos = jnp.cos(ang)[:, :, None, :]
    sin = jnp.sin(ang)[:, :, None, :]
    xf = x.astype(jnp.float32)
    x1, x2 = xf[..., :half], xf[..., half:]
    return jnp.concatenate([x1 * cos - x2 * sin, x2 * cos + x1 * sin], axis=-1).astype(x.dtype)


def retention(q, k, v, g, positions):
    B, S, _ = q.shape
    nc = S // CHUNK
    dt = q.dtype
    q = rotary(q.reshape(B, S, RET_HEADS, RET_QK_DIM), positions)
    k = rotary(k.reshape(B, S, RET_HEADS, RET_QK_DIM), positions) * (RET_QK_DIM ** -0.5)
    v = v.reshape(B, S, RET_HEADS, RET_V_DIM)
    log_gamma = jnp.log1p(-jnp.power(2.0, -5.0 - jnp.arange(RET_HEADS, dtype=jnp.float32)))
    pos = jnp.arange(CHUNK, dtype=jnp.float32)
    rel = pos[:, None] - pos[None, :]
    causal = rel >= 0
    decay_intra = jnp.where(causal[None],
                            jnp.exp(log_gamma[:, None, None] * jnp.where(causal, rel, 0.0)[None]),
                            0.0)
    decay_q = jnp.exp(log_gamma[:, None] * (pos + 1.0)[None])
    decay_k = jnp.exp(log_gamma[:, None] * (CHUNK - 1.0 - pos)[None])
    decay_chunk = jnp.exp(log_gamma * CHUNK)

    qc = q.reshape(B, nc, CHUNK, RET_HEADS, RET_QK_DIM)
    kc = k.reshape(B, nc, CHUNK, RET_HEADS, RET_QK_DIM)
    vc = v.reshape(B, nc, CHUNK, RET_HEADS, RET_V_DIM)
    scores = jnp.einsum('bnihd,bnjhd->bnhij', qc, kc) * decay_intra.astype(dt)
    intra = jnp.einsum('bnhij,bnjhe->bnihe', scores, vc)
    kv = jnp.einsum('bnjhd,hj,bnjhe->nbhde', kc, decay_k.astype(dt), vc).astype(jnp.float32)

    def step(state, kv_n):
        return decay_chunk[None, :, None, None] * state + kv_n, state

    _, prev = lax.scan(step, jnp.zeros((B, RET_HEADS, RET_QK_DIM, RET_V_DIM), jnp.float32), kv)
    cross = jnp.einsum('bnihd,hi,nbhde->bnihe', qc, decay_q.astype(dt), prev.astype(dt))
    o = (intra + cross).reshape(B, S, RET_HEADS, RET_V_DIM)
    o = rms_norm(o).reshape(B, S, RET_V_WIDTH)
    return jax.nn.silu(g) * o


def spatial_gating(u, v, w_s, b_s, v_gain):
    B, S, _ = u.shape
    nc = S // CHUNK
    u = jax.nn.gelu(u)
    v = rms_norm(jax.nn.gelu(v), v_gain)
    vc = v.reshape(B, nc, CHUNK, GMLP_GROUPS, GMLP_GROUP_DIM)
    mask = jnp.tril(jnp.ones((CHUNK, CHUNK), dtype=bool))
    w = jnp.where(mask[None], w_s, jnp.zeros_like(w_s))
    mixed = jnp.einsum('gts,bnsgc->bntgc', w, vc) + b_s.T[None, None, :, :, None]
    return u * mixed.reshape(B, S, GMLP_WIDTH)


def multiscale_pool(p, w_pool, b_pool, scale):
    B, S, _ = p.shape
    pg = p.reshape(B, S, POOL_GROUPS, POOL_GROUP_DIM).astype(jnp.float32)
    cs = jnp.concatenate([jnp.zeros((B, 1, POOL_GROUPS, POOL_GROUP_DIM), jnp.float32),
                          jnp.cumsum(pg, axis=1)], axis=1)
    t = jnp.arange(1, S + 1)
    outs = []
    for gi, w in enumerate(POOL_WINDOWS):
        start = jnp.maximum(t - w, 0)
        window_sum = cs[:, 1:, gi] - cs[:, start, gi]
        count = jnp.minimum(t, w).astype(jnp.float32)[None, :, None]
        outs.append(window_sum / count - pg[:, :, gi])
    pooled = jnp.stack(outs, axis=2).astype(p.dtype)
    mixed = jnp.einsum('bsgc,gce->bsge', pooled, w_pool) + b_pool[None, None]
    return mixed.reshape(B, S, POOL_WIDTH) * scale


def setup_inputs(seed: int = 0) -> dict:
    key = jax.random.key(seed)
    ks = jax.random.split(key, 20)
    f32 = jnp.float32
    nrm = lambda k, shape, s: jax.random.normal(k, shape, f32) * s
    return {
        "x": nrm(ks[0], (BATCH, SEQ, D_MODEL), 1.0),
        "c": nrm(ks[1], (BATCH, D_MODEL), 1.0),
        "positions": jnp.broadcast_to(jnp.arange(SEQ, dtype=jnp.int32), (BATCH, SEQ)),
        "w_ada": nrm(ks[2], (DEPTH, D_MODEL, N_MOD * D_MODEL), 0.5 * D_MODEL ** -0.5),
        "b_ada": nrm(ks[3], (DEPTH, N_MOD * D_MODEL), 0.01),
        "norm1": 1.0 + nrm(ks[4], (DEPTH, D_MODEL), 0.02),
        "norm2": 1.0 + nrm(ks[5], (DEPTH, D_MODEL), 0.02),
        "w_in": nrm(ks[6], (DEPTH, D_MODEL, D_IN), D_MODEL ** -0.5),
        "ws_gmlp": nrm(ks[7], (DEPTH, GMLP_GROUPS, CHUNK, CHUNK), CHUNK ** -0.5),
        "bs_gmlp": 1.0 + nrm(ks[8], (DEPTH, GMLP_GROUPS, CHUNK), 0.1),
        "vnorm_gmlp": 1.0 + nrm(ks[9], (DEPTH, GMLP_WIDTH), 0.02),
        "w_pool": nrm(ks[10], (DEPTH, POOL_GROUPS, POOL_GROUP_DIM, POOL_GROUP_DIM), POOL_GROUP_DIM ** -0.5),
        "b_pool": nrm(ks[11], (DEPTH, POOL_GROUPS, POOL_GROUP_DIM), 0.01),
        "pool_scale": 1.0 + nrm(ks[12], (DEPTH, POOL_WIDTH), 0.1),
        "w_branch": nrm(ks[13], (DEPTH, N_BRANCH, BRANCH_WIDTH, D_MODEL), BRANCH_WIDTH ** -0.5),
        "w_out": nrm(ks[14], (DEPTH, D_MODEL, D_MODEL), D_MODEL ** -0.5),
        "w_ff1": nrm(ks[15], (DEPTH, D_MODEL, D_FF), D_MODEL ** -0.5),
        "w_ff2": nrm(ks[16], (DEPTH, D_FF, D_MODEL), D_FF ** -0.5),
        "final_norm": 1.0 + nrm(ks[17], (D_MODEL,), 0.02),
    }


def reference(x, c, positions, w_ada, b_ada, norm1, norm2, w_in, ws_gmlp, bs_gmlp, vnorm_gmlp,
              w_pool, b_pool, pool_scale, w_branch, w_out, w_ff1, w_ff2, final_norm):
    B, S, _ = x.shape
    c_act = jax.nn.silu(c)
    for l in range(DEPTH):
        mod = c_act @ w_ada[l] + b_ada[l]
        sh1, sc1, gt1, sh2, sc2, gt2 = [m[:, None, :] for m in jnp.split(mod, N_MOD, axis=-1)]

        h = rms_norm(x, norm1[l]) * (1.0 + sc1) + sh1
        proj = h @ w_in[l]
        q, k, v, g, u, vs, p, gate_cols = split_columns(proj)
        y_ret = retention(q, k, v, g, positions)
        y_sgu = spatial_gating(u, vs, ws_gmlp[l], bs_gmlp[l], vnorm_gmlp[l])
        y_pool = multiscale_pool(p, w_pool[l], b_pool[l], pool_scale[l])
        branches = jnp.stack([y_ret, y_sgu, y_pool], axis=2)
        branch_proj = jnp.einsum('bsnc,ncd->bsnd', branches, w_branch[l])
        gates = jax.nn.sigmoid(gate_cols.reshape(B, S, N_BRANCH, D_MODEL))
        merged = jnp.sum(gates * branch_proj, axis=2)
        x = x + gt1 * (merged @ w_out[l])

        h2 = rms_norm(x, norm2[l]) * (1.0 + sc2) + sh2
        hidden = jnp.square(jax.nn.relu(h2 @ w_ff1[l]))
        x = x + gt2 * (hidden @ w_ff2[l])
    return rms_norm(x, final_norm)
```

```python
import functools

import jax
import jax.numpy as jnp
from jax import lax
from jax.experimental import pallas as pl
from jax.experimental.pallas import tpu as pltpu

CHUNK = 128
RET_HEADS = 4
RET_QK_DIM = 128
RET_V_DIM = 256
GMLP_GROUPS = 4
POOL_WINDOWS = (2, 4, 8, 16)
POOL_HISTORY = 16
N_BRANCH = 3
N_MOD = 6
ROPE_BASE = 10000.0
EPS = 1e-6

LANES = 128
SUBLANES = 8
VMEM_BYTES_V7X = 64 * 1024 * 1024

MIXER_ROWS = 256
FFN_ROWS = 512
FF_BLOCK = 1024
ROPE_ROWS = 2048
ADA_COLS = 1024

F32 = jnp.float32
BF16 = jnp.bfloat16


def _dot(a, b):
    return jnp.dot(a, b, preferred_element_type=F32)


def _rms(x):
    return x * lax.rsqrt(jnp.mean(x * x, axis=-1, keepdims=True) + EPS)


def _resident(shape):
    zeros = (0,) * len(shape)
    return pl.BlockSpec(shape, lambda *_: zeros, pipeline_mode=pl.Buffered(1))


def _vmem_limit(resident_bytes, streamed_bytes, scratch_bytes, temp_bytes):
    need = resident_bytes + 2 * streamed_bytes + scratch_bytes + temp_bytes
    return min(int(need), VMEM_BYTES_V7X - 4 * 1024 * 1024)


def _ada_kernel(c_ref, w_ref, b_ref, o_ref):
    c = c_ref[...]
    c_act = (c * jax.nn.sigmoid(c)).astype(BF16)
    o_ref[0] = _dot(c_act, w_ref[0].astype(BF16)) + b_ref[0]


def _ada_rows(c, w_ada, b_ada):
    depth, d, n = w_ada.shape
    b = c.shape[0]
    return pl.pallas_call(
        _ada_kernel,
        grid=(depth, n // ADA_COLS),
        in_specs=[
            pl.BlockSpec((b, d), lambda l, j: (0, 0)),
            pl.BlockSpec((1, d, ADA_COLS), lambda l, j: (l, 0, j)),
            pl.BlockSpec((1, 1, ADA_COLS), lambda l, j: (l, 0, j)),
        ],
        out_specs=pl.BlockSpec((1, b, ADA_COLS), lambda l, j: (l, 0, j)),
        out_shape=jax.ShapeDtypeStruct((depth, b, n), F32),
        compiler_params=pltpu.CompilerParams(
            dimension_semantics=("arbitrary", "arbitrary")),
        name="ada_rows",
    )(c, w_ada, b_ada.reshape(depth, 1, n))


def _rope_kernel(pos_ref, freq_ref, sign_ref, cos_ref, sin_ref):
    ang = pos_ref[0].astype(F32) * freq_ref[...]
    cos_ref[0] = jnp.cos(ang)
    sin_ref[0] = jnp.sin(ang) * sign_ref[...]


def _rope_tables(positions):
    b, s = positions.shape
    half = RET_QK_DIM // 2
    inv_freq = ROPE_BASE ** (-jnp.arange(half, dtype=F32) / half)
    freq = jnp.concatenate([inv_freq, inv_freq]).reshape(1, RET_QK_DIM)
    sign = jnp.concatenate([-jnp.ones((half,), F32), jnp.ones((half,), F32)]).reshape(1, RET_QK_DIM)
    rows = min(ROPE_ROWS, s)
    out = jax.ShapeDtypeStruct((b, s, RET_QK_DIM), F32)
    return pl.pallas_call(
        _rope_kernel,
        grid=(b, s // rows),
        in_specs=[
            pl.BlockSpec((1, rows, 1), lambda i, j: (i, j, 0)),
            pl.BlockSpec((1, RET_QK_DIM), lambda i, j: (0, 0)),
            pl.BlockSpec((1, RET_QK_DIM), lambda i, j: (0, 0)),
        ],
        out_specs=[pl.BlockSpec((1, rows, RET_QK_DIM), lambda i, j: (i, j, 0))] * 2,
        out_shape=[out, out],
        compiler_params=pltpu.CompilerParams(
            dimension_semantics=("arbitrary", "arbitrary")),
        name="rope_tables",
    )(positions.reshape(b, s, 1), freq, sign)


def _gelu_tanh(x):
    return 0.5 * x * (1.0 + jnp.tanh(0.7978845608028654 * (x + 0.044715 * (x * x * x))))


def _mixer_kernel(dchunk_ref, x_ref, mod_ref, n1_ref, cos_ref, sin_ref, win_ref, wbr_ref, wout_ref,
                  wpool_ref, ws_ref, bst_ref, vnorm_ref, bpool_ref, pscale_ref,
                  dintra_ref, dq_ref, dk_ref,
                  o_ref,
                  state_ref, ptail_ref, qb_ref, qd_ref, kb_ref, kd_ref, vb_ref, sg_ref,
                  u_ref, vn_ref, y_ref, m_ref):
    rows, d = x_ref.shape[1], x_ref.shape[2]
    n_chunks = rows // CHUNK
    qk_w = RET_HEADS * RET_QK_DIM
    v_w = RET_HEADS * RET_V_DIM
    o_q, o_k, o_v = 0, qk_w, 2 * qk_w
    o_g = o_v + v_w
    o_u = o_g + v_w
    o_vs = o_u + d
    o_p = o_vs + d
    o_gate = o_p + d
    seq_step = pl.program_id(1)

    @pl.when(seq_step == 0)
    def _():
        state_ref[...] = jnp.zeros_like(state_ref)
        ptail_ref[...] = jnp.zeros_like(ptail_ref)

    x = x_ref[0]
    sh1, sc1, gt1 = mod_ref[0, 0:1, :], mod_ref[0, 1:2, :], mod_ref[0, 2:3, :]
    h = ((_rms(x) * n1_ref[...]) * (1.0 + sc1) + sh1).astype(BF16)

    def proj(lo, width):
        return _dot(h, win_ref[:, lo:lo + width])

    def gate(n):
        return jax.nn.sigmoid(proj(o_gate + n * d, d))

    cosv, sinv = cos_ref[0], sin_ref[0]
    qf, kf = proj(o_q, qk_w), proj(o_k, qk_w)
    k_scale = RET_QK_DIM ** -0.5
    for hd in range(RET_HEADS):
        sl = slice(hd * RET_QK_DIM, (hd + 1) * RET_QK_DIM)
        q_h, k_h = qf[:, sl], kf[:, sl]
        q_r = q_h * cosv + pltpu.roll(q_h, RET_QK_DIM // 2, 1) * sinv
        k_r = (k_h * cosv + pltpu.roll(k_h, RET_QK_DIM // 2, 1) * sinv) * k_scale
        qb_ref[:, sl] = q_r.astype(BF16)
        qd_ref[:, sl] = (q_r * dq_ref[hd]).astype(BF16)
        kb_ref[:, sl] = k_r.astype(BF16)
        kd_ref[:, sl] = (k_r * dk_ref[hd]).astype(BF16)
    vb_ref[...] = proj(o_v, v_w).astype(BF16)
    g = proj(o_g, v_w)
    sg_ref[...] = g * jax.nn.sigmoid(g)

    for c in range(n_chunks):
        rs = slice(c * CHUNK, (c + 1) * CHUNK)
        for hd in range(RET_HEADS):
            sl = slice(hd * RET_QK_DIM, (hd + 1) * RET_QK_DIM)
            vsl = slice(hd * RET_V_DIM, (hd + 1) * RET_V_DIM)
            v_h = vb_ref[rs, vsl]
            scores = lax.dot_general(qb_ref[rs, sl], kb_ref[rs, sl], (((1,), (1,)), ((), ())),
                                     preferred_element_type=F32) * dintra_ref[hd]
            state = state_ref[hd]
            lhs = jnp.concatenate([scores.astype(BF16), qd_ref[rs, sl]], axis=1)
            rhs = jnp.concatenate([v_h, state.astype(BF16)], axis=0)
            o = _dot(lhs, rhs)
            kv = lax.dot_general(kd_ref[rs, sl], v_h, (((0,), (0,)), ((), ())),
                                 preferred_element_type=F32)
            state_ref[hd] = dchunk_ref[hd] * state + kv
            y_ref[0, rs, vsl] = (sg_ref[rs, vsl] * _rms(o)).astype(BF16)
    m_ref[...] = gate(0) * _dot(y_ref[0], wbr_ref[0])

    u_ref[...] = _gelu_tanh(proj(o_u, d))
    vs = _gelu_tanh(proj(o_vs, d))
    vn_ref[...] = (_rms(vs) * vnorm_ref[...]).astype(BF16)
    gdim = d // GMLP_GROUPS
    tri = lax.broadcasted_iota(jnp.int32, (CHUNK, CHUNK), 0) >= lax.broadcasted_iota(jnp.int32, (CHUNK, CHUNK), 1)
    for gi in range(GMLP_GROUPS):
        gsl = slice(gi * gdim, (gi + 1) * gdim)
        w_tri = jnp.where(tri, ws_ref[gi], 0.0).astype(BF16)
        bias = jnp.broadcast_to(bst_ref[:, gi:gi + 1], (CHUNK, gdim))
        for c in range(n_chunks):
            rs = slice(c * CHUNK, (c + 1) * CHUNK)
            mixed = _dot(w_tri, vn_ref[rs, gsl]) + bias
            y_ref[1, rs, gsl] = (u_ref[rs, gsl] * mixed).astype(BF16)
    m_ref[...] += gate(1) * _dot(y_ref[1], wbr_ref[1])

    p = proj(o_p, d)
    ext = jnp.concatenate([ptail_ref[...], p], axis=0)
    ptail_ref[...] = p[rows - POOL_HISTORY:, :]
    pdim = d // len(POOL_WINDOWS)
    pos = seq_step * rows + lax.broadcasted_iota(jnp.int32, (rows, pdim), 0)
    for gi, window in enumerate(POOL_WINDOWS):
        psl = slice(gi * pdim, (gi + 1) * pdim)
        acc = ext[:, psl]
        span = 1
        while span < window:
            acc = acc + pltpu.roll(acc, span, 0)
            span *= 2
        count = jnp.minimum(pos + 1, window).astype(F32)
        pooled = acc[POOL_HISTORY:, :] / count - p[:, psl]
        mixed = _dot(pooled.astype(BF16), wpool_ref[gi]) + bpool_ref[:, psl]
        y_ref[2, :, psl] = (mixed * pscale_ref[:, psl]).astype(BF16)
    m_ref[...] += gate(2) * _dot(y_ref[2], wbr_ref[2])

    o_ref[0] = x + gt1 * _dot(m_ref[...].astype(BF16), wout_ref[...])


def _mixer(x, mod, norm1, cos_t, sin_t, w_in, w_branch, w_out, w_pool, ws, bs_t, vnorm, b_pool,
           pool_scale, dintra, dq, dk, dchunk):
    b, s, d = x.shape
    rows = MIXER_ROWS
    d_in = w_in.shape[1]
    v_w = RET_HEADS * RET_V_DIM
    qk_w = RET_HEADS * RET_QK_DIM
    row_spec = lambda width: pl.BlockSpec((1, rows, width), lambda i, j: (i, j, 0))
    in_specs = [
        pl.BlockSpec(memory_space=pltpu.SMEM),
        row_spec(d),
        pl.BlockSpec((1, N_MOD, d), lambda i, j: (i, 0, 0)),
        _resident((1, d)),
        row_spec(RET_QK_DIM), row_spec(RET_QK_DIM),
        _resident((d, d_in)), _resident((N_BRANCH, d, d)), _resident((d, d)),
        _resident(w_pool.shape), _resident(ws.shape), _resident(bs_t.shape),
        _resident((1, d)), _resident((1, d)), _resident((1, d)),
        _resident(dintra.shape), _resident(dq.shape), _resident(dk.shape),
    ]
    scratch = [
        pltpu.VMEM((RET_HEADS, RET_QK_DIM, RET_V_DIM), F32),
        pltpu.VMEM((POOL_HISTORY, d), F32),
        pltpu.VMEM((rows, qk_w), BF16), pltpu.VMEM((rows, qk_w), BF16),
        pltpu.VMEM((rows, qk_w), BF16), pltpu.VMEM((rows, qk_w), BF16),
        pltpu.VMEM((rows, v_w), BF16),
        pltpu.VMEM((rows, v_w), F32),
        pltpu.VMEM((rows, d), F32),
        pltpu.VMEM((rows, d), BF16),
        pltpu.VMEM((N_BRANCH, rows, d), BF16),
        pltpu.VMEM((rows, d), F32),
    ]
    resident = 2 * (d * d_in + N_BRANCH * d * d + d * d + w_pool.size) + 4 * (
        ws.size + bs_t.size + dintra.size + dq.size + dk.size + 6 * d)
    streamed = 4 * rows * (2 * d + 2 * RET_QK_DIM) + 4 * N_MOD * d
    scratch_bytes = 4 * (RET_HEADS * RET_QK_DIM * RET_V_DIM + POOL_HISTORY * d) + rows * (
        2 * 4 * qk_w + 2 * v_w + 4 * v_w + 4 * d + 2 * d + 2 * N_BRANCH * d + 4 * d)
    temps = 8 * 4 * rows * d
    return pl.pallas_call(
        _mixer_kernel,
        grid=(b, s // rows),
        in_specs=in_specs,
        out_specs=row_spec(d),
        out_shape=jax.ShapeDtypeStruct((b, s, d), F32),
        scratch_shapes=scratch,
        compiler_params=pltpu.CompilerParams(
            dimension_semantics=("arbitrary", "arbitrary"),
            vmem_limit_bytes=_vmem_limit(resident, streamed, scratch_bytes, temps)),
        name="token_mixer",
    )(dchunk, x, mod, norm1, cos_t, sin_t, w_in, w_branch, w_out, w_pool, ws, bs_t, vnorm, b_pool,
      pool_scale, dintra, dq, dk)


def _ffn_kernel(x_ref, mod_ref, n2_ref, w1_ref, w2_ref, fn_ref, o_ref, *, final):
    x = x_ref[0]
    sh2, sc2, gt2 = mod_ref[0, 3:4, :], mod_ref[0, 4:5, :], mod_ref[0, 5:6, :]
    h = ((_rms(x) * n2_ref[...]) * (1.0 + sc2) + sh2).astype(BF16)
    d_ff = w1_ref.shape[1]
    acc = None
    for lo in range(0, d_ff, FF_BLOCK):
        hid = jnp.square(jnp.maximum(_dot(h, w1_ref[:, lo:lo + FF_BLOCK]), 0.0)).astype(BF16)
        part = _dot(hid, w2_ref[lo:lo + FF_BLOCK, :])
        acc = part if acc is None else acc + part
    y = x + gt2 * acc
    if final:
        y = _rms(y) * fn_ref[...]
    o_ref[0] = y


def _ffn(x, mod, norm2, w1, w2, final_norm, final):
    b, s, d = x.shape
    rows = FFN_ROWS
    d_ff = w1.shape[1]
    row_spec = pl.BlockSpec((1, rows, d), lambda i, j: (i, j, 0))
    resident = 2 * 2 * d * d_ff + 4 * 2 * d
    streamed = 4 * 2 * rows * d + 4 * N_MOD * d
    temps = rows * (2 * d + 6 * FF_BLOCK + 3 * 4 * d)
    return pl.pallas_call(
        functools.partial(_ffn_kernel, final=final),
        grid=(b, s // rows),
        in_specs=[
            row_spec,
            pl.BlockSpec((1, N_MOD, d), lambda i, j: (i, 0, 0)),
            _resident((1, d)), _resident((d, d_ff)), _resident((d_ff, d)), _resident((1, d)),
        ],
        out_specs=row_spec,
        out_shape=jax.ShapeDtypeStruct((b, s, d), F32),
        compiler_params=pltpu.CompilerParams(
            dimension_semantics=("arbitrary", "arbitrary"),
            vmem_limit_bytes=_vmem_limit(resident, streamed, 0, temps)),
        name="channel_mlp",
    )(x, mod, norm2, w1, w2, final_norm)


def _retention_decay_tables(rows):
    log_gamma = jnp.log1p(-jnp.power(2.0, -5.0 - jnp.arange(RET_HEADS, dtype=F32)))
    pos = jnp.arange(CHUNK, dtype=F32)
    rel = pos[:, None] - pos[None, :]
    causal = rel >= 0
    intra = jnp.where(causal[None], jnp.exp(log_gamma[:, None, None] * jnp.where(causal, rel, 0.0)[None]), 0.0)
    decay_q = jnp.exp(log_gamma[:, None] * (pos + 1.0)[None])
    decay_k = jnp.exp(log_gamma[:, None] * (CHUNK - 1.0 - pos)[None])
    decay_chunk = jnp.exp(log_gamma * CHUNK)
    reps = rows // CHUNK
    widen = lambda t: jnp.broadcast_to(jnp.tile(t, (1, reps))[:, :, None], (RET_HEADS, rows, RET_QK_DIM))
    return intra, widen(decay_q), widen(decay_k), decay_chunk


def kernel(x, c, positions, w_ada, b_ada, norm1, norm2, w_in, ws_gmlp, bs_gmlp, vnorm_gmlp, w_pool, b_pool,
           pool_scale, w_branch, w_out, w_ff1, w_ff2, final_norm):
    depth = w_in.shape[0]
    b, s, d = x.shape
    assert s % MIXER_ROWS == 0 and s % FFN_ROWS == 0 and MIXER_ROWS % CHUNK == 0
    assert w_ff1.shape[2] % FF_BLOCK == 0 and w_ada.shape[2] % ADA_COLS == 0

    mod = _ada_rows(c, w_ada, b_ada).reshape(depth, b, N_MOD, d)
    cos_t, sin_t = _rope_tables(positions)
    dintra, dq, dk, dchunk = _retention_decay_tables(MIXER_ROWS)
    row = lambda v: v.reshape(1, d)

    for l in range(depth):
        x = _mixer(x, mod[l], row(norm1[l]), cos_t, sin_t,
                   w_in[l].astype(BF16), w_branch[l].astype(BF16), w_out[l].astype(BF16),
                   w_pool[l].astype(BF16), ws_gmlp[l], bs_gmlp[l].T, row(vnorm_gmlp[l]),
                   row(b_pool[l]), row(pool_scale[l]), dintra, dq, dk, dchunk)
        x = _ffn(x, mod[l], row(norm2[l]), w_ff1[l].astype(BF16), w_ff2[l].astype(BF16),
                 row(final_norm), final=(l == depth - 1))
    return x
```

```python
import functools

import jax
import jax.numpy as jnp
from jax import lax
from jax.experimental import pallas as pl
from jax.experimental.pallas import tpu as pltpu

CHUNK = 128
RET_HEADS = 4
RET_QK_DIM = 128
RET_V_DIM = 256
GMLP_GROUPS = 4
POOL_WINDOWS = (2, 4, 8, 16)
POOL_HISTORY = 16
N_BRANCH = 3
N_MOD = 6
ROPE_BASE = 10000.0
EPS = 1e-6

VMEM_BYTES_V7X = 64 * 1024 * 1024

MIXER_ROWS = 256
FFN_ROWS = 512
FF_BLOCK = 1024
ROPE_ROWS = 2048
ADA_COLS = 1024

F32 = jnp.float32
BF16 = jnp.bfloat16


def _dot(a, b):
    return jnp.dot(a, b, preferred_element_type=F32)


def _rms(x):
    return x * lax.rsqrt(jnp.mean(x * x, axis=-1, keepdims=True) + EPS)


def _resident(shape):
    zeros = (0,) * len(shape)
    return pl.BlockSpec(shape, lambda *_: zeros, pipeline_mode=pl.Buffered(1))


def _layer_resident(layer, shape):
    index = (layer,) + (0,) * len(shape)
    return pl.BlockSpec((None,) + tuple(shape), lambda *_: index, pipeline_mode=pl.Buffered(1))


def _vmem_limit(resident_bytes, streamed_bytes, scratch_bytes, temp_bytes):
    need = resident_bytes + 2 * streamed_bytes + scratch_bytes + temp_bytes
    return min(int(need), VMEM_BYTES_V7X - 4 * 1024 * 1024)


def _ada_kernel(c_ref, w_ref, b_ref, o_ref):
    c = c_ref[...]
    c_act = (c * jax.nn.sigmoid(c)).astype(BF16)
    o_ref[0] = _dot(c_act, w_ref[0].astype(BF16)) + b_ref[0]


def _ada_rows(c, w_ada, b_ada):
    depth, d, n = w_ada.shape
    b = c.shape[0]
    return pl.pallas_call(
        _ada_kernel,
        grid=(depth, n // ADA_COLS),
        in_specs=[
            pl.BlockSpec((b, d), lambda l, j: (0, 0)),
            pl.BlockSpec((1, d, ADA_COLS), lambda l, j: (l, 0, j)),
            pl.BlockSpec((1, 1, ADA_COLS), lambda l, j: (l, 0, j)),
        ],
        out_specs=pl.BlockSpec((1, b, ADA_COLS), lambda l, j: (l, 0, j)),
        out_shape=jax.ShapeDtypeStruct((depth, b, n), F32),
        compiler_params=pltpu.CompilerParams(
            dimension_semantics=("arbitrary", "arbitrary")),
        name="ada_rows",
    )(c, w_ada, b_ada.reshape(depth, 1, n))


def _rope_kernel(pos_ref, freq_ref, sign_ref, cos_ref, sin_ref):
    ang = pos_ref[0].astype(F32) * freq_ref[...]
    cos_ref[0] = jnp.cos(ang)
    sin_ref[0] = jnp.sin(ang) * sign_ref[...]


def _rope_tables(positions):
    b, s = positions.shape
    half = RET_QK_DIM // 2
    inv_freq = ROPE_BASE ** (-jnp.arange(half, dtype=F32) / half)
    freq = jnp.concatenate([inv_freq, inv_freq]).reshape(1, RET_QK_DIM)
    sign = jnp.concatenate([-jnp.ones((half,), F32), jnp.ones((half,), F32)]).reshape(1, RET_QK_DIM)
    rows = min(ROPE_ROWS, s)
    out = jax.ShapeDtypeStruct((b, s, RET_QK_DIM), F32)
    return pl.pallas_call(
        _rope_kernel,
        grid=(b, s // rows),
        in_specs=[
            pl.BlockSpec((1, rows, 1), lambda i, j: (i, j, 0)),
            pl.BlockSpec((1, RET_QK_DIM), lambda i, j: (0, 0)),
            pl.BlockSpec((1, RET_QK_DIM), lambda i, j: (0, 0)),
        ],
        out_specs=[pl.BlockSpec((1, rows, RET_QK_DIM), lambda i, j: (i, j, 0))] * 2,
        out_shape=[out, out],
        compiler_params=pltpu.CompilerParams(
            dimension_semantics=("arbitrary", "arbitrary")),
        name="rope_tables",
    )(positions.reshape(b, s, 1), freq, sign)


def _gelu_tanh(x):
    return 0.5 * x * (1.0 + jnp.tanh(0.7978845608028654 * (x + 0.044715 * (x * x * x))))


def _mixer_kernel(dchunk_ref, x_ref, mod_ref, n1_ref, cos_ref, sin_ref, win_ref, wbr_ref, wout_ref,
                  wpool_ref, ws_ref, bst_ref, vnorm_ref, bpool_ref, pscale_ref,
                  dintra_ref, dq_ref, dk_ref,
                  o_ref,
                  state_ref, ptail_ref, qb_ref, qd_ref, kb_ref, kd_ref, vb_ref, sg_ref,
                  u_ref, vn_ref, y_ref, m_ref):
    rows, d = x_ref.shape
    n_chunks = rows // CHUNK
    qk_w = RET_HEADS * RET_QK_DIM
    v_w = RET_HEADS * RET_V_DIM
    gdim = d // GMLP_GROUPS
    pdim = d // len(POOL_WINDOWS)
    o_q, o_k, o_v = 0, qk_w, 2 * qk_w
    o_g = o_v + v_w
    o_u = o_g + v_w
    o_vs = o_u + d
    o_p = o_vs + d
    o_gate = o_p + d
    seq_step = pl.program_id(1)
    row_slices = [slice(c * CHUNK, (c + 1) * CHUNK) for c in range(n_chunks)]
    qk_slices = [slice(hd * RET_QK_DIM, (hd + 1) * RET_QK_DIM) for hd in range(RET_HEADS)]
    v_slices = [slice(hd * RET_V_DIM, (hd + 1) * RET_V_DIM) for hd in range(RET_HEADS)]

    @pl.when(seq_step == 0)
    def _():
        state_ref[...] = jnp.zeros_like(state_ref)
        ptail_ref[...] = jnp.zeros_like(ptail_ref)

    x = x_ref[...]
    sh1, sc1, gt1 = mod_ref[0:1, :], mod_ref[1:2, :], mod_ref[2:3, :]
    h = ((_rms(x) * n1_ref[...]) * (1.0 + sc1) + sh1).astype(BF16)

    def proj(lo, width):
        return _dot(h, win_ref[:, lo:lo + width])

    def gate(n):
        return jax.nn.sigmoid(proj(o_gate + n * d, d))

    cosv, sinv = cos_ref[...], sin_ref[...]
    qf, kf = proj(o_q, qk_w), proj(o_k, qk_w)
    k_scale = RET_QK_DIM ** -0.5
    for hd, sl in enumerate(qk_slices):
        q_h, k_h = qf[:, sl], kf[:, sl]
        q_r = q_h * cosv + pltpu.roll(q_h, RET_QK_DIM // 2, 1) * sinv
        k_r = (k_h * cosv + pltpu.roll(k_h, RET_QK_DIM // 2, 1) * sinv) * k_scale
        qb_ref[:, sl] = q_r.astype(BF16)
        qd_ref[:, sl] = (q_r * dq_ref[hd]).astype(BF16)
        kb_ref[:, sl] = k_r.astype(BF16)
        kd_ref[:, sl] = (k_r * dk_ref[hd]).astype(BF16)
    vb_ref[...] = proj(o_v, v_w).astype(BF16)
    g = proj(o_g, v_w)
    sg_ref[...] = g * jax.nn.sigmoid(g)
    vs = _gelu_tanh(proj(o_vs, d))
    vn_ref[...] = (_rms(vs) * vnorm_ref[...]).astype(BF16)

    scores, kv = {}, {}
    for c, rs in enumerate(row_slices):
        for hd, sl in enumerate(qk_slices):
            s = lax.dot_general(qb_ref[rs, sl], kb_ref[rs, sl], (((1,), (1,)), ((), ())),
                                preferred_element_type=F32)
            scores[c, hd] = (s * dintra_ref[hd]).astype(BF16)
            kv[c, hd] = lax.dot_general(kd_ref[rs, sl], vb_ref[rs, v_slices[hd]], (((0,), (0,)), ((), ())),
                                        preferred_element_type=F32)

    u_ref[...] = _gelu_tanh(proj(o_u, d))

    state = [state_ref[hd] for hd in range(RET_HEADS)]
    ret = {}
    for c, rs in enumerate(row_slices):
        for hd, sl in enumerate(qk_slices):
            lhs = jnp.concatenate([scores[c, hd], qd_ref[rs, sl]], axis=1)
            rhs = jnp.concatenate([vb_ref[rs, v_slices[hd]], state[hd].astype(BF16)], axis=0)
            ret[c, hd] = _dot(lhs, rhs)
            state[hd] = dchunk_ref[hd] * state[hd] + kv[c, hd]
    for hd in range(RET_HEADS):
        state_ref[hd] = state[hd]

    p = proj(o_p, d)
    ext = jnp.concatenate([ptail_ref[...], p], axis=0)
    ptail_ref[...] = p[rows - POOL_HISTORY:, :]
    pos = seq_step * rows + lax.broadcasted_iota(jnp.int32, (rows, pdim), 0)
    pooled = []
    for gi, window in enumerate(POOL_WINDOWS):
        psl = slice(gi * pdim, (gi + 1) * pdim)
        acc = ext[:, psl]
        span = 1
        while span < window:
            acc = acc + pltpu.roll(acc, span, 0)
            span *= 2
        count = jnp.minimum(pos + 1, window).astype(F32)
        pooled.append((acc[POOL_HISTORY:, :] / count - p[:, psl]).astype(BF16))

    for c, rs in enumerate(row_slices):
        for hd, vsl in enumerate(v_slices):
            y_ref[0, rs, vsl] = (sg_ref[rs, vsl] * _rms(ret[c, hd])).astype(BF16)

    tri = lax.broadcasted_iota(jnp.int32, (CHUNK, CHUNK), 0) >= lax.broadcasted_iota(jnp.int32, (CHUNK, CHUNK), 1)
    for gi in range(GMLP_GROUPS):
        gsl = slice(gi * gdim, (gi + 1) * gdim)
        w_tri = jnp.where(tri, ws_ref[gi], 0.0).astype(BF16)
        bias = jnp.broadcast_to(bst_ref[:, gi:gi + 1], (CHUNK, gdim))
        for rs in row_slices:
            mixed = _dot(w_tri, vn_ref[rs, gsl]) + bias
            y_ref[1, rs, gsl] = (u_ref[rs, gsl] * mixed).astype(BF16)

    m_ref[...] = gate(0) * _dot(y_ref[0], wbr_ref[0])

    for gi in range(len(POOL_WINDOWS)):
        psl = slice(gi * pdim, (gi + 1) * pdim)
        mixed = _dot(pooled[gi], wpool_ref[gi]) + bpool_ref[:, psl]
        y_ref[2, :, psl] = (mixed * pscale_ref[:, psl]).astype(BF16)

    m_ref[...] += gate(1) * _dot(y_ref[1], wbr_ref[1])
    m_ref[...] += gate(2) * _dot(y_ref[2], wbr_ref[2])
    o_ref[...] = x + gt1 * _dot(m_ref[...].astype(BF16), wout_ref[...])


def _mixer(layer, x, mod, norm1, cos_t, sin_t, w_in, w_branch, w_out, w_pool, ws, bs_t, vnorm, b_pool,
           pool_scale, dintra, dq, dk, dchunk):
    b, s, d = x.shape
    rows = MIXER_ROWS
    d_in = w_in.shape[2]
    v_w = RET_HEADS * RET_V_DIM
    qk_w = RET_HEADS * RET_QK_DIM
    row_spec = lambda width: pl.BlockSpec((None, rows, width), lambda i, j: (i, j, 0))
    lres = functools.partial(_layer_resident, layer)
    in_specs = [
        pl.BlockSpec(memory_space=pltpu.SMEM),
        row_spec(d),
        pl.BlockSpec((None, None, N_MOD, d), lambda i, j: (layer, i, 0, 0)),
        lres((1, d)),
        row_spec(RET_QK_DIM), row_spec(RET_QK_DIM),
        lres((d, d_in)), lres((N_BRANCH, d, d)), lres((d, d)),
        lres(w_pool.shape[1:]), lres(ws.shape[1:]), lres(bs_t.shape[1:]),
        lres((1, d)), lres((1, d)), lres((1, d)),
        _resident(dintra.shape), _resident(dq.shape), _resident(dk.shape),
    ]
    scratch = [
        pltpu.VMEM((RET_HEADS, RET_QK_DIM, RET_V_DIM), F32),
        pltpu.VMEM((POOL_HISTORY, d), F32),
        pltpu.VMEM((rows, qk_w), BF16), pltpu.VMEM((rows, qk_w), BF16),
        pltpu.VMEM((rows, qk_w), BF16), pltpu.VMEM((rows, qk_w), BF16),
        pltpu.VMEM((rows, v_w), BF16),
        pltpu.VMEM((rows, v_w), F32),
        pltpu.VMEM((rows, d), F32),
        pltpu.VMEM((rows, d), BF16),
        pltpu.VMEM((N_BRANCH, rows, d), BF16),
        pltpu.VMEM((rows, d), F32),
    ]
    resident = 2 * (d * d_in + N_BRANCH * d * d + d * d + w_pool[0].size) + 4 * (
        ws[0].size + bs_t[0].size + dintra.size + dq.size + dk.size + 6 * d)
    streamed = 4 * rows * (2 * d + 2 * RET_QK_DIM) + 4 * N_MOD * d
    scratch_bytes = 4 * (RET_HEADS * RET_QK_DIM * RET_V_DIM + POOL_HISTORY * d) + rows * (
        2 * 4 * qk_w + 2 * v_w + 4 * v_w + 4 * d + 2 * d + 2 * N_BRANCH * d + 4 * d)
    temps = 8 * 4 * rows * d
    return pl.pallas_call(
        _mixer_kernel,
        grid=(b, s // rows),
        in_specs=in_specs,
        out_specs=row_spec(d),
        out_shape=jax.ShapeDtypeStruct((b, s, d), F32),
        scratch_shapes=scratch,
        compiler_params=pltpu.CompilerParams(
            dimension_semantics=("arbitrary", "arbitrary"),
            vmem_limit_bytes=_vmem_limit(resident, streamed, scratch_bytes, temps)),
        name="token_mixer",
    )(dchunk, x, mod, norm1, cos_t, sin_t, w_in, w_branch, w_out, w_pool, ws, bs_t, vnorm, b_pool,
      pool_scale, dintra, dq, dk)


def _ffn_kernel(x_ref, mod_ref, n2_ref, w1_ref, w2_ref, fn_ref, o_ref, *, final):
    x = x_ref[...]
    sh2, sc2, gt2 = mod_ref[3:4, :], mod_ref[4:5, :], mod_ref[5:6, :]
    h = ((_rms(x) * n2_ref[...]) * (1.0 + sc2) + sh2).astype(BF16)
    d_ff = w1_ref.shape[1]
    acc = None
    for lo in range(0, d_ff, FF_BLOCK):
        hid = jnp.square(jnp.maximum(_dot(h, w1_ref[:, lo:lo + FF_BLOCK]), 0.0)).astype(BF16)
        part = _dot(hid, w2_ref[lo:lo + FF_BLOCK, :])
        acc = part if acc is None else acc + part
    y = x + gt2 * acc
    if final:
        y = _rms(y) * fn_ref[...]
    o_ref[...] = y


def _ffn(layer, x, mod, norm2, w1, w2, final_norm, final):
    b, s, d = x.shape
    rows = FFN_ROWS
    d_ff = w1.shape[2]
    row_spec = pl.BlockSpec((None, rows, d), lambda i, j: (i, j, 0))
    lres = functools.partial(_layer_resident, layer)
    resident = 2 * 2 * d * d_ff + 4 * 2 * d
    streamed = 4 * 2 * rows * d + 4 * N_MOD * d
    temps = rows * (2 * d + 6 * FF_BLOCK + 3 * 4 * d)
    return pl.pallas_call(
        functools.partial(_ffn_kernel, final=final),
        grid=(b, s // rows),
        in_specs=[
            row_spec,
            pl.BlockSpec((None, None, N_MOD, d), lambda i, j: (layer, i, 0, 0)),
            lres((1, d)), lres((d, d_ff)), lres((d_ff, d)), _resident((1, d)),
        ],
        out_specs=row_spec,
        out_shape=jax.ShapeDtypeStruct((b, s, d), F32),
        compiler_params=pltpu.CompilerParams(
            dimension_semantics=("arbitrary", "arbitrary"),
            vmem_limit_bytes=_vmem_limit(resident, streamed, 0, temps)),
        name="channel_mlp",
    )(x, mod, norm2, w1, w2, final_norm)


def _retention_decay_tables(rows):
    log_gamma = jnp.log1p(-jnp.power(2.0, -5.0 - jnp.arange(RET_HEADS, dtype=F32)))
    pos = jnp.arange(CHUNK, dtype=F32)
    rel = pos[:, None] - pos[None, :]
    causal = rel >= 0
    intra = jnp.where(causal[None], jnp.exp(log_gamma[:, None, None] * jnp.where(causal, rel, 0.0)[None]), 0.0)
    decay_q = jnp.exp(log_gamma[:, None] * (pos + 1.0)[None])
    decay_k = jnp.exp(log_gamma[:, None] * (CHUNK - 1.0 - pos)[None])
    decay_chunk = jnp.exp(log_gamma * CHUNK)
    reps = rows // CHUNK
    widen = lambda t: jnp.broadcast_to(jnp.tile(t, (1, reps))[:, :, None], (RET_HEADS, rows, RET_QK_DIM))
    return intra, widen(decay_q), widen(decay_k), decay_chunk


def kernel(x, c, positions, w_ada, b_ada, norm1, norm2, w_in, ws_gmlp, bs_gmlp, vnorm_gmlp, w_pool, b_pool,
           pool_scale, w_branch, w_out, w_ff1, w_ff2, final_norm):
    depth = w_in.shape[0]
    b, s, d = x.shape
    assert s % MIXER_ROWS == 0 and s % FFN_ROWS == 0 and MIXER_ROWS % CHUNK == 0
    assert w_ff1.shape[2] % FF_BLOCK == 0 and w_ada.shape[2] % ADA_COLS == 0

    mod = _ada_rows(c, w_ada, b_ada).reshape(depth, b, N_MOD, d)
    cos_t, sin_t = _rope_tables(positions)
    dintra, dq, dk, dchunk = _retention_decay_tables(MIXER_ROWS)
    rows_of = lambda v: v.reshape(depth, 1, d)
    norm1, norm2, vnorm, b_pool, pool_scale = map(rows_of, (norm1, norm2, vnorm_gmlp, b_pool, pool_scale))
    bs_t = jnp.swapaxes(bs_gmlp, 1, 2)
    w_in, w_branch, w_out, w_pool, w_ff1, w_ff2 = (
        w.astype(BF16) for w in (w_in, w_branch, w_out, w_pool, w_ff1, w_ff2))

    for l in range(depth):
        x = _mixer(l, x, mod, norm1, cos_t, sin_t, w_in, w_branch, w_out, w_pool, ws_gmlp, bs_t, vnorm,
                   b_pool, pool_scale, dintra, dq, dk, dchunk)
        x = _ffn(l, x, mod, norm2, w_ff1, w_ff2, final_norm.reshape(1, d), final=(l == depth - 1))
    return x
```

```python
import functools

import jax
import jax.numpy as jnp
from jax import lax
from jax.experimental import pallas as pl
from jax.experimental.pallas import tpu as pltpu

CHUNK = 128
RET_HEADS = 4
RET_QK_DIM = 128
RET_V_DIM = 256
GMLP_GROUPS = 4
POOL_WINDOWS = (2, 4, 8, 16)
POOL_HISTORY = 16
N_BRANCH = 3
N_MOD = 6
ROPE_BASE = 10000.0
EPS = 1e-6

VMEM_BYTES_V7X = 64 * 1024 * 1024
BF16_ROW_TILE = 16

MIXER_ROWS = 256
FFN_ROWS = 1024
FF_BLOCK = 1024
ROPE_ROWS = 2048
ADA_COLS = 1024

F32 = jnp.float32
BF16 = jnp.bfloat16


def _dot(a, b):
    return jnp.dot(a, b, preferred_element_type=F32)


def _rms(x):
    return x * lax.rsqrt(jnp.mean(x * x, axis=-1, keepdims=True) + EPS)


def _resident(shape):
    zeros = (0,) * len(shape)
    return pl.BlockSpec(shape, lambda *_: zeros, pipeline_mode=pl.Buffered(1))


def _layer_resident(layer, shape):
    index = (layer,) + (0,) * len(shape)
    return pl.BlockSpec((None,) + tuple(shape), lambda *_: index, pipeline_mode=pl.Buffered(1))


def _vmem_limit(resident_bytes, streamed_bytes, scratch_bytes, temp_bytes):
    need = resident_bytes + 2 * streamed_bytes + scratch_bytes + temp_bytes
    return min(int(need), VMEM_BYTES_V7X - 4 * 1024 * 1024)


def _ada_kernel(c_ref, w_ref, b_ref, o_ref):
    c = c_ref[...]
    c_act = (c * jax.nn.sigmoid(c)).astype(BF16)
    o_ref[0] = _dot(c_act, w_ref[0].astype(BF16)) + b_ref[0]


def _ada_rows(c, w_ada, b_ada):
    depth, d, n = w_ada.shape
    b = c.shape[0]
    return pl.pallas_call(
        _ada_kernel,
        grid=(depth, n // ADA_COLS),
        in_specs=[
            pl.BlockSpec((b, d), lambda l, j: (0, 0)),
            pl.BlockSpec((1, d, ADA_COLS), lambda l, j: (l, 0, j)),
            pl.BlockSpec((1, 1, ADA_COLS), lambda l, j: (l, 0, j)),
        ],
        out_specs=pl.BlockSpec((1, b, ADA_COLS), lambda l, j: (l, 0, j)),
        out_shape=jax.ShapeDtypeStruct((depth, b, n), F32),
        compiler_params=pltpu.CompilerParams(
            dimension_semantics=("arbitrary", "arbitrary")),
        name="ada_rows",
    )(c, w_ada, b_ada.reshape(depth, 1, n))


def _rope_kernel(pos_ref, freq_ref, sign_ref, cos_ref, sin_ref):
    ang = pos_ref[0].astype(F32) * freq_ref[...]
    cos_ref[0] = jnp.cos(ang)
    sin_ref[0] = jnp.sin(ang) * sign_ref[...]


def _rope_tables(positions):
    b, s = positions.shape
    half = RET_QK_DIM // 2
    inv_freq = ROPE_BASE ** (-jnp.arange(half, dtype=F32) / half)
    freq = jnp.concatenate([inv_freq, inv_freq]).reshape(1, RET_QK_DIM)
    sign = jnp.concatenate([-jnp.ones((half,), F32), jnp.ones((half,), F32)]).reshape(1, RET_QK_DIM)
    rows = min(ROPE_ROWS, s)
    out = jax.ShapeDtypeStruct((b, s, RET_QK_DIM), F32)
    return pl.pallas_call(
        _rope_kernel,
        grid=(b, s // rows),
        in_specs=[
            pl.BlockSpec((1, rows, 1), lambda i, j: (i, j, 0)),
            pl.BlockSpec((1, RET_QK_DIM), lambda i, j: (0, 0)),
            pl.BlockSpec((1, RET_QK_DIM), lambda i, j: (0, 0)),
        ],
        out_specs=[pl.BlockSpec((1, rows, RET_QK_DIM), lambda i, j: (i, j, 0))] * 2,
        out_shape=[out, out],
        compiler_params=pltpu.CompilerParams(
            dimension_semantics=("arbitrary", "arbitrary")),
        name="rope_tables",
    )(positions.reshape(b, s, 1), freq, sign)


def _gelu_tanh(x):
    return 0.5 * x * (1.0 + jnp.tanh(0.7978845608028654 * (x + 0.044715 * (x * x * x))))


N_MIXER_INPUTS = 18


def _mixer_kernel(*refs, n_cast):
    ins = refs[:N_MIXER_INPUTS]
    cast_in = refs[N_MIXER_INPUTS:N_MIXER_INPUTS + n_cast]
    o_ref = refs[N_MIXER_INPUTS + n_cast]
    cast_out = refs[N_MIXER_INPUTS + n_cast + 1:N_MIXER_INPUTS + 2 * n_cast + 1]
    scratch = refs[N_MIXER_INPUTS + 2 * n_cast + 1:]
    for src_ref, dst_ref in zip(cast_in, cast_out):
        dst_ref[...] = src_ref[...].astype(BF16)
    _mixer_step(*ins, o_ref, *scratch)


def _mixer_step(dchunk_ref, x_ref, mod_ref, n1_ref, cos_ref, sin_ref, win_ref, wbr_ref, wout_ref,
                wpool_ref, ws_ref, bst_ref, vnorm_ref, bpool_ref, pscale_ref,
                dintra_ref, dq_ref, dk_ref,
                o_ref,
                state_ref, ptail_ref, qb_ref, qd_ref, kb_ref, kd_ref, vb_ref, sg_ref,
                u_ref, vn_ref, y_ref, m_ref):
    rows, d = x_ref.shape
    n_chunks = rows // CHUNK
    qk_w = RET_HEADS * RET_QK_DIM
    v_w = RET_HEADS * RET_V_DIM
    gdim = d // GMLP_GROUPS
    pdim = d // len(POOL_WINDOWS)
    o_q, o_k, o_v = 0, qk_w, 2 * qk_w
    o_g = o_v + v_w
    o_u = o_g + v_w
    o_vs = o_u + d
    o_p = o_vs + d
    o_gate = o_p + d
    seq_step = pl.program_id(1)
    row_slices = [slice(c * CHUNK, (c + 1) * CHUNK) for c in range(n_chunks)]
    qk_slices = [slice(hd * RET_QK_DIM, (hd + 1) * RET_QK_DIM) for hd in range(RET_HEADS)]
    v_slices = [slice(hd * RET_V_DIM, (hd + 1) * RET_V_DIM) for hd in range(RET_HEADS)]

    @pl.when(seq_step == 0)
    def _():
        state_ref[...] = jnp.zeros_like(state_ref)
        ptail_ref[...] = jnp.zeros_like(ptail_ref)

    x = x_ref[...]
    sh1, sc1, gt1 = mod_ref[0:1, :], mod_ref[1:2, :], mod_ref[2:3, :]
    h = ((_rms(x) * n1_ref[...]) * (1.0 + sc1) + sh1).astype(BF16)

    def proj(lo, width):
        return _dot(h, win_ref[:, lo:lo + width])

    def gate(n):
        return jax.nn.sigmoid(proj(o_gate + n * d, d))

    cosv, sinv = cos_ref[...], sin_ref[...]
    qf, kf = proj(o_q, qk_w), proj(o_k, qk_w)
    k_scale = RET_QK_DIM ** -0.5
    for hd, sl in enumerate(qk_slices):
        q_h, k_h = qf[:, sl], kf[:, sl]
        q_r = q_h * cosv + pltpu.roll(q_h, RET_QK_DIM // 2, 1) * sinv
        k_r = (k_h * cosv + pltpu.roll(k_h, RET_QK_DIM // 2, 1) * sinv) * k_scale
        qb_ref[:, sl] = q_r.astype(BF16)
        qd_ref[:, sl] = (q_r * dq_ref[hd]).astype(BF16)
        kb_ref[:, sl] = k_r.astype(BF16)
        kd_ref[:, sl] = (k_r * dk_ref[hd]).astype(BF16)
    vb_ref[...] = proj(o_v, v_w).astype(BF16)
    g = proj(o_g, v_w)
    sg_ref[...] = g * jax.nn.sigmoid(g)
    vs = _gelu_tanh(proj(o_vs, d))
    vn_ref[...] = (_rms(vs) * vnorm_ref[...]).astype(BF16)

    scores, kv = {}, {}
    for c, rs in enumerate(row_slices):
        for hd, sl in enumerate(qk_slices):
            s = lax.dot_general(qb_ref[rs, sl], kb_ref[rs, sl], (((1,), (1,)), ((), ())),
                                preferred_element_type=F32)
            scores[c, hd] = (s * dintra_ref[hd]).astype(BF16)
            kv[c, hd] = lax.dot_general(kd_ref[rs, sl], vb_ref[rs, v_slices[hd]], (((0,), (0,)), ((), ())),
                                        preferred_element_type=F32)

    u_ref[...] = _gelu_tanh(proj(o_u, d))

    state = [state_ref[hd] for hd in range(RET_HEADS)]
    ret = {}
    for c, rs in enumerate(row_slices):
        for hd, sl in enumerate(qk_slices):
            lhs = jnp.concatenate([scores[c, hd], qd_ref[rs, sl]], axis=1)
            rhs = jnp.concatenate([vb_ref[rs, v_slices[hd]], state[hd].astype(BF16)], axis=0)
            ret[c, hd] = _dot(lhs, rhs)
            state[hd] = dchunk_ref[hd] * state[hd] + kv[c, hd]
    for hd in range(RET_HEADS):
        state_ref[hd] = state[hd]

    p = proj(o_p, d)
    ext = jnp.concatenate([ptail_ref[...], p], axis=0)
    ptail_ref[...] = p[rows - POOL_HISTORY:, :]
    pos = seq_step * rows + lax.broadcasted_iota(jnp.int32, (rows, pdim), 0)
    pooled = []
    for gi, window in enumerate(POOL_WINDOWS):
        psl = slice(gi * pdim, (gi + 1) * pdim)
        acc = ext[:, psl]
        span = 1
        while span < window:
            acc = acc + pltpu.roll(acc, span, 0)
            span *= 2
        count = jnp.minimum(pos + 1, window).astype(F32)
        pooled.append((acc[POOL_HISTORY:, :] / count - p[:, psl]).astype(BF16))

    for c, rs in enumerate(row_slices):
        for hd, vsl in enumerate(v_slices):
            y_ref[0, rs, vsl] = (sg_ref[rs, vsl] * _rms(ret[c, hd])).astype(BF16)

    tri = lax.broadcasted_iota(jnp.int32, (CHUNK, CHUNK), 0) >= lax.broadcasted_iota(jnp.int32, (CHUNK, CHUNK), 1)
    for gi in range(GMLP_GROUPS):
        gsl = slice(gi * gdim, (gi + 1) * gdim)
        w_tri = jnp.where(tri, ws_ref[gi], 0.0).astype(BF16)
        bias = jnp.broadcast_to(bst_ref[:, gi:gi + 1], (CHUNK, gdim))
        for rs in row_slices:
            mixed = _dot(w_tri, vn_ref[rs, gsl]) + bias
            y_ref[1, rs, gsl] = (u_ref[rs, gsl] * mixed).astype(BF16)

    m_ref[...] = gate(0) * _dot(y_ref[0], wbr_ref[0])

    for gi in range(len(POOL_WINDOWS)):
        psl = slice(gi * pdim, (gi + 1) * pdim)
        mixed = _dot(pooled[gi], wpool_ref[gi]) + bpool_ref[:, psl]
        y_ref[2, :, psl] = (mixed * pscale_ref[:, psl]).astype(BF16)

    m_ref[...] += gate(1) * _dot(y_ref[1], wbr_ref[1])
    m_ref[...] += gate(2) * _dot(y_ref[2], wbr_ref[2])
    o_ref[...] = x + gt1 * _dot(m_ref[...].astype(BF16), wout_ref[...])


def _mixer(layer, x, mod, norm1, cos_t, sin_t, w_in, w_branch, w_out, w_pool, ws, bs_t, vnorm, b_pool,
           pool_scale, dintra, dq, dk, dchunk, cast_jobs):
    b, s, d = x.shape
    rows = MIXER_ROWS
    steps = s // rows
    d_in = w_in.shape[1]
    v_w = RET_HEADS * RET_V_DIM
    qk_w = RET_HEADS * RET_QK_DIM
    row_spec = lambda width: pl.BlockSpec((None, rows, width), lambda i, j: (i, j, 0))
    lres = functools.partial(_layer_resident, layer)
    in_specs = [
        pl.BlockSpec(memory_space=pltpu.SMEM),
        row_spec(d),
        pl.BlockSpec((None, None, N_MOD, d), lambda i, j: (layer, i, 0, 0)),
        lres((1, d)),
        row_spec(RET_QK_DIM), row_spec(RET_QK_DIM),
        _resident((d, d_in)), _resident((N_BRANCH, d, d)), _resident((d, d)),
        _resident(w_pool.shape), lres(ws.shape[1:]), lres(bs_t.shape[1:]),
        lres((1, d)), lres((1, d)), lres((1, d)),
        _resident(dintra.shape), _resident(dq.shape), _resident(dk.shape),
    ]
    assert len(in_specs) == N_MIXER_INPUTS
    out_specs = [row_spec(d)]
    out_shape = [jax.ShapeDtypeStruct((b, s, d), F32)]
    cast_bytes = 0
    for w, src_layer in cast_jobs:
        _, r, c = w.shape
        blk = r // (b * steps)
        assert blk * b * steps == r and blk % BF16_ROW_TILE == 0
        in_specs.append(pl.BlockSpec((None, blk, c), lambda i, j, src_layer=src_layer: (src_layer, i * steps + j, 0)))
        out_specs.append(pl.BlockSpec((blk, c), lambda i, j: (i * steps + j, 0)))
        out_shape.append(jax.ShapeDtypeStruct((r, c), BF16))
        cast_bytes += blk * c * (4 + 2)
    scratch = [
        pltpu.VMEM((RET_HEADS, RET_QK_DIM, RET_V_DIM), F32),
        pltpu.VMEM((POOL_HISTORY, d), F32),
        pltpu.VMEM((rows, qk_w), BF16), pltpu.VMEM((rows, qk_w), BF16),
        pltpu.VMEM((rows, qk_w), BF16), pltpu.VMEM((rows, qk_w), BF16),
        pltpu.VMEM((rows, v_w), BF16),
        pltpu.VMEM((rows, v_w), F32),
        pltpu.VMEM((rows, d), F32),
        pltpu.VMEM((rows, d), BF16),
        pltpu.VMEM((N_BRANCH, rows, d), BF16),
        pltpu.VMEM((rows, d), F32),
    ]
    resident = 2 * (d * d_in + N_BRANCH * d * d + d * d + w_pool.size) + 4 * (
        ws[0].size + bs_t[0].size + dintra.size + dq.size + dk.size + 6 * d)
    streamed = 4 * rows * (2 * d + 2 * RET_QK_DIM) + 4 * N_MOD * d + cast_bytes
    scratch_bytes = 4 * (RET_HEADS * RET_QK_DIM * RET_V_DIM + POOL_HISTORY * d) + rows * (
        2 * 4 * qk_w + 2 * v_w + 4 * v_w + 4 * d + 2 * d + 2 * N_BRANCH * d + 4 * d)
    temps = 8 * 4 * rows * d
    return pl.pallas_call(
        functools.partial(_mixer_kernel, n_cast=len(cast_jobs)),
        grid=(b, steps),
        in_specs=in_specs,
        out_specs=out_specs,
        out_shape=out_shape,
        scratch_shapes=scratch,
        compiler_params=pltpu.CompilerParams(
            dimension_semantics=("arbitrary", "arbitrary"),
            vmem_limit_bytes=_vmem_limit(resident, streamed, scratch_bytes, temps)),
        name="token_mixer",
    )(dchunk, x, mod, norm1, cos_t, sin_t, w_in, w_branch, w_out, w_pool, ws, bs_t, vnorm, b_pool,
      pool_scale, dintra, dq, dk, *[w for w, _ in cast_jobs])


def _ffn_kernel(x_ref, mod_ref, n2_ref, w1_ref, w2_ref, fn_ref, o_ref, *, final):
    x = x_ref[...]
    sh2, sc2, gt2 = mod_ref[3:4, :], mod_ref[4:5, :], mod_ref[5:6, :]
    h = ((_rms(x) * n2_ref[...]) * (1.0 + sc2) + sh2).astype(BF16)
    d_ff = w1_ref.shape[1]
    acc = None
    for lo in range(0, d_ff, FF_BLOCK):
        hid = jnp.square(jnp.maximum(_dot(h, w1_ref[:, lo:lo + FF_BLOCK]), 0.0)).astype(BF16)
        part = _dot(hid, w2_ref[lo:lo + FF_BLOCK, :])
        acc = part if acc is None else acc + part
    y = x + gt2 * acc
    if final:
        y = _rms(y) * fn_ref[...]
    o_ref[...] = y


def _ffn(layer, x, mod, norm2, w1, w2, final_norm, final):
    b, s, d = x.shape
    rows = FFN_ROWS
    d_ff = w1.shape[1]
    row_spec = pl.BlockSpec((None, rows, d), lambda i, j: (i, j, 0))
    lres = functools.partial(_layer_resident, layer)
    resident = 2 * 2 * d * d_ff + 4 * 2 * d
    streamed = 4 * 2 * rows * d + 4 * N_MOD * d
    temps = rows * (2 * d + 6 * FF_BLOCK + 3 * 4 * d)
    return pl.pallas_call(
        functools.partial(_ffn_kernel, final=final),
        grid=(b, s // rows),
        in_specs=[
            row_spec,
            pl.BlockSpec((None, None, N_MOD, d), lambda i, j: (layer, i, 0, 0)),
            lres((1, d)), _resident((d, d_ff)), _resident((d_ff, d)), _resident((1, d)),
        ],
        out_specs=row_spec,
        out_shape=jax.ShapeDtypeStruct((b, s, d), F32),
        compiler_params=pltpu.CompilerParams(
            dimension_semantics=("arbitrary", "arbitrary"),
            vmem_limit_bytes=_vmem_limit(resident, streamed, 0, temps)),
        name="channel_mlp",
    )(x, mod, norm2, w1, w2, final_norm)


def _retention_decay_tables(rows):
    log_gamma = jnp.log1p(-jnp.power(2.0, -5.0 - jnp.arange(RET_HEADS, dtype=F32)))
    pos = jnp.arange(CHUNK, dtype=F32)
    rel = pos[:, None] - pos[None, :]
    causal = rel >= 0
    intra = jnp.where(causal[None], jnp.exp(log_gamma[:, None, None] * jnp.where(causal, rel, 0.0)[None]), 0.0)
    decay_q = jnp.exp(log_gamma[:, None] * (pos + 1.0)[None])
    decay_k = jnp.exp(log_gamma[:, None] * (CHUNK - 1.0 - pos)[None])
    decay_chunk = jnp.exp(log_gamma * CHUNK)
    reps = rows // CHUNK
    widen = lambda t: jnp.broadcast_to(jnp.tile(t, (1, reps))[:, :, None], (RET_HEADS, rows, RET_QK_DIM))
    return intra, widen(decay_q), widen(decay_k), decay_chunk


def kernel(x, c, positions, w_ada, b_ada, norm1, norm2, w_in, ws_gmlp, bs_gmlp, vnorm_gmlp, w_pool, b_pool,
           pool_scale, w_branch, w_out, w_ff1, w_ff2, final_norm):
    depth = w_in.shape[0]
    b, s, d = x.shape
    assert s % MIXER_ROWS == 0 and s % FFN_ROWS == 0 and MIXER_ROWS % CHUNK == 0
    assert w_ff1.shape[2] % FF_BLOCK == 0 and w_ada.shape[2] % ADA_COLS == 0

    mod = _ada_rows(c, w_ada, b_ada).reshape(depth, b, N_MOD, d)
    cos_t, sin_t = _rope_tables(positions)
    dintra, dq, dk, dchunk = _retention_decay_tables(MIXER_ROWS)
    rows_of = lambda v: v.reshape(depth, 1, d)
    norm1, norm2, vnorm, b_pool, pool_scale = map(rows_of, (norm1, norm2, vnorm_gmlp, b_pool, pool_scale))
    bs_t = jnp.swapaxes(bs_gmlp, 1, 2)

    mixer_w = (w_in, w_branch, w_out, w_pool)
    as_rows = lambda w: w.reshape(depth, -1, w.shape[-1])
    mixer_bf16 = [w[0].astype(BF16) for w in mixer_w]
    for l in range(depth):
        jobs = [(as_rows(w_ff1), l), (as_rows(w_ff2), l)]
        if l + 1 < depth:
            jobs += [(as_rows(w), l + 1) for w in mixer_w]
        x, ff1, ff2, *next_mixer = _mixer(l, x, mod, norm1, cos_t, sin_t, *mixer_bf16, ws_gmlp, bs_t, vnorm,
                                          b_pool, pool_scale, dintra, dq, dk, dchunk, jobs)
        x = _ffn(l, x, mod, norm2, ff1, ff2, final_norm.reshape(1, d), final=(l == depth - 1))
        mixer_bf16 = [w.reshape(ref.shape[1:]) for w, ref in zip(next_mixer, mixer_w)]
    return x
```

```python
import functools

import jax
import jax.numpy as jnp
from jax import lax
from jax.experimental import pallas as pl
from jax.experimental.pallas import tpu as pltpu

CHUNK = 128
RET_HEADS = 4
RET_QK_DIM = 128
RET_V_DIM = 256
GMLP_GROUPS = 4
POOL_WINDOWS = (2, 4, 8, 16)
POOL_HISTORY = 16
N_BRANCH = 3
N_MOD = 6
ROPE_BASE = 10000.0
EPS = 1e-6

VMEM_BYTES_V7X = 64 * 1024 * 1024
BF16_ROW_TILE = 16

MIXER_ROWS = 256
FFN_ROWS = 1024
FF_BLOCK = 1024
ROPE_ROWS = 512
ADA_COLS = 1024

F32 = jnp.float32
BF16 = jnp.bfloat16


def _dot(a, b):
    return jnp.dot(a, b, preferred_element_type=F32)


def _rms(x):
    return x * lax.rsqrt(jnp.mean(x * x, axis=-1, keepdims=True) + EPS)


def _resident(shape):
    zeros = (0,) * len(shape)
    return pl.BlockSpec(shape, lambda *_: zeros, pipeline_mode=pl.Buffered(1))


def _layer_resident(layer, shape):
    index = (layer,) + (0,) * len(shape)
    return pl.BlockSpec((None,) + tuple(shape), lambda *_: index, pipeline_mode=pl.Buffered(1))


def _vmem_limit(resident_bytes, streamed_bytes, scratch_bytes, temp_bytes):
    need = resident_bytes + 2 * streamed_bytes + scratch_bytes + temp_bytes
    return min(int(need), VMEM_BYTES_V7X - 4 * 1024 * 1024)


def _ada_kernel(c_ref, w_ref, b_ref, o_ref):
    c = c_ref[...]
    c_act = (c * jax.nn.sigmoid(c)).astype(BF16)
    o_ref[0] = _dot(c_act, w_ref[0].astype(BF16)) + b_ref[0]


def _ada_rows(c, w_ada, b_ada):
    depth, d, n = w_ada.shape
    b = c.shape[0]
    return pl.pallas_call(
        _ada_kernel,
        grid=(depth, n // ADA_COLS),
        in_specs=[
            pl.BlockSpec((b, d), lambda l, j: (0, 0)),
            pl.BlockSpec((1, d, ADA_COLS), lambda l, j: (l, 0, j)),
            pl.BlockSpec((1, 1, ADA_COLS), lambda l, j: (l, 0, j)),
        ],
        out_specs=pl.BlockSpec((1, b, ADA_COLS), lambda l, j: (l, 0, j)),
        out_shape=jax.ShapeDtypeStruct((depth, b, n), F32),
        compiler_params=pltpu.CompilerParams(
            dimension_semantics=("arbitrary", "arbitrary")),
        name="ada_rows",
    )(c, w_ada, b_ada.reshape(depth, 1, n))


def _round_blocks(cast_in, cast_out):
    for src_ref, dst_ref in zip(cast_in, cast_out):
        dst_ref[...] = src_ref[...].astype(BF16)


def _round_block_specs(cast_jobs, n_steps, flat_step):
    in_specs, out_specs, out_shapes, nbytes = [], [], [], 0
    for w, layer in cast_jobs:
        _, r, c = w.shape
        blk = r // n_steps
        assert blk * n_steps == r and blk % BF16_ROW_TILE == 0
        in_specs.append(pl.BlockSpec((None, blk, c), lambda i, j, layer=layer: (layer, flat_step(i, j), 0)))
        out_specs.append(pl.BlockSpec((blk, c), lambda i, j: (flat_step(i, j), 0)))
        out_shapes.append(jax.ShapeDtypeStruct((r, c), BF16))
        nbytes += blk * c * (4 + 2)
    return in_specs, out_specs, out_shapes, nbytes


N_ROPE_INPUTS = 3


def _rope_kernel(*refs, n_cast):
    pos_ref, freq_ref, sign_ref = refs[:N_ROPE_INPUTS]
    cast_in = refs[N_ROPE_INPUTS:N_ROPE_INPUTS + n_cast]
    cos_ref, sin_ref = refs[N_ROPE_INPUTS + n_cast:N_ROPE_INPUTS + n_cast + 2]
    cast_out = refs[N_ROPE_INPUTS + n_cast + 2:]
    _round_blocks(cast_in, cast_out)

    rows = pos_ref.shape[0]
    half_rows, half = rows // 2, RET_QK_DIM // 2
    pos = pos_ref[...].astype(F32)
    low = lax.broadcasted_iota(jnp.int32, (half_rows, RET_QK_DIM), 1) < half
    ang = jnp.where(low, pos[:half_rows], pos[half_rows:]) * freq_ref[...]
    cos_p, sin_p = jnp.cos(ang), jnp.sin(ang)
    cos_s, sin_s = pltpu.roll(cos_p, half, 1), pltpu.roll(sin_p, half, 1)
    sign = sign_ref[...]
    cos_ref[:half_rows, :] = jnp.where(low, cos_p, cos_s)
    cos_ref[half_rows:, :] = jnp.where(low, cos_s, cos_p)
    sin_ref[:half_rows, :] = jnp.where(low, sin_p, sin_s) * sign
    sin_ref[half_rows:, :] = jnp.where(low, sin_s, sin_p) * sign


def _rope_tables(positions, cast_jobs):
    b, s = positions.shape
    half = RET_QK_DIM // 2
    inv_freq = ROPE_BASE ** (-jnp.arange(half, dtype=F32) / half)
    freq = jnp.concatenate([inv_freq, inv_freq]).reshape(1, RET_QK_DIM)
    sign = jnp.concatenate([-jnp.ones((half,), F32), jnp.ones((half,), F32)]).reshape(1, RET_QK_DIM)
    rows = ROPE_ROWS
    steps = s // rows
    cast_in, cast_out, cast_shapes, cast_bytes = _round_block_specs(cast_jobs, b * steps, lambda i, j: i * steps + j)
    table = jax.ShapeDtypeStruct((b, s, RET_QK_DIM), F32)
    table_spec = pl.BlockSpec((None, rows, RET_QK_DIM), lambda i, j: (i, j, 0))
    return pl.pallas_call(
        functools.partial(_rope_kernel, n_cast=len(cast_jobs)),
        grid=(b, steps),
        in_specs=[
            pl.BlockSpec((None, rows, 1), lambda i, j: (i, j, 0)),
            pl.BlockSpec((1, RET_QK_DIM), lambda i, j: (0, 0)),
            pl.BlockSpec((1, RET_QK_DIM), lambda i, j: (0, 0)),
        ] + cast_in,
        out_specs=[table_spec, table_spec] + cast_out,
        out_shape=[table, table] + cast_shapes,
        compiler_params=pltpu.CompilerParams(
            dimension_semantics=("arbitrary", "arbitrary"),
            vmem_limit_bytes=_vmem_limit(0, cast_bytes + 4 * rows * 3 * RET_QK_DIM, 0, 16 * 4 * rows * RET_QK_DIM)),
        name="rope_tables",
    )(positions.reshape(b, s, 1), freq, sign, *[w for w, _ in cast_jobs])


def _gelu_tanh(x):
    return 0.5 * x * (1.0 + jnp.tanh(0.7978845608028654 * (x + 0.044715 * (x * x * x))))


N_MIXER_INPUTS = 18


def _mixer_kernel(*refs, n_cast):
    ins = refs[:N_MIXER_INPUTS]
    cast_in = refs[N_MIXER_INPUTS:N_MIXER_INPUTS + n_cast]
    o_ref = refs[N_MIXER_INPUTS + n_cast]
    cast_out = refs[N_MIXER_INPUTS + n_cast + 1:N_MIXER_INPUTS + 2 * n_cast + 1]
    scratch = refs[N_MIXER_INPUTS + 2 * n_cast + 1:]
    _round_blocks(cast_in, cast_out)
    _mixer_step(*ins, o_ref, *scratch)


def _mixer_step(dchunk_ref, x_ref, mod_ref, n1_ref, cos_ref, sin_ref, win_ref, wbr_ref, wout_ref,
                wpool_ref, ws_ref, bst_ref, vnorm_ref, bpool_ref, pscale_ref,
                dintra_ref, dq_ref, dk_ref,
                o_ref,
                state_ref, ptail_ref, qb_ref, qd_ref, kb_ref, kd_ref, vb_ref, sg_ref,
                u_ref, vn_ref, y_ref, m_ref):
    rows, d = x_ref.shape
    n_chunks = rows // CHUNK
    qk_w = RET_HEADS * RET_QK_DIM
    v_w = RET_HEADS * RET_V_DIM
    gdim = d // GMLP_GROUPS
    pdim = d // len(POOL_WINDOWS)
    o_q, o_k, o_v = 0, qk_w, 2 * qk_w
    o_g = o_v + v_w
    o_u = o_g + v_w
    o_vs = o_u + d
    o_p = o_vs + d
    o_gate = o_p + d
    seq_step = pl.program_id(1)
    row_slices = [slice(c * CHUNK, (c + 1) * CHUNK) for c in range(n_chunks)]
    qk_slices = [slice(hd * RET_QK_DIM, (hd + 1) * RET_QK_DIM) for hd in range(RET_HEADS)]
    v_slices = [slice(hd * RET_V_DIM, (hd + 1) * RET_V_DIM) for hd in range(RET_HEADS)]

    @pl.when(seq_step == 0)
    def _():
        state_ref[...] = jnp.zeros_like(state_ref)
        ptail_ref[...] = jnp.zeros_like(ptail_ref)

    x = x_ref[...]
    sh1, sc1, gt1 = mod_ref[0:1, :], mod_ref[1:2, :], mod_ref[2:3, :]
    h = ((_rms(x) * n1_ref[...]) * (1.0 + sc1) + sh1).astype(BF16)

    def proj(lo, width):
        return _dot(h, win_ref[:, lo:lo + width])

    def gate(n):
        return jax.nn.sigmoid(proj(o_gate + n * d, d))

    cosv, sinv = cos_ref[...], sin_ref[...]
    qf, kf = proj(o_q, qk_w), proj(o_k, qk_w)
    k_scale = RET_QK_DIM ** -0.5
    for hd, sl in enumerate(qk_slices):
        q_h, k_h = qf[:, sl], kf[:, sl]
        q_r = q_h * cosv + pltpu.roll(q_h, RET_QK_DIM // 2, 1) * sinv
        k_r = (k_h * cosv + pltpu.roll(k_h, RET_QK_DIM // 2, 1) * sinv) * k_scale
        qb_ref[:, sl] = q_r.astype(BF16)
        qd_ref[:, sl] = (q_r * dq_ref[hd]).astype(BF16)
        kb_ref[:, sl] = k_r.astype(BF16)
        kd_ref[:, sl] = (k_r * dk_ref[hd]).astype(BF16)
    vb_ref[...] = proj(o_v, v_w).astype(BF16)
    g = proj(o_g, v_w)
    sg_ref[...] = g * jax.nn.sigmoid(g)
    vs = _gelu_tanh(proj(o_vs, d))
    vn_ref[...] = (_rms(vs) * vnorm_ref[...]).astype(BF16)

    scores, kv = {}, {}
    for c, rs in enumerate(row_slices):
        for hd, sl in enumerate(qk_slices):
            s = lax.dot_general(qb_ref[rs, sl], kb_ref[rs, sl], (((1,), (1,)), ((), ())),
                                preferred_element_type=F32)
            scores[c, hd] = (s * dintra_ref[hd]).astype(BF16)
            kv[c, hd] = lax.dot_general(kd_ref[rs, sl], vb_ref[rs, v_slices[hd]], (((0,), (0,)), ((), ())),
                                        preferred_element_type=F32)

    u_ref[...] = _gelu_tanh(proj(o_u, d))

    state = [state_ref[hd] for hd in range(RET_HEADS)]
    ret = {}
    for c, rs in enumerate(row_slices):
        for hd, sl in enumerate(qk_slices):
            lhs = jnp.concatenate([scores[c, hd], qd_ref[rs, sl]], axis=1)
            rhs = jnp.concatenate([vb_ref[rs, v_slices[hd]], state[hd].astype(BF16)], axis=0)
            ret[c, hd] = _dot(lhs, rhs)
            state[hd] = dchunk_ref[hd] * state[hd] + kv[c, hd]
    for hd in range(RET_HEADS):
        state_ref[hd] = state[hd]

    p = proj(o_p, d)
    ext = jnp.concatenate([ptail_ref[...], p], axis=0)
    ptail_ref[...] = p[rows - POOL_HISTORY:, :]
    pos = seq_step * rows + lax.broadcasted_iota(jnp.int32, (rows, pdim), 0)
    pooled = []
    for gi, window in enumerate(POOL_WINDOWS):
        psl = slice(gi * pdim, (gi + 1) * pdim)
        acc = ext[:, psl]
        span = 1
        while span < window:
            acc = acc + pltpu.roll(acc, span, 0)
            span *= 2
        count = jnp.minimum(pos + 1, window).astype(F32)
        pooled.append((acc[POOL_HISTORY:, :] / count - p[:, psl]).astype(BF16))

    for c, rs in enumerate(row_slices):
        for hd, vsl in enumerate(v_slices):
            y_ref[0, rs, vsl] = (sg_ref[rs, vsl] * _rms(ret[c, hd])).astype(BF16)

    tri = lax.broadcasted_iota(jnp.int32, (CHUNK, CHUNK), 0) >= lax.broadcasted_iota(jnp.int32, (CHUNK, CHUNK), 1)
    for gi in range(GMLP_GROUPS):
        gsl = slice(gi * gdim, (gi + 1) * gdim)
        w_tri = jnp.where(tri, ws_ref[gi], 0.0).astype(BF16)
        bias = jnp.broadcast_to(bst_ref[:, gi:gi + 1], (CHUNK, gdim))
        for rs in row_slices:
            mixed = _dot(w_tri, vn_ref[rs, gsl]) + bias
            y_ref[1, rs, gsl] = (u_ref[rs, gsl] * mixed).astype(BF16)

    m_ref[...] = gate(0) * _dot(y_ref[0], wbr_ref[0])

    for gi in range(len(POOL_WINDOWS)):
        psl = slice(gi * pdim, (gi + 1) * pdim)
        mixed = _dot(pooled[gi], wpool_ref[gi]) + bpool_ref[:, psl]
        y_ref[2, :, psl] = (mixed * pscale_ref[:, psl]).astype(BF16)

    m_ref[...] += gate(1) * _dot(y_ref[1], wbr_ref[1])
    m_ref[...] += gate(2) * _dot(y_ref[2], wbr_ref[2])
    o_ref[...] = x + gt1 * _dot(m_ref[...].astype(BF16), wout_ref[...])


def _mixer(layer, x, mod, norm1, cos_t, sin_t, w_in, w_branch, w_out, w_pool, ws, bs_t, vnorm, b_pool,
           pool_scale, dintra, dq, dk, dchunk, cast_jobs):
    b, s, d = x.shape
    rows = MIXER_ROWS
    steps = s // rows
    d_in = w_in.shape[1]
    v_w = RET_HEADS * RET_V_DIM
    qk_w = RET_HEADS * RET_QK_DIM
    row_spec = lambda width: pl.BlockSpec((None, rows, width), lambda i, j: (i, j, 0))
    lres = functools.partial(_layer_resident, layer)
    in_specs = [
        pl.BlockSpec(memory_space=pltpu.SMEM),
        row_spec(d),
        pl.BlockSpec((None, None, N_MOD, d), lambda i, j: (layer, i, 0, 0)),
        lres((1, d)),
        row_spec(RET_QK_DIM), row_spec(RET_QK_DIM),
        _resident((d, d_in)), _resident((N_BRANCH, d, d)), _resident((d, d)),
        _resident(w_pool.shape), lres(ws.shape[1:]), lres(bs_t.shape[1:]),
        lres((1, d)), lres((1, d)), lres((1, d)),
        _resident(dintra.shape), _resident(dq.shape), _resident(dk.shape),
    ]
    assert len(in_specs) == N_MIXER_INPUTS
    cast_in, cast_out, cast_shapes, cast_bytes = _round_block_specs(cast_jobs, b * steps, lambda i, j: i * steps + j)
    in_specs += cast_in
    out_specs = [row_spec(d)] + cast_out
    out_shape = [jax.ShapeDtypeStruct((b, s, d), F32)] + cast_shapes
    scratch = [
        pltpu.VMEM((RET_HEADS, RET_QK_DIM, RET_V_DIM), F32),
        pltpu.VMEM((POOL_HISTORY, d), F32),
        pltpu.VMEM((rows, qk_w), BF16), pltpu.VMEM((rows, qk_w), BF16),
        pltpu.VMEM((rows, qk_w), BF16), pltpu.VMEM((rows, qk_w), BF16),
        pltpu.VMEM((rows, v_w), BF16),
        pltpu.VMEM((rows, v_w), F32),
        pltpu.VMEM((rows, d), F32),
        pltpu.VMEM((rows, d), BF16),
        pltpu.VMEM((N_BRANCH, rows, d), BF16),
        pltpu.VMEM((rows, d), F32),
    ]
    resident = 2 * (d * d_in + N_BRANCH * d * d + d * d + w_pool.size) + 4 * (
        ws[0].size + bs_t[0].size + dintra.size + dq.size + dk.size + 6 * d)
    streamed = 4 * rows * (2 * d + 2 * RET_QK_DIM) + 4 * N_MOD * d + cast_bytes
    scratch_bytes = 4 * (RET_HEADS * RET_QK_DIM * RET_V_DIM + POOL_HISTORY * d) + rows * (
        2 * 4 * qk_w + 2 * v_w + 4 * v_w + 4 * d + 2 * d + 2 * N_BRANCH * d + 4 * d)
    temps = 8 * 4 * rows * d
    return pl.pallas_call(
        functools.partial(_mixer_kernel, n_cast=len(cast_jobs)),
        grid=(b, steps),
        in_specs=in_specs,
        out_specs=out_specs,
        out_shape=out_shape,
        scratch_shapes=scratch,
        compiler_params=pltpu.CompilerParams(
            dimension_semantics=("arbitrary", "arbitrary"),
            vmem_limit_bytes=_vmem_limit(resident, streamed, scratch_bytes, temps)),
        name="token_mixer",
    )(dchunk, x, mod, norm1, cos_t, sin_t, w_in, w_branch, w_out, w_pool, ws, bs_t, vnorm, b_pool,
      pool_scale, dintra, dq, dk, *[w for w, _ in cast_jobs])


def _ffn_kernel(x_ref, mod_ref, n2_ref, w1_ref, w2_ref, fn_ref, o_ref, *, final):
    x = x_ref[...]
    sh2, sc2, gt2 = mod_ref[3:4, :], mod_ref[4:5, :], mod_ref[5:6, :]
    h = ((_rms(x) * n2_ref[...]) * (1.0 + sc2) + sh2).astype(BF16)
    d_ff = w1_ref.shape[1]
    acc = None
    for lo in range(0, d_ff, FF_BLOCK):
        hid = jnp.square(jnp.maximum(_dot(h, w1_ref[:, lo:lo + FF_BLOCK]), 0.0)).astype(BF16)
        part = _dot(hid, w2_ref[lo:lo + FF_BLOCK, :])
        acc = part if acc is None else acc + part
    y = x + gt2 * acc
    if final:
        y = _rms(y) * fn_ref[...]
    o_ref[...] = y


def _ffn(layer, x, mod, norm2, w1, w2, final_norm, final):
    b, s, d = x.shape
    rows = FFN_ROWS
    d_ff = w1.shape[1]
    row_spec = pl.BlockSpec((None, rows, d), lambda i, j: (i, j, 0))
    lres = functools.partial(_layer_resident, layer)
    resident = 2 * 2 * d * d_ff + 4 * 2 * d
    streamed = 4 * 2 * rows * d + 4 * N_MOD * d
    temps = rows * (2 * d + 6 * FF_BLOCK + 3 * 4 * d)
    return pl.pallas_call(
        functools.partial(_ffn_kernel, final=final),
        grid=(b, s // rows),
        in_specs=[
            row_spec,
            pl.BlockSpec((None, None, N_MOD, d), lambda i, j: (layer, i, 0, 0)),
            lres((1, d)), _resident((d, d_ff)), _resident((d_ff, d)), _resident((1, d)),
        ],
        out_specs=row_spec,
        out_shape=jax.ShapeDtypeStruct((b, s, d), F32),
        compiler_params=pltpu.CompilerParams(
            dimension_semantics=("arbitrary", "arbitrary"),
            vmem_limit_bytes=_vmem_limit(resident, streamed, 0, temps)),
        name="channel_mlp",
    )(x, mod, norm2, w1, w2, final_norm)


def _retention_decay_tables(rows):
    log_gamma = jnp.log1p(-jnp.power(2.0, -5.0 - jnp.arange(RET_HEADS, dtype=F32)))
    pos = jnp.arange(CHUNK, dtype=F32)
    rel = pos[:, None] - pos[None, :]
    causal = rel >= 0
    intra = jnp.where(causal[None], jnp.exp(log_gamma[:, None, None] * jnp.where(causal, rel, 0.0)[None]), 0.0)
    decay_q = jnp.exp(log_gamma[:, None] * (pos + 1.0)[None])
    decay_k = jnp.exp(log_gamma[:, None] * (CHUNK - 1.0 - pos)[None])
    decay_chunk = jnp.exp(log_gamma * CHUNK)
    reps = rows // CHUNK
    widen = lambda t: jnp.broadcast_to(jnp.tile(t, (1, reps))[:, :, None], (RET_HEADS, rows, RET_QK_DIM))
    return intra, widen(decay_q), widen(decay_k), decay_chunk


def kernel(x, c, positions, w_ada, b_ada, norm1, norm2, w_in, ws_gmlp, bs_gmlp, vnorm_gmlp, w_pool, b_pool,
           pool_scale, w_branch, w_out, w_ff1, w_ff2, final_norm):
    depth = w_in.shape[0]
    b, s, d = x.shape
    assert s % MIXER_ROWS == 0 and s % FFN_ROWS == 0 and s % ROPE_ROWS == 0 and MIXER_ROWS % CHUNK == 0
    assert w_ff1.shape[2] % FF_BLOCK == 0 and w_ada.shape[2] % ADA_COLS == 0

    mod = _ada_rows(c, w_ada, b_ada).reshape(depth, b, N_MOD, d)
    dintra, dq, dk, dchunk = _retention_decay_tables(MIXER_ROWS)
    rows_of = lambda v: v.reshape(depth, 1, d)
    norm1, norm2, vnorm, b_pool, pool_scale = map(rows_of, (norm1, norm2, vnorm_gmlp, b_pool, pool_scale))
    bs_t = jnp.swapaxes(bs_gmlp, 1, 2)

    mixer_w = (w_in, w_branch, w_out, w_pool)
    as_rows = lambda w: w.reshape(depth, -1, w.shape[-1])
    cos_t, sin_t, *next_mixer = _rope_tables(positions, [(as_rows(w), 0) for w in mixer_w])
    for l in range(depth):
        mixer_bf16 = [w.reshape(ref.shape[1:]) for w, ref in zip(next_mixer, mixer_w)]
        jobs = [(as_rows(w_ff1), l), (as_rows(w_ff2), l)]
        if l + 1 < depth:
            jobs += [(as_rows(w), l + 1) for w in mixer_w]
        x, ff1, ff2, *next_mixer = _mixer(l, x, mod, norm1, cos_t, sin_t, *mixer_bf16, ws_gmlp, bs_t, vnorm,
                                          b_pool, pool_scale, dintra, dq, dk, dchunk, jobs)
        x = _ffn(l, x, mod, norm2, ff1, ff2, final_norm.reshape(1, d), final=(l == depth - 1))
    return x
```

```python
import functools

import jax
import jax.numpy as jnp
from jax import lax
from jax.experimental import pallas as pl
from jax.experimental.pallas import tpu as pltpu

CHUNK = 128
RET_HEADS = 4
RET_QK_DIM = 128
RET_V_DIM = 256
GMLP_GROUPS = 4
POOL_WINDOWS = (2, 4, 8, 16)
POOL_HISTORY = 16
N_BRANCH = 3
N_MOD = 6
ROPE_BASE = 10000.0
EPS = 1e-6

VMEM_BYTES_V7X = 64 * 1024 * 1024
BF16_ROW_TILE = 16

MIXER_ROWS = 256
FFN_ROWS = 1024
FF_BLOCK = 1024
ROPE_ROWS = 2048
ADA_COLS = 1024

F32 = jnp.float32
BF16 = jnp.bfloat16


def _dot(a, b):
    return jnp.dot(a, b, preferred_element_type=F32)


def _rms(x):
    return x * lax.rsqrt(jnp.mean(x * x, axis=-1, keepdims=True) + EPS)


def _resident(shape):
    zeros = (0,) * len(shape)
    return pl.BlockSpec(shape, lambda *_: zeros, pipeline_mode=pl.Buffered(1))


def _layer_resident(layer, shape):
    index = (layer,) + (0,) * len(shape)
    return pl.BlockSpec((None,) + tuple(shape), lambda *_: index, pipeline_mode=pl.Buffered(1))


def _vmem_limit(resident_bytes, streamed_bytes, scratch_bytes, temp_bytes):
    need = resident_bytes + 2 * streamed_bytes + scratch_bytes + temp_bytes
    return min(int(need), VMEM_BYTES_V7X - 4 * 1024 * 1024)


def _ada_kernel(c_ref, w_ref, b_ref, o_ref):
    c = c_ref[...]
    c_act = (c * jax.nn.sigmoid(c)).astype(BF16)
    o_ref[0] = _dot(c_act, w_ref[0].astype(BF16)) + b_ref[0]


def _ada_rows(c, w_ada, b_ada):
    depth, d, n = w_ada.shape
    b = c.shape[0]
    return pl.pallas_call(
        _ada_kernel,
        grid=(depth, n // ADA_COLS),
        in_specs=[
            pl.BlockSpec((b, d), lambda l, j: (0, 0)),
            pl.BlockSpec((1, d, ADA_COLS), lambda l, j: (l, 0, j)),
            pl.BlockSpec((1, 1, ADA_COLS), lambda l, j: (l, 0, j)),
        ],
        out_specs=pl.BlockSpec((1, b, ADA_COLS), lambda l, j: (l, 0, j)),
        out_shape=jax.ShapeDtypeStruct((depth, b, n), F32),
        compiler_params=pltpu.CompilerParams(
            dimension_semantics=("arbitrary", "arbitrary")),
        name="ada_rows",
    )(c, w_ada, b_ada.reshape(depth, 1, n))


def _round_blocks(cast_in, cast_out):
    for src_ref, dst_ref in zip(cast_in, cast_out):
        dst_ref[...] = src_ref[...].astype(BF16)


def _round_block_specs(cast_jobs, n_steps, flat_step):
    in_specs, out_specs, out_shapes, nbytes = [], [], [], 0
    for w, layer in cast_jobs:
        _, r, c = w.shape
        blk = r // n_steps
        assert blk * n_steps == r and blk % BF16_ROW_TILE == 0
        in_specs.append(pl.BlockSpec((None, blk, c), lambda i, j, layer=layer: (layer, flat_step(i, j), 0)))
        out_specs.append(pl.BlockSpec((blk, c), lambda i, j: (flat_step(i, j), 0)))
        out_shapes.append(jax.ShapeDtypeStruct((r, c), BF16))
        nbytes += blk * c * (4 + 2)
    return in_specs, out_specs, out_shapes, nbytes


N_ROPE_INPUTS = 3


def _rope_kernel(*refs, n_cast):
    pos_ref, freq_ref, sign_ref = refs[:N_ROPE_INPUTS]
    cast_in = refs[N_ROPE_INPUTS:N_ROPE_INPUTS + n_cast]
    cos_ref, sin_ref = refs[N_ROPE_INPUTS + n_cast:N_ROPE_INPUTS + n_cast + 2]
    cast_out = refs[N_ROPE_INPUTS + n_cast + 2:]
    _round_blocks(cast_in, cast_out)

    rows = pos_ref.shape[0]
    half_rows, half = rows // 2, RET_QK_DIM // 2
    pos = pos_ref[...].astype(F32)
    low = lax.broadcasted_iota(jnp.int32, (half_rows, RET_QK_DIM), 1) < half
    ang = jnp.where(low, pos[:half_rows], pos[half_rows:]) * freq_ref[...]
    cos_p, sin_p = jnp.cos(ang), jnp.sin(ang)
    cos_s, sin_s = pltpu.roll(cos_p, half, 1), pltpu.roll(sin_p, half, 1)
    sign = sign_ref[...]
    cos_ref[:half_rows, :] = jnp.where(low, cos_p, cos_s)
    cos_ref[half_rows:, :] = jnp.where(low, cos_s, cos_p)
    sin_ref[:half_rows, :] = jnp.where(low, sin_p, sin_s) * sign
    sin_ref[half_rows:, :] = jnp.where(low, sin_s, sin_p) * sign


def _rope_tables(positions, cast_jobs):
    b, s = positions.shape
    half = RET_QK_DIM // 2
    inv_freq = ROPE_BASE ** (-jnp.arange(half, dtype=F32) / half)
    freq = jnp.concatenate([inv_freq, inv_freq]).reshape(1, RET_QK_DIM)
    sign = jnp.concatenate([-jnp.ones((half,), F32), jnp.ones((half,), F32)]).reshape(1, RET_QK_DIM)
    rows = ROPE_ROWS
    steps = s // rows
    cast_in, cast_out, cast_shapes, cast_bytes = _round_block_specs(cast_jobs, b * steps, lambda i, j: i * steps + j)
    table = jax.ShapeDtypeStruct((b, s, RET_QK_DIM), F32)
    table_spec = pl.BlockSpec((None, rows, RET_QK_DIM), lambda i, j: (i, j, 0))
    return pl.pallas_call(
        functools.partial(_rope_kernel, n_cast=len(cast_jobs)),
        grid=(b, steps),
        in_specs=[
            pl.BlockSpec((None, rows, 1), lambda i, j: (i, j, 0)),
            pl.BlockSpec((1, RET_QK_DIM), lambda i, j: (0, 0)),
            pl.BlockSpec((1, RET_QK_DIM), lambda i, j: (0, 0)),
        ] + cast_in,
        out_specs=[table_spec, table_spec] + cast_out,
        out_shape=[table, table] + cast_shapes,
        compiler_params=pltpu.CompilerParams(
            dimension_semantics=("arbitrary", "arbitrary"),
            vmem_limit_bytes=_vmem_limit(0, cast_bytes + 4 * rows * 3 * RET_QK_DIM, 0, 16 * 4 * rows * RET_QK_DIM)),
        name="rope_tables",
    )(positions.reshape(b, s, 1), freq, sign, *[w for w, _ in cast_jobs])


def _gelu_tanh(x):
    return 0.5 * x * (1.0 + jnp.tanh(0.7978845608028654 * (x + 0.044715 * (x * x * x))))


N_MIXER_INPUTS = 18


def _mixer_kernel(*refs, n_cast):
    ins = refs[:N_MIXER_INPUTS]
    cast_in = refs[N_MIXER_INPUTS:N_MIXER_INPUTS + n_cast]
    o_ref = refs[N_MIXER_INPUTS + n_cast]
    cast_out = refs[N_MIXER_INPUTS + n_cast + 1:N_MIXER_INPUTS + 2 * n_cast + 1]
    scratch = refs[N_MIXER_INPUTS + 2 * n_cast + 1:]
    _round_blocks(cast_in, cast_out)
    _mixer_step(*ins, o_ref, *scratch)


def _mixer_step(dchunk_ref, x_ref, mod_ref, n1_ref, cos_ref, sin_ref, win_ref, wbr_ref, wout_ref,
                wpool_ref, ws_ref, bst_ref, vnorm_ref, bpool_ref, pscale_ref,
                dintra_ref, dq_ref, dk_ref,
                o_ref,
                state_ref, ptail_ref, qb_ref, qd_ref, kb_ref, kd_ref, vb_ref, sg_ref,
                u_ref, vn_ref, y_ref, m_ref):
    rows, d = x_ref.shape
    n_chunks = rows // CHUNK
    qk_w = RET_HEADS * RET_QK_DIM
    v_w = RET_HEADS * RET_V_DIM
    gdim = d // GMLP_GROUPS
    pdim = d // len(POOL_WINDOWS)
    o_q, o_k, o_v = 0, qk_w, 2 * qk_w
    o_g = o_v + v_w
    o_u = o_g + v_w
    o_vs = o_u + d
    o_p = o_vs + d
    o_gate = o_p + d
    seq_step = pl.program_id(1)
    row_slices = [slice(c * CHUNK, (c + 1) * CHUNK) for c in range(n_chunks)]
    qk_slices = [slice(hd * RET_QK_DIM, (hd + 1) * RET_QK_DIM) for hd in range(RET_HEADS)]
    v_slices = [slice(hd * RET_V_DIM, (hd + 1) * RET_V_DIM) for hd in range(RET_HEADS)]

    @pl.when(seq_step == 0)
    def _():
        state_ref[...] = jnp.zeros_like(state_ref)
        ptail_ref[...] = jnp.zeros_like(ptail_ref)

    x = x_ref[...]
    sh1, sc1, gt1 = mod_ref[0:1, :], mod_ref[1:2, :], mod_ref[2:3, :]
    h = ((_rms(x) * n1_ref[...]) * (1.0 + sc1) + sh1).astype(BF16)

    def proj(lo, width):
        return _dot(h, win_ref[:, lo:lo + width])

    def gate(n):
        return jax.nn.sigmoid(proj(o_gate + n * d, d))

    cosv, sinv = cos_ref[...], sin_ref[...]
    qf, kf = proj(o_q, qk_w), proj(o_k, qk_w)
    k_scale = RET_QK_DIM ** -0.5
    for hd, sl in enumerate(qk_slices):
        q_h, k_h = qf[:, sl], kf[:, sl]
        q_r = q_h * cosv + pltpu.roll(q_h, RET_QK_DIM // 2, 1) * sinv
        k_r = (k_h * cosv + pltpu.roll(k_h, RET_QK_DIM // 2, 1) * sinv) * k_scale
        qb_ref[:, sl] = q_r.astype(BF16)
        qd_ref[:, sl] = (q_r * dq_ref[hd]).astype(BF16)
        kb_ref[:, sl] = k_r.astype(BF16)
        kd_ref[:, sl] = (k_r * dk_ref[hd]).astype(BF16)
    vb_ref[...] = proj(o_v, v_w).astype(BF16)
    g = proj(o_g, v_w)
    sg_ref[...] = g * jax.nn.sigmoid(g)
    vs = _gelu_tanh(proj(o_vs, d))
    vn_ref[...] = (_rms(vs) * vnorm_ref[...]).astype(BF16)

    scores, kv = {}, {}
    for c, rs in enumerate(row_slices):
        for hd, sl in enumerate(qk_slices):
            s = lax.dot_general(qb_ref[rs, sl], kb_ref[rs, sl], (((1,), (1,)), ((), ())),
                                preferred_element_type=F32)
            scores[c, hd] = (s * dintra_ref[hd]).astype(BF16)
            kv[c, hd] = lax.dot_general(kd_ref[rs, sl], vb_ref[rs, v_slices[hd]], (((0,), (0,)), ((), ())),
                                        preferred_element_type=F32)

    u_ref[...] = _gelu_tanh(proj(o_u, d))

    state = [state_ref[hd] for hd in range(RET_HEADS)]
    ret = {}
    for c, rs in enumerate(row_slices):
        for hd, sl in enumerate(qk_slices):
            lhs = jnp.concatenate([scores[c, hd], qd_ref[rs, sl]], axis=1)
            rhs = jnp.concatenate([vb_ref[rs, v_slices[hd]], state[hd].astype(BF16)], axis=0)
            ret[c, hd] = _dot(lhs, rhs)
            state[hd] = dchunk_ref[hd] * state[hd] + kv[c, hd]
    for hd in range(RET_HEADS):
        state_ref[hd] = state[hd]

    p = proj(o_p, d)
    ext = jnp.concatenate([ptail_ref[...], p], axis=0)
    ptail_ref[...] = p[rows - POOL_HISTORY:, :]
    pos = seq_step * rows + lax.broadcasted_iota(jnp.int32, (rows, pdim), 0)
    pooled = []
    for gi, window in enumerate(POOL_WINDOWS):
        psl = slice(gi * pdim, (gi + 1) * pdim)
        acc = ext[:, psl]
        span = 1
        while span < window:
            acc = acc + pltpu.roll(acc, span, 0)
            span *= 2
        count = jnp.minimum(pos + 1, window).astype(F32)
        pooled.append((acc[POOL_HISTORY:, :] / count - p[:, psl]).astype(BF16))

    for c, rs in enumerate(row_slices):
        for hd, vsl in enumerate(v_slices):
            y_ref[0, rs, vsl] = (sg_ref[rs, vsl] * _rms(ret[c, hd])).astype(BF16)

    tri = lax.broadcasted_iota(jnp.int32, (CHUNK, CHUNK), 0) >= lax.broadcasted_iota(jnp.int32, (CHUNK, CHUNK), 1)
    for gi in range(GMLP_GROUPS):
        gsl = slice(gi * gdim, (gi + 1) * gdim)
        w_tri = jnp.where(tri, ws_ref[gi], 0.0).astype(BF16)
        bias = jnp.broadcast_to(bst_ref[:, gi:gi + 1], (CHUNK, gdim))
        for rs in row_slices:
            mixed = _dot(w_tri, vn_ref[rs, gsl]) + bias
            y_ref[1, rs, gsl] = (u_ref[rs, gsl] * mixed).astype(BF16)

    m_ref[...] = gate(0) * _dot(y_ref[0], wbr_ref[0])

    for gi in range(len(POOL_WINDOWS)):
        psl = slice(gi * pdim, (gi + 1) * pdim)
        mixed = _dot(pooled[gi], wpool_ref[gi]) + bpool_ref[:, psl]
        y_ref[2, :, psl] = (mixed * pscale_ref[:, psl]).astype(BF16)

    m_ref[...] += gate(1) * _dot(y_ref[1], wbr_ref[1])
    m_ref[...] += gate(2) * _dot(y_ref[2], wbr_ref[2])
    o_ref[...] = x + gt1 * _dot(m_ref[...].astype(BF16), wout_ref[...])


def _mixer(layer, x, mod, norm1, cos_t, sin_t, w_in, w_branch, w_out, w_pool, ws, bs_t, vnorm, b_pool,
           pool_scale, dintra, dq, dk, dchunk, cast_jobs):
    b, s, d = x.shape
    rows = MIXER_ROWS
    steps = s // rows
    d_in = w_in.shape[1]
    v_w = RET_HEADS * RET_V_DIM
    qk_w = RET_HEADS * RET_QK_DIM
    row_spec = lambda width: pl.BlockSpec((None, rows, width), lambda i, j: (i, j, 0))
    lres = functools.partial(_layer_resident, layer)
    in_specs = [
        pl.BlockSpec(memory_space=pltpu.SMEM),
        row_spec(d),
        pl.BlockSpec((None, None, N_MOD, d), lambda i, j: (layer, i, 0, 0)),
        lres((1, d)),
        row_spec(RET_QK_DIM), row_spec(RET_QK_DIM),
        _resident((d, d_in)), _resident((N_BRANCH, d, d)), _resident((d, d)),
        _resident(w_pool.shape), lres(ws.shape[1:]), lres(bs_t.shape[1:]),
        lres((1, d)), lres((1, d)), lres((1, d)),
        _resident(dintra.shape), _resident(dq.shape), _resident(dk.shape),
    ]
    assert len(in_specs) == N_MIXER_INPUTS
    cast_in, cast_out, cast_shapes, cast_bytes = _round_block_specs(cast_jobs, b * steps, lambda i, j: i * steps + j)
    in_specs += cast_in
    out_specs = [row_spec(d)] + cast_out
    out_shape = [jax.ShapeDtypeStruct((b, s, d), F32)] + cast_shapes
    scratch = [
        pltpu.VMEM((RET_HEADS, RET_QK_DIM, RET_V_DIM), F32),
        pltpu.VMEM((POOL_HISTORY, d), F32),
        pltpu.VMEM((rows, qk_w), BF16), pltpu.VMEM((rows, qk_w), BF16),
        pltpu.VMEM((rows, qk_w), BF16), pltpu.VMEM((rows, qk_w), BF16),
        pltpu.VMEM((rows, v_w), BF16),
        pltpu.VMEM((rows, v_w), F32),
        pltpu.VMEM((rows, d), F32),
        pltpu.VMEM((rows, d), BF16),
        pltpu.VMEM((N_BRANCH, rows, d), BF16),
        pltpu.VMEM((rows, d), F32),
    ]
    resident = 2 * (d * d_in + N_BRANCH * d * d + d * d + w_pool.size) + 4 * (
        ws[0].size + bs_t[0].size + dintra.size + dq.size + dk.size + 6 * d)
    streamed = 4 * rows * (2 * d + 2 * RET_QK_DIM) + 4 * N_MOD * d + cast_bytes
    scratch_bytes = 4 * (RET_HEADS * RET_QK_DIM * RET_V_DIM + POOL_HISTORY * d) + rows * (
        2 * 4 * qk_w + 2 * v_w + 4 * v_w + 4 * d + 2 * d + 2 * N_BRANCH * d + 4 * d)
    temps = 8 * 4 * rows * d
    return pl.pallas_call(
        functools.partial(_mixer_kernel, n_cast=len(cast_jobs)),
        grid=(b, steps),
        in_specs=in_specs,
        out_specs=out_specs,
        out_shape=out_shape,
        scratch_shapes=scratch,
        compiler_params=pltpu.CompilerParams(
            dimension_semantics=("arbitrary", "arbitrary"),
            vmem_limit_bytes=_vmem_limit(resident, streamed, scratch_bytes, temps)),
        name="token_mixer",
    )(dchunk, x, mod, norm1, cos_t, sin_t, w_in, w_branch, w_out, w_pool, ws, bs_t, vnorm, b_pool,
      pool_scale, dintra, dq, dk, *[w for w, _ in cast_jobs])


def _ffn_kernel(x_ref, mod_ref, n2_ref, w1_ref, w2_ref, fn_ref, o_ref, hid_ref, *, final):
    x = x_ref[...]
    sh2, sc2, gt2 = mod_ref[3:4, :], mod_ref[4:5, :], mod_ref[5:6, :]
    h = ((_rms(x) * n2_ref[...]) * (1.0 + sc2) + sh2).astype(BF16)
    d_ff = w1_ref.shape[1]
    for lo in range(0, d_ff, FF_BLOCK):
        hid = _dot(h, w1_ref[:, lo:lo + FF_BLOCK])
        hid_ref[:, lo:lo + FF_BLOCK] = jnp.square(jnp.maximum(hid, 0.0)).astype(BF16)
    y = x + gt2 * _dot(hid_ref[...], w2_ref[...])
    if final:
        y = _rms(y) * fn_ref[...]
    o_ref[...] = y


def _ffn(layer, x, mod, norm2, w1, w2, final_norm, final):
    b, s, d = x.shape
    rows = FFN_ROWS
    d_ff = w1.shape[1]
    row_spec = pl.BlockSpec((None, rows, d), lambda i, j: (i, j, 0))
    lres = functools.partial(_layer_resident, layer)
    resident = 2 * 2 * d * d_ff + 4 * 2 * d
    streamed = 4 * 2 * rows * d + 4 * N_MOD * d
    temps = rows * (2 * d + 6 * FF_BLOCK + 3 * 4 * d)
    return pl.pallas_call(
        functools.partial(_ffn_kernel, final=final),
        grid=(b, s // rows),
        in_specs=[
            row_spec,
            pl.BlockSpec((None, None, N_MOD, d), lambda i, j: (layer, i, 0, 0)),
            lres((1, d)), _resident((d, d_ff)), _resident((d_ff, d)), _resident((1, d)),
        ],
        out_specs=row_spec,
        out_shape=jax.ShapeDtypeStruct((b, s, d), F32),
        scratch_shapes=[pltpu.VMEM((rows, d_ff), BF16)],
        compiler_params=pltpu.CompilerParams(
            dimension_semantics=("arbitrary", "arbitrary"),
            vmem_limit_bytes=_vmem_limit(resident, streamed, 2 * rows * d_ff, temps)),
        name="channel_mlp",
    )(x, mod, norm2, w1, w2, final_norm)


def _retention_decay_tables(rows):
    log_gamma = jnp.log1p(-jnp.power(2.0, -5.0 - jnp.arange(RET_HEADS, dtype=F32)))
    pos = jnp.arange(CHUNK, dtype=F32)
    rel = pos[:, None] - pos[None, :]
    causal = rel >= 0
    intra = jnp.where(causal[None], jnp.exp(log_gamma[:, None, None] * jnp.where(causal, rel, 0.0)[None]), 0.0)
    decay_q = jnp.exp(log_gamma[:, None] * (pos + 1.0)[None])
    decay_k = jnp.exp(log_gamma[:, None] * (CHUNK - 1.0 - pos)[None])
    decay_chunk = jnp.exp(log_gamma * CHUNK)
    reps = rows // CHUNK
    widen = lambda t: jnp.broadcast_to(jnp.tile(t, (1, reps))[:, :, None], (RET_HEADS, rows, RET_QK_DIM))
    return intra, widen(decay_q), widen(decay_k), decay_chunk


def kernel(x, c, positions, w_ada, b_ada, norm1, norm2, w_in, ws_gmlp, bs_gmlp, vnorm_gmlp, w_pool, b_pool,
           pool_scale, w_branch, w_out, w_ff1, w_ff2, final_norm):
    depth = w_in.shape[0]
    b, s, d = x.shape
    assert s % MIXER_ROWS == 0 and s % FFN_ROWS == 0 and s % ROPE_ROWS == 0 and MIXER_ROWS % CHUNK == 0
    assert w_ff1.shape[2] % FF_BLOCK == 0 and w_ada.shape[2] % ADA_COLS == 0

    mod = _ada_rows(c, w_ada, b_ada).reshape(depth, b, N_MOD, d)
    dintra, dq, dk, dchunk = _retention_decay_tables(MIXER_ROWS)
    rows_of = lambda v: v.reshape(depth, 1, d)
    norm1, norm2, vnorm, b_pool, pool_scale = map(rows_of, (norm1, norm2, vnorm_gmlp, b_pool, pool_scale))
    bs_t = jnp.swapaxes(bs_gmlp, 1, 2)

    mixer_w = (w_in, w_branch, w_out, w_pool)
    as_rows = lambda w: w.reshape(depth, -1, w.shape[-1])
    cos_t, sin_t, *next_mixer = _rope_tables(positions, [(as_rows(w), 0) for w in mixer_w])
    for l in range(depth):
        mixer_bf16 = [w.reshape(ref.shape[1:]) for w, ref in zip(next_mixer, mixer_w)]
        jobs = [(as_rows(w_ff1), l), (as_rows(w_ff2), l)]
        if l + 1 < depth:
            jobs += [(as_rows(w), l + 1) for w in mixer_w]
        x, ff1, ff2, *next_mixer = _mixer(l, x, mod, norm1, cos_t, sin_t, *mixer_bf16, ws_gmlp, bs_t, vnorm,
                                          b_pool, pool_scale, dintra, dq, dk, dchunk, jobs)
        x = _ffn(l, x, mod, norm2, ff1, ff2, final_norm.reshape(1, d), final=(l == depth - 1))
    return x
```

```python
import functools

import jax
import jax.numpy as jnp
from jax import lax
from jax.experimental import pallas as pl
from jax.experimental.pallas import tpu as pltpu

CHUNK = 128
RET_HEADS = 4
RET_QK_DIM = 128
RET_V_DIM = 256
GMLP_GROUPS = 4
POOL_WINDOWS = (2, 4, 8, 16)
POOL_HISTORY = 16
N_BRANCH = 3
N_MOD = 6
ROPE_BASE = 10000.0
EPS = 1e-6

VMEM_BYTES_V7X = 64 * 1024 * 1024
BF16_ROW_TILE = 16

MIXER_ROWS = 256
FFN_ROWS = 1024
FF_BLOCK = 1024
ROPE_ROWS = 2048

F32 = jnp.float32
BF16 = jnp.bfloat16


def _dot(a, b):
    return jnp.dot(a, b, preferred_element_type=F32)


def _rms(x):
    return x * lax.rsqrt(jnp.mean(x * x, axis=-1, keepdims=True) + EPS)


def _resident(shape):
    zeros = (0,) * len(shape)
    return pl.BlockSpec(shape, lambda *_: zeros, pipeline_mode=pl.Buffered(1))


def _layer_resident(layer, shape):
    index = (layer,) + (0,) * len(shape)
    return pl.BlockSpec((None,) + tuple(shape), lambda *_: index, pipeline_mode=pl.Buffered(1))


def _vmem_limit(resident_bytes, streamed_bytes, scratch_bytes, temp_bytes):
    need = resident_bytes + 2 * streamed_bytes + scratch_bytes + temp_bytes
    return min(int(need), VMEM_BYTES_V7X - 4 * 1024 * 1024)


def _ada_kernel(c_ref, w_ref, b_ref, o_ref):
    c = c_ref[...]
    c_act = (c * jax.nn.sigmoid(c)).astype(BF16)
    o_ref[...] = _dot(c_act, w_ref[...].astype(BF16)) + b_ref[...]


def _ada_rows(c, w_ada, b_ada):
    depth, d, n = w_ada.shape
    b = c.shape[0]
    assert n == N_MOD * d
    return pl.pallas_call(
        _ada_kernel,
        grid=(depth, N_MOD),
        in_specs=[
            pl.BlockSpec((b, d), lambda l, j: (0, 0)),
            pl.BlockSpec((None, d, d), lambda l, j: (l, 0, j)),
            pl.BlockSpec((None, 1, d), lambda l, j: (l, 0, j)),
        ],
        out_specs=pl.BlockSpec((None, None, b, d), lambda l, j: (l, j, 0, 0)),
        out_shape=jax.ShapeDtypeStruct((depth, N_MOD, b, d), F32),
        compiler_params=pltpu.CompilerParams(
            dimension_semantics=("arbitrary", "arbitrary")),
        name="ada_rows",
    )(c, w_ada, b_ada.reshape(depth, 1, n))


def _round_blocks(cast_in, cast_out):
    for src_ref, dst_ref in zip(cast_in, cast_out):
        dst_ref[...] = src_ref[...].astype(BF16)


def _round_block_specs(cast_jobs, n_steps, flat_step):
    in_specs, out_specs, out_shapes, nbytes = [], [], [], 0
    for w, layer in cast_jobs:
        _, r, c = w.shape
        blk = r // n_steps
        assert blk * n_steps == r and blk % BF16_ROW_TILE == 0
        in_specs.append(pl.BlockSpec((None, blk, c), lambda i, j, layer=layer: (layer, flat_step(i, j), 0)))
        out_specs.append(pl.BlockSpec((blk, c), lambda i, j: (flat_step(i, j), 0)))
        out_shapes.append(jax.ShapeDtypeStruct((r, c), BF16))
        nbytes += blk * c * (4 + 2)
    return in_specs, out_specs, out_shapes, nbytes


N_ROPE_INPUTS = 3


def _rope_kernel(*refs, n_cast):
    pos_ref, freq_ref, sign_ref = refs[:N_ROPE_INPUTS]
    cast_in = refs[N_ROPE_INPUTS:N_ROPE_INPUTS + n_cast]
    tab_ref = refs[N_ROPE_INPUTS + n_cast]
    cast_out = refs[N_ROPE_INPUTS + n_cast + 1:]
    _round_blocks(cast_in, cast_out)

    rows = pos_ref.shape[0]
    half_rows, half = rows // 2, RET_QK_DIM // 2
    pos = pos_ref[...].astype(F32)
    low = lax.broadcasted_iota(jnp.int32, (half_rows, RET_QK_DIM), 1) < half
    ang = jnp.where(low, pos[:half_rows], pos[half_rows:]) * freq_ref[...]
    cos_p, sin_p = jnp.cos(ang), jnp.sin(ang)
    cos_s, sin_s = pltpu.roll(cos_p, half, 1), pltpu.roll(sin_p, half, 1)
    sign = sign_ref[...]
    tab_ref[:half_rows, :RET_QK_DIM] = jnp.where(low, cos_p, cos_s)
    tab_ref[half_rows:, :RET_QK_DIM] = jnp.where(low, cos_s, cos_p)
    tab_ref[:half_rows, RET_QK_DIM:] = jnp.where(low, sin_p, sin_s) * sign
    tab_ref[half_rows:, RET_QK_DIM:] = jnp.where(low, sin_s, sin_p) * sign


def _rope_tables(positions, cast_jobs):
    b, s = positions.shape
    half = RET_QK_DIM // 2
    inv_freq = ROPE_BASE ** (-jnp.arange(half, dtype=F32) / half)
    freq = jnp.concatenate([inv_freq, inv_freq]).reshape(1, RET_QK_DIM)
    sign = jnp.concatenate([-jnp.ones((half,), F32), jnp.ones((half,), F32)]).reshape(1, RET_QK_DIM)
    rows = ROPE_ROWS
    steps = s // rows
    cast_in, cast_out, cast_shapes, cast_bytes = _round_block_specs(cast_jobs, b * steps, lambda i, j: i * steps + j)
    table = jax.ShapeDtypeStruct((b, s, 2 * RET_QK_DIM), F32)
    table_spec = pl.BlockSpec((None, rows, 2 * RET_QK_DIM), lambda i, j: (i, j, 0))
    return pl.pallas_call(
        functools.partial(_rope_kernel, n_cast=len(cast_jobs)),
        grid=(b, steps),
        in_specs=[
            pl.BlockSpec((None, rows, 1), lambda i, j: (i, j, 0)),
            pl.BlockSpec((1, RET_QK_DIM), lambda i, j: (0, 0)),
            pl.BlockSpec((1, RET_QK_DIM), lambda i, j: (0, 0)),
        ] + cast_in,
        out_specs=[table_spec] + cast_out,
        out_shape=[table] + cast_shapes,
        compiler_params=pltpu.CompilerParams(
            dimension_semantics=("arbitrary", "arbitrary"),
            vmem_limit_bytes=_vmem_limit(0, cast_bytes + 4 * rows * 3 * RET_QK_DIM, 0, 16 * 4 * rows * RET_QK_DIM)),
        name="rope_tables",
    )(positions.reshape(b, s, 1), freq, sign, *[w for w, _ in cast_jobs])


GELU_C0 = 0.7978845608028654
GELU_C1 = GELU_C0 * 0.044715


def _gelu_tanh(x):
    half_x = 0.5 * x
    return half_x + half_x * jnp.tanh(x * (GELU_C0 + GELU_C1 * (x * x)))


N_MIXER_INPUTS = 13


def _mixer_kernel(*refs, n_cast):
    ins = refs[:N_MIXER_INPUTS]
    cast_in = refs[N_MIXER_INPUTS:N_MIXER_INPUTS + n_cast]
    o_ref = refs[N_MIXER_INPUTS + n_cast]
    cast_out = refs[N_MIXER_INPUTS + n_cast + 1:N_MIXER_INPUTS + 2 * n_cast + 1]
    scratch = refs[N_MIXER_INPUTS + 2 * n_cast + 1:]
    _round_blocks(cast_in, cast_out)
    _mixer_step(*ins, o_ref, *scratch)


def _mixer_step(dchunk_ref, x_ref, mod_ref, prm_ref, rope_ref, win_ref, wbr_ref, wout_ref,
                wpool_ref, ws_ref, bst_ref, dintra_ref, dqk_ref,
                o_ref,
                state_ref, ptail_ref, qb_ref, qd_ref, kb_ref, kd_ref, vb_ref, sg_ref,
                u_ref, vn_ref, y_ref, m_ref):
    rows, d = x_ref.shape
    n_chunks = rows // CHUNK
    qk_w = RET_HEADS * RET_QK_DIM
    v_w = RET_HEADS * RET_V_DIM
    gdim = d // GMLP_GROUPS
    pdim = d // len(POOL_WINDOWS)
    o_q, o_k, o_v = 0, qk_w, 2 * qk_w
    o_g = o_v + v_w
    o_u = o_g + v_w
    o_vs = o_u + d
    o_p = o_vs + d
    o_gate = o_p + d
    seq_step = pl.program_id(1)
    row_slices = [slice(c * CHUNK, (c + 1) * CHUNK) for c in range(n_chunks)]
    qk_slices = [slice(hd * RET_QK_DIM, (hd + 1) * RET_QK_DIM) for hd in range(RET_HEADS)]
    v_slices = [slice(hd * RET_V_DIM, (hd + 1) * RET_V_DIM) for hd in range(RET_HEADS)]

    @pl.when(seq_step == 0)
    def _():
        state_ref[...] = jnp.zeros_like(state_ref)
        ptail_ref[...] = jnp.zeros_like(ptail_ref)

    x = x_ref[...]
    batch_row = pl.ds(pl.program_id(0), 1)
    sh1, sc1, gt1 = mod_ref[0, batch_row, :], mod_ref[1, batch_row, :], mod_ref[2, batch_row, :]
    h = (_rms(x) * (prm_ref[0:1, :] * (1.0 + sc1)) + sh1).astype(BF16)

    def proj(lo, width):
        return _dot(h, win_ref[:, lo:lo + width])

    def gate(n):
        return jax.nn.sigmoid(proj(o_gate + n * d, d))

    cosv, sinv = rope_ref[:, :RET_QK_DIM], rope_ref[:, RET_QK_DIM:]
    qf, kf = proj(o_q, qk_w), proj(o_k, qk_w)
    k_scale = RET_QK_DIM ** -0.5
    for hd, sl in enumerate(qk_slices):
        q_h, k_h = qf[:, sl], kf[:, sl]
        q_r = q_h * cosv + pltpu.roll(q_h, RET_QK_DIM // 2, 1) * sinv
        k_r = (k_h * cosv + pltpu.roll(k_h, RET_QK_DIM // 2, 1) * sinv) * k_scale
        qb_ref[:, sl] = q_r.astype(BF16)
        qd_ref[:, sl] = (q_r * dqk_ref[0, hd]).astype(BF16)
        kb_ref[:, sl] = k_r.astype(BF16)
        kd_ref[:, sl] = (k_r * dqk_ref[1, hd]).astype(BF16)
    vb_ref[...] = proj(o_v, v_w).astype(BF16)
    g = proj(o_g, v_w)
    sg_ref[...] = g * jax.nn.sigmoid(g)
    vs = _gelu_tanh(proj(o_vs, d))
    vn_ref[...] = (_rms(vs) * prm_ref[1:2, :]).astype(BF16)

    scores, kv = {}, {}
    for c, rs in enumerate(row_slices):
        for hd, sl in enumerate(qk_slices):
            s = lax.dot_general(qb_ref[rs, sl], kb_ref[rs, sl], (((1,), (1,)), ((), ())),
                                preferred_element_type=F32)
            scores[c, hd] = (s * dintra_ref[hd]).astype(BF16)
            kv[c, hd] = lax.dot_general(kd_ref[rs, sl], vb_ref[rs, v_slices[hd]], (((0,), (0,)), ((), ())),
                                        preferred_element_type=F32)

    u_ref[...] = _gelu_tanh(proj(o_u, d))

    state = [state_ref[hd] for hd in range(RET_HEADS)]
    ret = {}
    for c, rs in enumerate(row_slices):
        for hd, sl in enumerate(qk_slices):
            lhs = jnp.concatenate([scores[c, hd], qd_ref[rs, sl]], axis=1)
            rhs = jnp.concatenate([vb_ref[rs, v_slices[hd]], state[hd].astype(BF16)], axis=0)
            ret[c, hd] = _dot(lhs, rhs)
            state[hd] = dchunk_ref[hd] * state[hd] + kv[c, hd]
    for hd in range(RET_HEADS):
        state_ref[hd] = state[hd]

    p = proj(o_p, d)
    ext = jnp.concatenate([ptail_ref[...], p], axis=0)
    ptail_ref[...] = p[rows - POOL_HISTORY:, :]
    pos = seq_step * rows + lax.broadcasted_iota(jnp.int32, (rows, pdim), 0)
    pooled = []
    for gi, window in enumerate(POOL_WINDOWS):
        psl = slice(gi * pdim, (gi + 1) * pdim)
        acc = ext[:, psl]
        span = 1
        while span < window:
            acc = acc + pltpu.roll(acc, span, 0)
            span *= 2
        count = jnp.minimum(pos + 1, window).astype(F32)
        pooled.append((acc[POOL_HISTORY:, :] / count - p[:, psl]).astype(BF16))

    for c, rs in enumerate(row_slices):
        for hd, vsl in enumerate(v_slices):
            y_ref[0, rs, vsl] = (sg_ref[rs, vsl] * _rms(ret[c, hd])).astype(BF16)

    tri = lax.broadcasted_iota(jnp.int32, (CHUNK, CHUNK), 0) >= lax.broadcasted_iota(jnp.int32, (CHUNK, CHUNK), 1)
    for gi in range(GMLP_GROUPS):
        gsl = slice(gi * gdim, (gi + 1) * gdim)
        w_tri = jnp.where(tri, ws_ref[gi], 0.0).astype(BF16)
        bias = jnp.broadcast_to(bst_ref[:, gi:gi + 1], (CHUNK, gdim))
        for rs in row_slices:
            mixed = _dot(w_tri, vn_ref[rs, gsl]) + bias
            y_ref[1, rs, gsl] = (u_ref[rs, gsl] * mixed).astype(BF16)

    m_ref[...] = gate(0) * _dot(y_ref[0], wbr_ref[0])

    for gi in range(len(POOL_WINDOWS)):
        psl = slice(gi * pdim, (gi + 1) * pdim)
        mixed = _dot(pooled[gi], wpool_ref[gi]) + prm_ref[2:3, psl]
        y_ref[2, :, psl] = (mixed * prm_ref[3:4, psl]).astype(BF16)

    m_ref[...] += gate(1) * _dot(y_ref[1], wbr_ref[1])
    m_ref[...] += gate(2) * _dot(y_ref[2], wbr_ref[2])
    o_ref[...] = x + gt1 * _dot(m_ref[...].astype(BF16), wout_ref[...])


def _mixer(layer, x, mod, prm, rope_t, w_in, w_branch, w_out, w_pool, ws, bs_t, dintra, dqk, dchunk, cast_jobs):
    b, s, d = x.shape
    rows = MIXER_ROWS
    steps = s // rows
    d_in = w_in.shape[1]
    v_w = RET_HEADS * RET_V_DIM
    qk_w = RET_HEADS * RET_QK_DIM
    row_spec = lambda width: pl.BlockSpec((None, rows, width), lambda i, j: (i, j, 0))
    lres = functools.partial(_layer_resident, layer)
    in_specs = [
        pl.BlockSpec(memory_space=pltpu.SMEM),
        row_spec(d),
        lres(mod.shape[1:]),
        lres(prm.shape[1:]),
        row_spec(2 * RET_QK_DIM),
        _resident((d, d_in)), _resident((N_BRANCH, d, d)), _resident((d, d)),
        _resident(w_pool.shape), lres(ws.shape[1:]), lres(bs_t.shape[1:]),
        _resident(dintra.shape), _resident(dqk.shape),
    ]
    assert len(in_specs) == N_MIXER_INPUTS
    cast_in, cast_out, cast_shapes, cast_bytes = _round_block_specs(cast_jobs, b * steps, lambda i, j: i * steps + j)
    in_specs += cast_in
    out_specs = [row_spec(d)] + cast_out
    out_shape = [jax.ShapeDtypeStruct((b, s, d), F32)] + cast_shapes
    scratch = [
        pltpu.VMEM((RET_HEADS, RET_QK_DIM, RET_V_DIM), F32),
        pltpu.VMEM((POOL_HISTORY, d), F32),
        pltpu.VMEM((rows, qk_w), BF16), pltpu.VMEM((rows, qk_w), BF16),
        pltpu.VMEM((rows, qk_w), BF16), pltpu.VMEM((rows, qk_w), BF16),
        pltpu.VMEM((rows, v_w), BF16),
        pltpu.VMEM((rows, v_w), F32),
        pltpu.VMEM((rows, d), F32),
        pltpu.VMEM((rows, d), BF16),
        pltpu.VMEM((N_BRANCH, rows, d), BF16),
        pltpu.VMEM((rows, d), F32),
    ]
    resident = 2 * (d * d_in + N_BRANCH * d * d + d * d + w_pool.size) + 4 * (
        ws[0].size + bs_t[0].size + dintra.size + dqk.size + mod[0].size + prm[0].size)
    streamed = 4 * rows * (2 * d + 2 * RET_QK_DIM) + cast_bytes
    scratch_bytes = 4 * (RET_HEADS * RET_QK_DIM * RET_V_DIM + POOL_HISTORY * d) + rows * (
        2 * 4 * qk_w + 2 * v_w + 4 * v_w + 4 * d + 2 * d + 2 * N_BRANCH * d + 4 * d)
    temps = 8 * 4 * rows * d
    return pl.pallas_call(
        functools.partial(_mixer_kernel, n_cast=len(cast_jobs)),
        grid=(b, steps),
        in_specs=in_specs,
        out_specs=out_specs,
        out_shape=out_shape,
        scratch_shapes=scratch,
        compiler_params=pltpu.CompilerParams(
            dimension_semantics=("arbitrary", "arbitrary"),
            vmem_limit_bytes=_vmem_limit(resident, streamed, scratch_bytes, temps)),
        name="token_mixer",
    )(dchunk, x, mod, prm, rope_t, w_in, w_branch, w_out, w_pool, ws, bs_t, dintra, dqk,
      *[w for w, _ in cast_jobs])


def _ffn_kernel(x_ref, mod_ref, n2_ref, w1_ref, w2_ref, fn_ref, o_ref, hid_ref, *, final):
    x = x_ref[...]
    batch_row = pl.ds(pl.program_id(0), 1)
    sh2, sc2, gt2 = mod_ref[3, batch_row, :], mod_ref[4, batch_row, :], mod_ref[5, batch_row, :]
    h = (_rms(x) * (n2_ref[...] * (1.0 + sc2)) + sh2).astype(BF16)
    d_ff = w1_ref.shape[1]
    for lo in range(0, d_ff, FF_BLOCK):
        hid = _dot(h, w1_ref[:, lo:lo + FF_BLOCK])
        hid_ref[:, lo:lo + FF_BLOCK] = jnp.square(jnp.maximum(hid, 0.0)).astype(BF16)
    y = x + gt2 * _dot(hid_ref[...], w2_ref[...])
    if final:
        y = _rms(y) * fn_ref[...]
    o_ref[...] = y


def _ffn(layer, x, mod, norm2, w1, w2, final_norm, final):
    b, s, d = x.shape
    rows = FFN_ROWS
    d_ff = w1.shape[1]
    row_spec = pl.BlockSpec((None, rows, d), lambda i, j: (i, j, 0))
    lres = functools.partial(_layer_resident, layer)
    resident = 2 * 2 * d * d_ff + 4 * 2 * d + 4 * mod[0].size
    streamed = 4 * 2 * rows * d
    temps = rows * (2 * d + 6 * FF_BLOCK + 3 * 4 * d)
    return pl.pallas_call(
        functools.partial(_ffn_kernel, final=final),
        grid=(b, s // rows),
        in_specs=[
            row_spec,
            lres(mod.shape[1:]),
            lres((1, d)), _resident((d, d_ff)), _resident((d_ff, d)), _resident((1, d)),
        ],
        out_specs=row_spec,
        out_shape=jax.ShapeDtypeStruct((b, s, d), F32),
        scratch_shapes=[pltpu.VMEM((rows, d_ff), BF16)],
        compiler_params=pltpu.CompilerParams(
            dimension_semantics=("arbitrary", "arbitrary"),
            vmem_limit_bytes=_vmem_limit(resident, streamed, 2 * rows * d_ff, temps)),
        name="channel_mlp",
    )(x, mod, norm2, w1, w2, final_norm)


def _retention_decay_tables(rows):
    log_gamma = jnp.log1p(-jnp.power(2.0, -5.0 - jnp.arange(RET_HEADS, dtype=F32)))
    pos = jnp.arange(CHUNK, dtype=F32)
    rel = pos[:, None] - pos[None, :]
    causal = rel >= 0
    intra = jnp.where(causal[None], jnp.exp(log_gamma[:, None, None] * jnp.where(causal, rel, 0.0)[None]), 0.0)
    decay_q = jnp.exp(log_gamma[:, None] * (pos + 1.0)[None])
    decay_k = jnp.exp(log_gamma[:, None] * (CHUNK - 1.0 - pos)[None])
    decay_chunk = jnp.exp(log_gamma * CHUNK)
    reps = rows // CHUNK
    widen = lambda t: jnp.broadcast_to(jnp.tile(t, (1, reps))[:, :, None], (RET_HEADS, rows, RET_QK_DIM))
    return intra, jnp.stack([widen(decay_q), widen(decay_k)]), decay_chunk


def kernel(x, c, positions, w_ada, b_ada, norm1, norm2, w_in, ws_gmlp, bs_gmlp, vnorm_gmlp, w_pool, b_pool,
           pool_scale, w_branch, w_out, w_ff1, w_ff2, final_norm):
    depth = w_in.shape[0]
    b, s, d = x.shape
    assert s % MIXER_ROWS == 0 and s % FFN_ROWS == 0 and s % ROPE_ROWS == 0 and MIXER_ROWS % CHUNK == 0
    assert w_ff1.shape[2] % FF_BLOCK == 0

    mod = _ada_rows(c, w_ada, b_ada)
    dintra, dqk, dchunk = _retention_decay_tables(MIXER_ROWS)
    rows_of = lambda v: v.reshape(depth, 1, d)
    prm = jnp.concatenate([rows_of(v) for v in (norm1, vnorm_gmlp, b_pool, pool_scale)], axis=1)
    norm2 = rows_of(norm2)
    bs_t = jnp.swapaxes(bs_gmlp, 1, 2)

    mixer_w = (w_in, w_branch, w_out, w_pool)
    as_rows = lambda w: w.reshape(depth, -1, w.shape[-1])
    rope_t, *next_mixer = _rope_tables(positions, [(as_rows(w), 0) for w in mixer_w])
    for l in range(depth):
        mixer_bf16 = [w.reshape(ref.shape[1:]) for w, ref in zip(next_mixer, mixer_w)]
        jobs = [(as_rows(w_ff1), l), (as_rows(w_ff2), l)]
        if l + 1 < depth:
            jobs += [(as_rows(w), l + 1) for w in mixer_w]
        x, ff1, ff2, *next_mixer = _mixer(l, x, mod, prm, rope_t, *mixer_bf16, ws_gmlp, bs_t,
                                          dintra, dqk, dchunk, jobs)
        x = _ffn(l, x, mod, norm2, ff1, ff2, final_norm.reshape(1, d), final=(l == depth - 1))
    return x
```

```python
import functools

import jax
import jax.numpy as jnp
from jax import lax
from jax.experimental import pallas as pl
from jax.experimental.pallas import tpu as pltpu

CHUNK = 128
RET_HEADS = 4
RET_QK_DIM = 128
RET_V_DIM = 256
GMLP_GROUPS = 4
POOL_WINDOWS = (2, 4, 8, 16)
POOL_HISTORY = 16
N_BRANCH = 3
N_MOD = 6
ROPE_BASE = 10000.0
EPS = 1e-6

VMEM_BYTES_V7X = 64 * 1024 * 1024
BF16_ROW_TILE = 16

MIXER_ROWS = 256
FFN_ROWS = 512
FF_BLOCK = 1024
ROPE_ROWS = 2048

F32 = jnp.float32
BF16 = jnp.bfloat16


def _dot(a, b):
    return jnp.dot(a, b, preferred_element_type=F32)


def _rms(x):
    return x * lax.rsqrt(jnp.mean(x * x, axis=-1, keepdims=True) + EPS)


def _resident(shape):
    zeros = (0,) * len(shape)
    return pl.BlockSpec(shape, lambda *_: zeros, pipeline_mode=pl.Buffered(1))


def _layer_resident(layer, shape):
    index = (layer,) + (0,) * len(shape)
    return pl.BlockSpec((None,) + tuple(shape), lambda *_: index, pipeline_mode=pl.Buffered(1))


def _vmem_limit(resident_bytes, streamed_bytes, scratch_bytes, temp_bytes):
    need = resident_bytes + 2 * streamed_bytes + scratch_bytes + temp_bytes
    return min(int(need), VMEM_BYTES_V7X - 4 * 1024 * 1024)


def _ada_kernel(c_ref, w_ref, b_ref, o_ref):
    c = c_ref[...]
    c_act = (c * jax.nn.sigmoid(c)).astype(BF16)
    o_ref[...] = _dot(c_act, w_ref[...].astype(BF16)) + b_ref[...]


def _ada_rows(c, w_ada, b_ada):
    depth, d, n = w_ada.shape
    b = c.shape[0]
    assert n == N_MOD * d
    return pl.pallas_call(
        _ada_kernel,
        grid=(depth, N_MOD),
        in_specs=[
            pl.BlockSpec((b, d), lambda l, j: (0, 0)),
            pl.BlockSpec((None, d, d), lambda l, j: (l, 0, j)),
            pl.BlockSpec((None, 1, d), lambda l, j: (l, 0, j)),
        ],
        out_specs=pl.BlockSpec((None, None, b, d), lambda l, j: (l, j, 0, 0)),
        out_shape=jax.ShapeDtypeStruct((depth, N_MOD, b, d), F32),
        compiler_params=pltpu.CompilerParams(
            dimension_semantics=("arbitrary", "arbitrary")),
        name="ada_rows",
    )(c, w_ada, b_ada.reshape(depth, 1, n))


def _round_blocks(cast_in, cast_out):
    for src_ref, dst_ref in zip(cast_in, cast_out):
        dst_ref[...] = src_ref[...].astype(BF16)


def _round_block_specs(cast_jobs, n_steps, flat_step):
    in_specs, out_specs, out_shapes, nbytes = [], [], [], 0
    for w, layer in cast_jobs:
        _, r, c = w.shape
        blk = r // n_steps
        assert blk * n_steps == r and blk % BF16_ROW_TILE == 0
        in_specs.append(pl.BlockSpec((None, blk, c), lambda i, j, layer=layer: (layer, flat_step(i, j), 0)))
        out_specs.append(pl.BlockSpec((blk, c), lambda i, j: (flat_step(i, j), 0)))
        out_shapes.append(jax.ShapeDtypeStruct((r, c), BF16))
        nbytes += blk * c * (4 + 2)
    return in_specs, out_specs, out_shapes, nbytes


N_ROPE_INPUTS = 3


def _rope_kernel(*refs, n_cast):
    pos_ref, freq_ref, sign_ref = refs[:N_ROPE_INPUTS]
    cast_in = refs[N_ROPE_INPUTS:N_ROPE_INPUTS + n_cast]
    tab_ref = refs[N_ROPE_INPUTS + n_cast]
    cast_out = refs[N_ROPE_INPUTS + n_cast + 1:]
    _round_blocks(cast_in, cast_out)

    rows = pos_ref.shape[0]
    half_rows, half = rows // 2, RET_QK_DIM // 2
    pos = pos_ref[...].astype(F32)
    low = lax.broadcasted_iota(jnp.int32, (half_rows, RET_QK_DIM), 1) < half
    ang = jnp.where(low, pos[:half_rows], pos[half_rows:]) * freq_ref[...]
    cos_p, sin_p = jnp.cos(ang), jnp.sin(ang)
    cos_s, sin_s = pltpu.roll(cos_p, half, 1), pltpu.roll(sin_p, half, 1)
    sign = sign_ref[...]
    tab_ref[:half_rows, :RET_QK_DIM] = jnp.where(low, cos_p, cos_s)
    tab_ref[half_rows:, :RET_QK_DIM] = jnp.where(low, cos_s, cos_p)
    tab_ref[:half_rows, RET_QK_DIM:] = jnp.where(low, sin_p, sin_s) * sign
    tab_ref[half_rows:, RET_QK_DIM:] = jnp.where(low, sin_s, sin_p) * sign


def _rope_tables(positions, cast_jobs):
    b, s = positions.shape
    half = RET_QK_DIM // 2
    inv_freq = ROPE_BASE ** (-jnp.arange(half, dtype=F32) / half)
    freq = jnp.concatenate([inv_freq, inv_freq]).reshape(1, RET_QK_DIM)
    sign = jnp.concatenate([-jnp.ones((half,), F32), jnp.ones((half,), F32)]).reshape(1, RET_QK_DIM)
    rows = ROPE_ROWS
    steps = s // rows
    cast_in, cast_out, cast_shapes, cast_bytes = _round_block_specs(cast_jobs, b * steps, lambda i, j: i * steps + j)
    table = jax.ShapeDtypeStruct((b, s, 2 * RET_QK_DIM), F32)
    table_spec = pl.BlockSpec((None, rows, 2 * RET_QK_DIM), lambda i, j: (i, j, 0))
    return pl.pallas_call(
        functools.partial(_rope_kernel, n_cast=len(cast_jobs)),
        grid=(b, steps),
        in_specs=[
            pl.BlockSpec((None, rows, 1), lambda i, j: (i, j, 0)),
            pl.BlockSpec((1, RET_QK_DIM), lambda i, j: (0, 0)),
            pl.BlockSpec((1, RET_QK_DIM), lambda i, j: (0, 0)),
        ] + cast_in,
        out_specs=[table_spec] + cast_out,
        out_shape=[table] + cast_shapes,
        compiler_params=pltpu.CompilerParams(
            dimension_semantics=("arbitrary", "arbitrary"),
            vmem_limit_bytes=_vmem_limit(0, cast_bytes + 4 * rows * 3 * RET_QK_DIM, 0, 16 * 4 * rows * RET_QK_DIM)),
        name="rope_tables",
    )(positions.reshape(b, s, 1), freq, sign, *[w for w, _ in cast_jobs])


GELU_C0 = 0.7978845608028654
GELU_C1 = GELU_C0 * 0.044715


def _gelu_tanh(x):
    half_x = 0.5 * x
    return half_x + half_x * jnp.tanh(x * (GELU_C0 + GELU_C1 * (x * x)))


N_MIXER_INPUTS = 13


def _mixer_kernel(*refs, n_cast):
    ins = refs[:N_MIXER_INPUTS]
    cast_in = refs[N_MIXER_INPUTS:N_MIXER_INPUTS + n_cast]
    o_ref = refs[N_MIXER_INPUTS + n_cast]
    cast_out = refs[N_MIXER_INPUTS + n_cast + 1:N_MIXER_INPUTS + 2 * n_cast + 1]
    scratch = refs[N_MIXER_INPUTS + 2 * n_cast + 1:]
    _round_blocks(cast_in, cast_out)
    _mixer_step(*ins, o_ref, *scratch)


def _mixer_step(dchunk_ref, x_ref, mod_ref, prm_ref, rope_ref, win_ref, wbr_ref, wout_ref,
                wpool_ref, ws_ref, bst_ref, dintra_ref, dqk_ref,
                o_ref,
                state_ref, ptail_ref, qb_ref, qd_ref, kb_ref, kd_ref, vb_ref, sg_ref,
                u_ref, vn_ref, y_ref, m_ref):
    rows, d = x_ref.shape
    n_chunks = rows // CHUNK
    qk_w = RET_HEADS * RET_QK_DIM
    v_w = RET_HEADS * RET_V_DIM
    gdim = d // GMLP_GROUPS
    pdim = d // len(POOL_WINDOWS)
    o_q, o_k, o_v = 0, qk_w, 2 * qk_w
    o_g = o_v + v_w
    o_u = o_g + v_w
    o_vs = o_u + d
    o_p = o_vs + d
    o_gate = o_p + d
    seq_step = pl.program_id(1)
    row_slices = [slice(c * CHUNK, (c + 1) * CHUNK) for c in range(n_chunks)]
    qk_slices = [slice(hd * RET_QK_DIM, (hd + 1) * RET_QK_DIM) for hd in range(RET_HEADS)]
    v_slices = [slice(hd * RET_V_DIM, (hd + 1) * RET_V_DIM) for hd in range(RET_HEADS)]

    @pl.when(seq_step == 0)
    def _():
        state_ref[...] = jnp.zeros_like(state_ref)
        ptail_ref[...] = jnp.zeros_like(ptail_ref)

    x = x_ref[...]
    batch_row = pl.ds(pl.program_id(0), 1)
    sh1, sc1, gt1 = mod_ref[0, batch_row, :], mod_ref[1, batch_row, :], mod_ref[2, batch_row, :]
    h = (_rms(x) * (prm_ref[0:1, :] * (1.0 + sc1)) + sh1).astype(BF16)

    def proj(lo, width):
        return _dot(h, win_ref[:, lo:lo + width])

    def gate(n):
        return jax.nn.sigmoid(proj(o_gate + n * d, d))

    cosv, sinv = rope_ref[:, :RET_QK_DIM], rope_ref[:, RET_QK_DIM:]
    qf, kf = proj(o_q, qk_w), proj(o_k, qk_w)
    k_scale = RET_QK_DIM ** -0.5
    for hd, sl in enumerate(qk_slices):
        q_h, k_h = qf[:, sl], kf[:, sl]
        q_r = q_h * cosv + pltpu.roll(q_h, RET_QK_DIM // 2, 1) * sinv
        k_r = (k_h * cosv + pltpu.roll(k_h, RET_QK_DIM // 2, 1) * sinv) * k_scale
        qb_ref[:, sl] = q_r.astype(BF16)
        qd_ref[:, sl] = (q_r * dqk_ref[0, hd]).astype(BF16)
        kb_ref[:, sl] = k_r.astype(BF16)
        kd_ref[:, sl] = (k_r * dqk_ref[1, hd]).astype(BF16)
    vb_ref[...] = proj(o_v, v_w).astype(BF16)
    g = proj(o_g, v_w)
    sg_ref[...] = g * jax.nn.sigmoid(g)
    vs = _gelu_tanh(proj(o_vs, d))
    vn_ref[...] = (_rms(vs) * prm_ref[1:2, :]).astype(BF16)

    scores, kv = {}, {}
    for c, rs in enumerate(row_slices):
        for hd, sl in enumerate(qk_slices):
            s = lax.dot_general(qb_ref[rs, sl], kb_ref[rs, sl], (((1,), (1,)), ((), ())),
                                preferred_element_type=F32)
            scores[c, hd] = (s * dintra_ref[hd]).astype(BF16)
            kv[c, hd] = lax.dot_general(kd_ref[rs, sl], vb_ref[rs, v_slices[hd]], (((0,), (0,)), ((), ())),
                                        preferred_element_type=F32)

    u_ref[...] = _gelu_tanh(proj(o_u, d))

    state = [state_ref[hd] for hd in range(RET_HEADS)]
    ret = {}
    for c, rs in enumerate(row_slices):
        for hd, sl in enumerate(qk_slices):
            lhs = jnp.concatenate([scores[c, hd], qd_ref[rs, sl]], axis=1)
            rhs = jnp.concatenate([vb_ref[rs, v_slices[hd]], state[hd].astype(BF16)], axis=0)
            ret[c, hd] = _dot(lhs, rhs)
            state[hd] = dchunk_ref[hd] * state[hd] + kv[c, hd]
    for hd in range(RET_HEADS):
        state_ref[hd] = state[hd]

    p = proj(o_p, d)
    ext = jnp.concatenate([ptail_ref[...], p], axis=0)
    ptail_ref[...] = p[rows - POOL_HISTORY:, :]
    pos = seq_step * rows + lax.broadcasted_iota(jnp.int32, (rows, pdim), 0)
    pooled = []
    for gi, window in enumerate(POOL_WINDOWS):
        psl = slice(gi * pdim, (gi + 1) * pdim)
        acc = ext[:, psl]
        span = 1
        while span < window:
            acc = acc + pltpu.roll(acc, span, 0)
            span *= 2
        count = jnp.minimum(pos + 1, window).astype(F32)
        pooled.append((acc[POOL_HISTORY:, :] / count - p[:, psl]).astype(BF16))

    for c, rs in enumerate(row_slices):
        for hd, vsl in enumerate(v_slices):
            y_ref[0, rs, vsl] = (sg_ref[rs, vsl] * _rms(ret[c, hd])).astype(BF16)

    tri = lax.broadcasted_iota(jnp.int32, (CHUNK, CHUNK), 0) >= lax.broadcasted_iota(jnp.int32, (CHUNK, CHUNK), 1)
    for gi in range(GMLP_GROUPS):
        gsl = slice(gi * gdim, (gi + 1) * gdim)
        w_tri = jnp.where(tri, ws_ref[gi], 0.0).astype(BF16)
        bias = jnp.broadcast_to(bst_ref[:, gi:gi + 1], (CHUNK, gdim))
        for rs in row_slices:
            mixed = _dot(w_tri, vn_ref[rs, gsl]) + bias
            y_ref[1, rs, gsl] = (u_ref[rs, gsl] * mixed).astype(BF16)

    m_ref[...] = gate(0) * _dot(y_ref[0], wbr_ref[0:d, :])

    for gi in range(len(POOL_WINDOWS)):
        psl = slice(gi * pdim, (gi + 1) * pdim)
        mixed = _dot(pooled[gi], wpool_ref[psl, :]) + prm_ref[2:3, psl]
        y_ref[2, :, psl] = (mixed * prm_ref[3:4, psl]).astype(BF16)

    m_ref[...] += gate(1) * _dot(y_ref[1], wbr_ref[d:2 * d, :])
    m_ref[...] += gate(2) * _dot(y_ref[2], wbr_ref[2 * d:3 * d, :])
    o_ref[...] = x + gt1 * _dot(m_ref[...].astype(BF16), wout_ref[...])


def _mixer(layer, x, mod, prm, rope_t, w_in, w_branch, w_out, w_pool, ws, bs_t, dintra, dqk, dchunk, cast_jobs):
    b, s, d = x.shape
    rows = MIXER_ROWS
    steps = s // rows
    d_in = w_in.shape[1]
    v_w = RET_HEADS * RET_V_DIM
    qk_w = RET_HEADS * RET_QK_DIM
    row_spec = lambda width: pl.BlockSpec((None, rows, width), lambda i, j: (i, j, 0))
    lres = functools.partial(_layer_resident, layer)
    in_specs = [
        pl.BlockSpec(memory_space=pltpu.SMEM),
        row_spec(d),
        lres(mod.shape[1:]),
        lres(prm.shape[1:]),
        row_spec(2 * RET_QK_DIM),
        _resident((d, d_in)), _resident((N_BRANCH * d, d)), _resident((d, d)),
        _resident((d, d // len(POOL_WINDOWS))), lres(ws.shape[1:]), lres(bs_t.shape[1:]),
        _resident(dintra.shape), _resident(dqk.shape),
    ]
    assert len(in_specs) == N_MIXER_INPUTS
    cast_in, cast_out, cast_shapes, cast_bytes = _round_block_specs(cast_jobs, b * steps, lambda i, j: i * steps + j)
    in_specs += cast_in
    out_specs = [row_spec(d)] + cast_out
    out_shape = [jax.ShapeDtypeStruct((b, s, d), F32)] + cast_shapes
    scratch = [
        pltpu.VMEM((RET_HEADS, RET_QK_DIM, RET_V_DIM), F32),
        pltpu.VMEM((POOL_HISTORY, d), F32),
        pltpu.VMEM((rows, qk_w), BF16), pltpu.VMEM((rows, qk_w), BF16),
        pltpu.VMEM((rows, qk_w), BF16), pltpu.VMEM((rows, qk_w), BF16),
        pltpu.VMEM((rows, v_w), BF16),
        pltpu.VMEM((rows, v_w), F32),
        pltpu.VMEM((rows, d), F32),
        pltpu.VMEM((rows, d), BF16),
        pltpu.VMEM((N_BRANCH, rows, d), BF16),
        pltpu.VMEM((rows, d), F32),
    ]
    resident = 2 * (d * d_in + N_BRANCH * d * d + d * d + w_pool.size) + 4 * (
        ws[0].size + bs_t[0].size + dintra.size + dqk.size + mod[0].size + prm[0].size)
    streamed = 4 * rows * (2 * d + 2 * RET_QK_DIM) + cast_bytes
    scratch_bytes = 4 * (RET_HEADS * RET_QK_DIM * RET_V_DIM + POOL_HISTORY * d) + rows * (
        2 * 4 * qk_w + 2 * v_w + 4 * v_w + 4 * d + 2 * d + 2 * N_BRANCH * d + 4 * d)
    temps = 8 * 4 * rows * d
    return pl.pallas_call(
        functools.partial(_mixer_kernel, n_cast=len(cast_jobs)),
        grid=(b, steps),
        in_specs=in_specs,
        out_specs=out_specs,
        out_shape=out_shape,
        scratch_shapes=scratch,
        compiler_params=pltpu.CompilerParams(
            dimension_semantics=("arbitrary", "arbitrary"),
            vmem_limit_bytes=_vmem_limit(resident, streamed, scratch_bytes, temps)),
        name="token_mixer",
    )(dchunk, x, mod, prm, rope_t, w_in, w_branch, w_out, w_pool, ws, bs_t, dintra, dqk,
      *[w for w, _ in cast_jobs])


def _ffn_kernel(x_ref, mod_ref, n2_ref, w1_ref, w2_ref, fn_ref, o_ref, hid_ref, *, final):
    x = x_ref[...]
    batch_row = pl.ds(pl.program_id(0), 1)
    sh2, sc2, gt2 = mod_ref[3, batch_row, :], mod_ref[4, batch_row, :], mod_ref[5, batch_row, :]
    h = (_rms(x) * (n2_ref[...] * (1.0 + sc2)) + sh2).astype(BF16)
    d_ff = w1_ref.shape[1]
    for lo in range(0, d_ff, FF_BLOCK):
        hid = _dot(h, w1_ref[:, lo:lo + FF_BLOCK])
        hid_ref[:, lo:lo + FF_BLOCK] = jnp.square(jnp.maximum(hid, 0.0)).astype(BF16)
    y = x + gt2 * _dot(hid_ref[...], w2_ref[...])
    if final:
        y = _rms(y) * fn_ref[...]
    o_ref[...] = y


def _ffn(layer, x, mod, norm2, w1, w2, final_norm, final):
    b, s, d = x.shape
    rows = FFN_ROWS
    d_ff = w1.shape[1]
    row_spec = pl.BlockSpec((None, rows, d), lambda i, j: (i, j, 0))
    lres = functools.partial(_layer_resident, layer)
    resident = 2 * 2 * d * d_ff + 4 * 2 * d + 4 * mod[0].size
    streamed = 4 * 2 * rows * d
    temps = rows * (2 * d + 6 * FF_BLOCK + 3 * 4 * d)
    return pl.pallas_call(
        functools.partial(_ffn_kernel, final=final),
        grid=(b, s // rows),
        in_specs=[
            row_spec,
            lres(mod.shape[1:]),
            lres((1, d)), _resident((d, d_ff)), _resident((d_ff, d)), _resident((1, d)),
        ],
        out_specs=row_spec,
        out_shape=jax.ShapeDtypeStruct((b, s, d), F32),
        scratch_shapes=[pltpu.VMEM((rows, d_ff), BF16)],
        compiler_params=pltpu.CompilerParams(
            dimension_semantics=("arbitrary", "arbitrary"),
            vmem_limit_bytes=_vmem_limit(resident, streamed, 2 * rows * d_ff, temps)),
        name="channel_mlp",
    )(x, mod, norm2, w1, w2, final_norm)


def _retention_decay_tables(rows):
    log_gamma = jnp.log1p(-jnp.power(2.0, -5.0 - jnp.arange(RET_HEADS, dtype=F32)))
    pos = jnp.arange(CHUNK, dtype=F32)
    rel = pos[:, None] - pos[None, :]
    causal = rel >= 0
    intra = jnp.where(causal[None], jnp.exp(log_gamma[:, None, None] * jnp.where(causal, rel, 0.0)[None]), 0.0)
    decay_q = jnp.exp(log_gamma[:, None] * (pos + 1.0)[None])
    decay_k = jnp.exp(log_gamma[:, None] * (CHUNK - 1.0 - pos)[None])
    decay_chunk = jnp.exp(log_gamma * CHUNK)
    reps = rows // CHUNK
    widen = lambda t: jnp.broadcast_to(jnp.tile(t, (1, reps))[:, :, None], (RET_HEADS, rows, RET_QK_DIM))
    return intra, jnp.stack([widen(decay_q), widen(decay_k)]), decay_chunk


def kernel(x, c, positions, w_ada, b_ada, norm1, norm2, w_in, ws_gmlp, bs_gmlp, vnorm_gmlp, w_pool, b_pool,
           pool_scale, w_branch, w_out, w_ff1, w_ff2, final_norm):
    depth = w_in.shape[0]
    b, s, d = x.shape
    assert s % MIXER_ROWS == 0 and s % FFN_ROWS == 0 and s % ROPE_ROWS == 0 and MIXER_ROWS % CHUNK == 0
    assert w_ff1.shape[2] % FF_BLOCK == 0

    mod = _ada_rows(c, w_ada, b_ada)
    dintra, dqk, dchunk = _retention_decay_tables(MIXER_ROWS)
    rows_of = lambda v: v.reshape(depth, 1, d)
    prm = jnp.concatenate([rows_of(v) for v in (norm1, vnorm_gmlp, b_pool, pool_scale)], axis=1)
    norm2 = rows_of(norm2)
    bs_t = jnp.swapaxes(bs_gmlp, 1, 2)

    mixer_w = (w_in, w_branch, w_out, w_pool)
    as_rows = lambda w: w.reshape(depth, -1, w.shape[-1])
    rope_t, *next_mixer = _rope_tables(positions, [(as_rows(w), 0) for w in mixer_w])
    for l in range(depth):
        mixer_bf16 = next_mixer
        jobs = [(as_rows(w_ff1), l), (as_rows(w_ff2), l)]
        if l + 1 < depth:
            jobs += [(as_rows(w), l + 1) for w in mixer_w]
        x, ff1, ff2, *next_mixer = _mixer(l, x, mod, prm, rope_t, *mixer_bf16, ws_gmlp, bs_t,
                                          dintra, dqk, dchunk, jobs)
        x = _ffn(l, x, mod, norm2, ff1, ff2, final_norm.reshape(1, d), final=(l == depth - 1))
    return x
```

```python
import functools

import jax
import jax.numpy as jnp
from jax import lax
from jax.experimental import pallas as pl
from jax.experimental.pallas import tpu as pltpu

CHUNK = 128
RET_HEADS = 4
RET_QK_DIM = 128
RET_V_DIM = 256
GMLP_GROUPS = 4
POOL_WINDOWS = (2, 4, 8, 16)
POOL_HISTORY = 16
N_BRANCH = 3
N_MOD = 6
ROPE_BASE = 10000.0
EPS = 1e-6

VMEM_BYTES_V7X = 64 * 1024 * 1024
VMEM_REQUEST_CAP_V7X = VMEM_BYTES_V7X - 4 * 1024 * 1024
BF16_ROW_TILE = 16

MIXER_ROWS = 256
FFN_ROWS = 1024
FF_BLOCK = 1024
ROPE_ROWS = 2048

F32 = jnp.float32
BF16 = jnp.bfloat16


def _dot(a, b):
    return jnp.dot(a, b, preferred_element_type=F32)


def _rms(x):
    return x * lax.rsqrt(jnp.mean(x * x, axis=-1, keepdims=True) + EPS)


def _resident(shape):
    zeros = (0,) * len(shape)
    return pl.BlockSpec(shape, lambda *_: zeros, pipeline_mode=pl.Buffered(1))


def _layer_resident(layer, shape):
    index = (layer,) + (0,) * len(shape)
    return pl.BlockSpec((None,) + tuple(shape), lambda *_: index, pipeline_mode=pl.Buffered(1))


def _vmem_limit(resident_bytes, streamed_bytes, scratch_bytes, temp_bytes):
    need = resident_bytes + 2 * streamed_bytes + scratch_bytes + temp_bytes
    return min(int(need), VMEM_REQUEST_CAP_V7X)


def _ada_kernel(c_ref, w_ref, b_ref, o_ref):
    c = c_ref[...]
    c_act = (c * jax.nn.sigmoid(c)).astype(BF16)
    o_ref[...] = _dot(c_act, w_ref[...].astype(BF16)) + b_ref[...]


def _ada_rows(c, w_ada, b_ada):
    depth, d, n = w_ada.shape
    b = c.shape[0]
    assert n == N_MOD * d
    return pl.pallas_call(
        _ada_kernel,
        grid=(depth, N_MOD),
        in_specs=[
            pl.BlockSpec((b, d), lambda l, j: (0, 0)),
            pl.BlockSpec((None, d, d), lambda l, j: (l, 0, j)),
            pl.BlockSpec((None, 1, d), lambda l, j: (l, 0, j)),
        ],
        out_specs=pl.BlockSpec((None, None, b, d), lambda l, j: (l, j, 0, 0)),
        out_shape=jax.ShapeDtypeStruct((depth, N_MOD, b, d), F32),
        compiler_params=pltpu.CompilerParams(
            dimension_semantics=("arbitrary", "arbitrary")),
        name="ada_rows",
    )(c, w_ada, b_ada.reshape(depth, 1, n))


def _round_blocks(cast_in, cast_out):
    for src_ref, dst_ref in zip(cast_in, cast_out):
        dst_ref[...] = src_ref[...].astype(BF16)


def _round_block_specs(cast_jobs, n_steps, flat_step):
    in_specs, out_specs, out_shapes, nbytes = [], [], [], 0
    for w, layer in cast_jobs:
        _, r, c = w.shape
        blk = r // n_steps
        assert blk * n_steps == r and blk % BF16_ROW_TILE == 0
        in_specs.append(pl.BlockSpec((None, blk, c), lambda i, j, layer=layer: (layer, flat_step(i, j), 0)))
        out_specs.append(pl.BlockSpec((blk, c), lambda i, j: (flat_step(i, j), 0)))
        out_shapes.append(jax.ShapeDtypeStruct((r, c), BF16))
        nbytes += blk * c * (4 + 2)
    return in_specs, out_specs, out_shapes, nbytes


N_ROPE_INPUTS = 3


def _rope_kernel(*refs, n_cast):
    pos_ref, freq_ref, sign_ref = refs[:N_ROPE_INPUTS]
    cast_in = refs[N_ROPE_INPUTS:N_ROPE_INPUTS + n_cast]
    tab_ref = refs[N_ROPE_INPUTS + n_cast]
    cast_out = refs[N_ROPE_INPUTS + n_cast + 1:]
    _round_blocks(cast_in, cast_out)

    rows = pos_ref.shape[0]
    half_rows, half = rows // 2, RET_QK_DIM // 2
    pos = pos_ref[...].astype(F32)
    low = lax.broadcasted_iota(jnp.int32, (half_rows, RET_QK_DIM), 1) < half
    ang = jnp.where(low, pos[:half_rows], pos[half_rows:]) * freq_ref[...]
    cos_p, sin_p = jnp.cos(ang), jnp.sin(ang)
    cos_s, sin_s = pltpu.roll(cos_p, half, 1), pltpu.roll(sin_p, half, 1)
    sign = sign_ref[...]
    tab_ref[:half_rows, :RET_QK_DIM] = jnp.where(low, cos_p, cos_s)
    tab_ref[half_rows:, :RET_QK_DIM] = jnp.where(low, cos_s, cos_p)
    tab_ref[:half_rows, RET_QK_DIM:] = jnp.where(low, sin_p, sin_s) * sign
    tab_ref[half_rows:, RET_QK_DIM:] = jnp.where(low, sin_s, sin_p) * sign


def _rope_tables(positions, cast_jobs):
    b, s = positions.shape
    half = RET_QK_DIM // 2
    inv_freq = ROPE_BASE ** (-jnp.arange(half, dtype=F32) / half)
    freq = jnp.concatenate([inv_freq, inv_freq]).reshape(1, RET_QK_DIM)
    sign = jnp.concatenate([-jnp.ones((half,), F32), jnp.ones((half,), F32)]).reshape(1, RET_QK_DIM)
    rows = ROPE_ROWS
    steps = s // rows
    cast_in, cast_out, cast_shapes, cast_bytes = _round_block_specs(cast_jobs, b * steps, lambda i, j: i * steps + j)
    table = jax.ShapeDtypeStruct((b, s, 2 * RET_QK_DIM), F32)
    table_spec = pl.BlockSpec((None, rows, 2 * RET_QK_DIM), lambda i, j: (i, j, 0))
    return pl.pallas_call(
        functools.partial(_rope_kernel, n_cast=len(cast_jobs)),
        grid=(b, steps),
        in_specs=[
            pl.BlockSpec((None, rows, 1), lambda i, j: (i, j, 0)),
            pl.BlockSpec((1, RET_QK_DIM), lambda i, j: (0, 0)),
            pl.BlockSpec((1, RET_QK_DIM), lambda i, j: (0, 0)),
        ] + cast_in,
        out_specs=[table_spec] + cast_out,
        out_shape=[table] + cast_shapes,
        compiler_params=pltpu.CompilerParams(
            dimension_semantics=("arbitrary", "arbitrary"),
            vmem_limit_bytes=_vmem_limit(0, cast_bytes + 4 * rows * 3 * RET_QK_DIM, 0, 16 * 4 * rows * RET_QK_DIM)),
        name="rope_tables",
    )(positions.reshape(b, s, 1), freq, sign, *[w for w, _ in cast_jobs])


GELU_C0 = 0.7978845608028654
GELU_C1 = GELU_C0 * 0.044715


def _gelu_tanh(x):
    half_x = 0.5 * x
    return half_x + half_x * jnp.tanh(x * (GELU_C0 + GELU_C1 * (x * x)))


N_MIXER_INPUTS = 13


def _mixer_kernel(*refs, n_cast):
    ins = refs[:N_MIXER_INPUTS]
    cast_in = refs[N_MIXER_INPUTS:N_MIXER_INPUTS + n_cast]
    o_ref = refs[N_MIXER_INPUTS + n_cast]
    cast_out = refs[N_MIXER_INPUTS + n_cast + 1:N_MIXER_INPUTS + 2 * n_cast + 1]
    scratch = refs[N_MIXER_INPUTS + 2 * n_cast + 1:]
    _mixer_step(*ins, o_ref, *scratch)
    _round_blocks(cast_in, cast_out)


def _mixer_step(dchunk_ref, x_ref, mod_ref, prm_ref, rope_ref, win_ref, wbr_ref, wout_ref,
                wpool_ref, ws_ref, bst_ref, dintra_ref, dqk_ref,
                o_ref,
                state_ref, ptail_ref, qb_ref, qd_ref, kb_ref, kd_ref, vb_ref, sg_ref,
                u_ref, vn_ref, y_ref, m_ref):
    rows, d = x_ref.shape
    n_chunks = rows // CHUNK
    qk_w = RET_HEADS * RET_QK_DIM
    v_w = RET_HEADS * RET_V_DIM
    gdim = d // GMLP_GROUPS
    pdim = d // len(POOL_WINDOWS)
    o_q, o_k, o_v = 0, qk_w, 2 * qk_w
    o_g = o_v + v_w
    o_u = o_g + v_w
    o_vs = o_u + d
    o_p = o_vs + d
    o_gate = o_p + d
    seq_step = pl.program_id(1)
    row_slices = [slice(c * CHUNK, (c + 1) * CHUNK) for c in range(n_chunks)]
    qk_slices = [slice(hd * RET_QK_DIM, (hd + 1) * RET_QK_DIM) for hd in range(RET_HEADS)]
    v_slices = [slice(hd * RET_V_DIM, (hd + 1) * RET_V_DIM) for hd in range(RET_HEADS)]

    @pl.when(seq_step == 0)
    def _():
        state_ref[...] = jnp.zeros_like(state_ref)
        ptail_ref[...] = jnp.zeros_like(ptail_ref)

    x = x_ref[...]
    batch_row = pl.ds(pl.program_id(0), 1)
    sh1, sc1, gt1 = mod_ref[0, batch_row, :], mod_ref[1, batch_row, :], mod_ref[2, batch_row, :]
    h = (_rms(x) * (prm_ref[0:1, :] * (1.0 + sc1)) + sh1).astype(BF16)

    def proj(lo, width):
        return _dot(h, win_ref[:, lo:lo + width])

    def gate(n):
        return jax.nn.sigmoid(proj(o_gate + n * d, d))

    cosv, sinv = rope_ref[:, :RET_QK_DIM], rope_ref[:, RET_QK_DIM:]
    qf, kf = proj(o_q, qk_w), proj(o_k, qk_w)
    k_scale = RET_QK_DIM ** -0.5
    for hd, sl in enumerate(qk_slices):
        q_h, k_h = qf[:, sl], kf[:, sl]
        q_r = q_h * cosv + pltpu.roll(q_h, RET_QK_DIM // 2, 1) * sinv
        k_r = (k_h * cosv + pltpu.roll(k_h, RET_QK_DIM // 2, 1) * sinv) * k_scale
        qb_ref[:, sl] = q_r.astype(BF16)
        qd_ref[:, sl] = (q_r * dqk_ref[0, hd]).astype(BF16)
        kb_ref[:, sl] = k_r.astype(BF16)
        kd_ref[:, sl] = (k_r * dqk_ref[1, hd]).astype(BF16)
    vb_ref[...] = proj(o_v, v_w).astype(BF16)
    g = proj(o_g, v_w)
    sg_ref[...] = g * jax.nn.sigmoid(g)
    vs = _gelu_tanh(proj(o_vs, d))
    vn_ref[...] = (_rms(vs) * prm_ref[1:2, :]).astype(BF16)

    scores, kv = {}, {}
    for c, rs in enumerate(row_slices):
        for hd, sl in enumerate(qk_slices):
            s = lax.dot_general(qb_ref[rs, sl], kb_ref[rs, sl], (((1,), (1,)), ((), ())),
                                preferred_element_type=F32)
            scores[c, hd] = (s * dintra_ref[hd]).astype(BF16)
            kv[c, hd] = lax.dot_general(kd_ref[rs, sl], vb_ref[rs, v_slices[hd]], (((0,), (0,)), ((), ())),
                                        preferred_element_type=F32)

    u_ref[...] = _gelu_tanh(proj(o_u, d))

    state = [state_ref[hd] for hd in range(RET_HEADS)]
    ret = {}
    for c, rs in enumerate(row_slices):
        for hd, sl in enumerate(qk_slices):
            lhs = jnp.concatenate([scores[c, hd], qd_ref[rs, sl]], axis=1)
            rhs = jnp.concatenate([vb_ref[rs, v_slices[hd]], state[hd].astype(BF16)], axis=0)
            ret[c, hd] = _dot(lhs, rhs)
            state[hd] = dchunk_ref[hd] * state[hd] + kv[c, hd]
    for hd in range(RET_HEADS):
        state_ref[hd] = state[hd]

    p = proj(o_p, d)
    ext = jnp.concatenate([ptail_ref[...], p], axis=0)
    ptail_ref[...] = p[rows - POOL_HISTORY:, :]
    pos = seq_step * rows + lax.broadcasted_iota(jnp.int32, (rows, pdim), 0)
    pooled = []
    for gi, window in enumerate(POOL_WINDOWS):
        psl = slice(gi * pdim, (gi + 1) * pdim)
        acc = ext[:, psl]
        span = 1
        while span < window:
            acc = acc + pltpu.roll(acc, span, 0)
            span *= 2
        count = jnp.minimum(pos + 1, window).astype(F32)
        pooled.append((acc[POOL_HISTORY:, :] / count - p[:, psl]).astype(BF16))

    for c, rs in enumerate(row_slices):
        for hd, vsl in enumerate(v_slices):
            y_ref[0, rs, vsl] = (sg_ref[rs, vsl] * _rms(ret[c, hd])).astype(BF16)

    tri = lax.broadcasted_iota(jnp.int32, (CHUNK, CHUNK), 0) >= lax.broadcasted_iota(jnp.int32, (CHUNK, CHUNK), 1)
    for gi in range(GMLP_GROUPS):
        gsl = slice(gi * gdim, (gi + 1) * gdim)
        w_tri = jnp.where(tri, ws_ref[gi], 0.0).astype(BF16)
        bias = jnp.broadcast_to(bst_ref[:, gi:gi + 1], (CHUNK, gdim))
        for rs in row_slices:
            mixed = _dot(w_tri, vn_ref[rs, gsl]) + bias
            y_ref[1, rs, gsl] = (u_ref[rs, gsl] * mixed).astype(BF16)

    m_ref[...] = gate(0) * _dot(y_ref[0], wbr_ref[0])

    for gi in range(len(POOL_WINDOWS)):
        psl = slice(gi * pdim, (gi + 1) * pdim)
        mixed = _dot(pooled[gi], wpool_ref[gi]) + prm_ref[2:3, psl]
        y_ref[2, :, psl] = (mixed * prm_ref[3:4, psl]).astype(BF16)

    m_ref[...] += gate(1) * _dot(y_ref[1], wbr_ref[1])
    m_ref[...] += gate(2) * _dot(y_ref[2], wbr_ref[2])
    o_ref[...] = x + gt1 * _dot(m_ref[...].astype(BF16), wout_ref[...])


def _mixer(layer, x, mod, prm, rope_t, w_in, w_branch, w_out, w_pool, ws, bs_t, dintra, dqk, dchunk, cast_jobs):
    b, s, d = x.shape
    rows = MIXER_ROWS
    steps = s // rows
    d_in = w_in.shape[1]
    v_w = RET_HEADS * RET_V_DIM
    qk_w = RET_HEADS * RET_QK_DIM
    row_spec = lambda width: pl.BlockSpec((None, rows, width), lambda i, j: (i, j, 0))
    lres = functools.partial(_layer_resident, layer)
    in_specs = [
        pl.BlockSpec(memory_space=pltpu.SMEM),
        row_spec(d),
        lres(mod.shape[1:]),
        lres(prm.shape[1:]),
        row_spec(2 * RET_QK_DIM),
        _resident((d, d_in)), _resident((N_BRANCH, d, d)), _resident((d, d)),
        _resident(w_pool.shape), lres(ws.shape[1:]), lres(bs_t.shape[1:]),
        _resident(dintra.shape), _resident(dqk.shape),
    ]
    assert len(in_specs) == N_MIXER_INPUTS
    cast_in, cast_out, cast_shapes, cast_bytes = _round_block_specs(cast_jobs, b * steps, lambda i, j: i * steps + j)
    in_specs += cast_in
    out_specs = [row_spec(d)] + cast_out
    out_shape = [jax.ShapeDtypeStruct((b, s, d), F32)] + cast_shapes
    scratch = [
        pltpu.VMEM((RET_HEADS, RET_QK_DIM, RET_V_DIM), F32),
        pltpu.VMEM((POOL_HISTORY, d), F32),
        pltpu.VMEM((rows, qk_w), BF16), pltpu.VMEM((rows, qk_w), BF16),
        pltpu.VMEM((rows, qk_w), BF16), pltpu.VMEM((rows, qk_w), BF16),
        pltpu.VMEM((rows, v_w), BF16),
        pltpu.VMEM((rows, v_w), F32),
        pltpu.VMEM((rows, d), F32),
        pltpu.VMEM((rows, d), BF16),
        pltpu.VMEM((N_BRANCH, rows, d), BF16),
        pltpu.VMEM((rows, d), F32),
    ]
    resident = 2 * (d * d_in + N_BRANCH * d * d + d * d + w_pool.size) + 4 * (
        ws[0].size + bs_t[0].size + dintra.size + dqk.size + mod[0].size + prm[0].size)
    streamed = 4 * rows * (2 * d + 2 * RET_QK_DIM) + cast_bytes
    scratch_bytes = 4 * (RET_HEADS * RET_QK_DIM * RET_V_DIM + POOL_HISTORY * d) + rows * (
        2 * 4 * qk_w + 2 * v_w + 4 * v_w + 4 * d + 2 * d + 2 * N_BRANCH * d + 4 * d)
    temps = 8 * 4 * rows * d
    return pl.pallas_call(
        functools.partial(_mixer_kernel, n_cast=len(cast_jobs)),
        grid=(b, steps),
        in_specs=in_specs,
        out_specs=out_specs,
        out_shape=out_shape,
        scratch_shapes=scratch,
        compiler_params=pltpu.CompilerParams(
            dimension_semantics=("arbitrary", "arbitrary"),
            vmem_limit_bytes=_vmem_limit(resident, streamed, scratch_bytes, temps)),
        name="token_mixer",
    )(dchunk, x, mod, prm, rope_t, w_in, w_branch, w_out, w_pool, ws, bs_t, dintra, dqk,
      *[w for w, _ in cast_jobs])


def _ffn_kernel(x_ref, mod_ref, n2_ref, w1_ref, w2_ref, fn_ref, o_ref, hid_ref, *, final):
    x = x_ref[...]
    batch_row = pl.ds(pl.program_id(0), 1)
    sh2, sc2, gt2 = mod_ref[3, batch_row, :], mod_ref[4, batch_row, :], mod_ref[5, batch_row, :]
    h = (_rms(x) * (n2_ref[...] * (1.0 + sc2)) + sh2).astype(BF16)
    d_ff = w1_ref.shape[1]
    for lo in range(0, d_ff, FF_BLOCK):
        hid = _dot(h, w1_ref[:, lo:lo + FF_BLOCK])
        hid_ref[:, lo:lo + FF_BLOCK] = jnp.square(jnp.maximum(hid, 0.0)).astype(BF16)
    y = x + gt2 * _dot(hid_ref[...], w2_ref[...])
    if final:
        y = _rms(y) * fn_ref[...]
    o_ref[...] = y


def _ffn(layer, x, mod, norm2, w1, w2, final_norm, final):
    b, s, d = x.shape
    rows = FFN_ROWS
    d_ff = w1.shape[1]
    row_spec = pl.BlockSpec((None, rows, d), lambda i, j: (i, j, 0))
    lres = functools.partial(_layer_resident, layer)
    resident = 2 * 2 * d * d_ff + 4 * 2 * d + 4 * mod[0].size
    streamed = 4 * 2 * rows * d
    temps = rows * (2 * d + 6 * FF_BLOCK + 3 * 4 * d)
    return pl.pallas_call(
        functools.partial(_ffn_kernel, final=final),
        grid=(b, s // rows),
        in_specs=[
            row_spec,
            lres(mod.shape[1:]),
            lres((1, d)), _resident((d, d_ff)), _resident((d_ff, d)), _resident((1, d)),
        ],
        out_specs=row_spec,
        out_shape=jax.ShapeDtypeStruct((b, s, d), F32),
        scratch_shapes=[pltpu.VMEM((rows, d_ff), BF16)],
        compiler_params=pltpu.CompilerParams(
            dimension_semantics=("arbitrary", "arbitrary"),
            vmem_limit_bytes=_vmem_limit(resident, streamed, 2 * rows * d_ff, temps)),
        name="channel_mlp",
    )(x, mod, norm2, w1, w2, final_norm)


def _retention_decay_tables(rows):
    log_gamma = jnp.log1p(-jnp.power(2.0, -5.0 - jnp.arange(RET_HEADS, dtype=F32)))
    pos = jnp.arange(CHUNK, dtype=F32)
    rel = pos[:, None] - pos[None, :]
    causal = rel >= 0
    intra = jnp.where(causal[None], jnp.exp(log_gamma[:, None, None] * jnp.where(causal, rel, 0.0)[None]), 0.0)
    decay_q = jnp.exp(log_gamma[:, None] * (pos + 1.0)[None])
    decay_k = jnp.exp(log_gamma[:, None] * (CHUNK - 1.0 - pos)[None])
    decay_chunk = jnp.exp(log_gamma * CHUNK)
    reps = rows // CHUNK
    widen = lambda t: jnp.broadcast_to(jnp.tile(t, (1, reps))[:, :, None], (RET_HEADS, rows, RET_QK_DIM))
    return intra, jnp.stack([widen(decay_q), widen(decay_k)]), decay_chunk


def kernel(x, c, positions, w_ada, b_ada, norm1, norm2, w_in, ws_gmlp, bs_gmlp, vnorm_gmlp, w_pool, b_pool,
           pool_scale, w_branch, w_out, w_ff1, w_ff2, final_norm):
    depth = w_in.shape[0]
    b, s, d = x.shape
    assert s % MIXER_ROWS == 0 and s % FFN_ROWS == 0 and s % ROPE_ROWS == 0 and MIXER_ROWS % CHUNK == 0
    assert w_ff1.shape[2] % FF_BLOCK == 0

    mod = _ada_rows(c, w_ada, b_ada)
    dintra, dqk, dchunk = _retention_decay_tables(MIXER_ROWS)
    rows_of = lambda v: v.reshape(depth, 1, d)
    prm = jnp.concatenate([rows_of(v) for v in (norm1, vnorm_gmlp, b_pool, pool_scale)], axis=1)
    norm2 = rows_of(norm2)
    bs_t = jnp.swapaxes(bs_gmlp, 1, 2)

    mixer_w = (w_in, w_branch, w_out, w_pool)
    as_rows = lambda w: w.reshape(depth, -1, w.shape[-1])
    rope_t, *next_mixer = _rope_tables(positions, [(as_rows(w), 0) for w in mixer_w])
    for l in range(depth):
        mixer_bf16 = [w.reshape(ref.shape[1:]) for w, ref in zip(next_mixer, mixer_w)]
        jobs = [(as_rows(w_ff1), l), (as_rows(w_ff2), l)]
        if l + 1 < depth:
            jobs += [(as_rows(w), l + 1) for w in mixer_w]
        x, ff1, ff2, *next_mixer = _mixer(l, x, mod, prm, rope_t, *mixer_bf16, ws_gmlp, bs_t,
                                          dintra, dqk, dchunk, jobs)
        x = _ffn(l, x, mod, norm2, ff1, ff2, final_norm.reshape(1, d), final=(l == depth - 1))
    return x
```

```python
import functools

import jax
import jax.numpy as jnp
from jax import lax
from jax.experimental import pallas as pl
from jax.experimental.pallas import tpu as pltpu

CHUNK = 128
RET_HEADS = 4
RET_QK_DIM = 128
RET_V_DIM = 256
GMLP_GROUPS = 4
POOL_WINDOWS = (2, 4, 8, 16)
POOL_HISTORY = 16
N_BRANCH = 3
N_MOD = 6
ROPE_BASE = 10000.0
EPS = 1e-6

VMEM_BYTES_V7X = 64 * 1024 * 1024
VMEM_REQUEST_CAP_V7X = VMEM_BYTES_V7X - 4 * 1024 * 1024
BF16_ROW_TILE = 16

MIXER_ROWS = 256
FFN_ROWS = 1024
FF_BLOCK = 1024
ROPE_ROWS = 2048

F32 = jnp.float32
BF16 = jnp.bfloat16


def _dot(a, b):
    return jnp.dot(a, b, preferred_element_type=F32)


def _rms(x):
    return x * lax.rsqrt(jnp.mean(x * x, axis=-1, keepdims=True) + EPS)


def _resident(shape):
    zeros = (0,) * len(shape)
    return pl.BlockSpec(shape, lambda *_: zeros, pipeline_mode=pl.Buffered(1))


def _layer_resident(layer, shape):
    index = (layer,) + (0,) * len(shape)
    return pl.BlockSpec((None,) + tuple(shape), lambda *_: index, pipeline_mode=pl.Buffered(1))


def _vmem_limit(resident_bytes, streamed_bytes, scratch_bytes, temp_bytes):
    need = resident_bytes + 2 * streamed_bytes + scratch_bytes + temp_bytes
    return min(int(need), VMEM_REQUEST_CAP_V7X)


def _ada_kernel(c_ref, w_ref, b_ref, o_ref):
    c = c_ref[...]
    c_act = (c * jax.nn.sigmoid(c)).astype(BF16)
    o_ref[...] = _dot(c_act, w_ref[...].astype(BF16)) + b_ref[...]


def _ada_rows(c, w_ada, b_ada):
    depth, d, n = w_ada.shape
    b = c.shape[0]
    assert n == N_MOD * d
    return pl.pallas_call(
        _ada_kernel,
        grid=(depth, N_MOD),
        in_specs=[
            pl.BlockSpec((b, d), lambda l, j: (0, 0)),
            pl.BlockSpec((None, d, d), lambda l, j: (l, 0, j)),
            pl.BlockSpec((None, 1, d), lambda l, j: (l, 0, j)),
        ],
        out_specs=pl.BlockSpec((None, None, b, d), lambda l, j: (l, j, 0, 0)),
        out_shape=jax.ShapeDtypeStruct((depth, N_MOD, b, d), F32),
        compiler_params=pltpu.CompilerParams(
            dimension_semantics=("arbitrary", "arbitrary")),
        name="ada_rows",
    )(c, w_ada, b_ada.reshape(depth, 1, n))


def _round_blocks(cast_in, cast_out):
    for src_ref, dst_ref in zip(cast_in, cast_out):
        dst_ref[...] = src_ref[...].astype(BF16)


def _round_block_specs(cast_jobs, n_steps, flat_step):
    in_specs, out_specs, out_shapes, nbytes = [], [], [], 0
    for w, layer in cast_jobs:
        _, r, c = w.shape
        blk = r // n_steps
        assert blk * n_steps == r and blk % BF16_ROW_TILE == 0
        in_specs.append(pl.BlockSpec((None, blk, c), lambda i, j, layer=layer: (layer, flat_step(i, j), 0)))
        out_specs.append(pl.BlockSpec((blk, c), lambda i, j: (flat_step(i, j), 0)))
        out_shapes.append(jax.ShapeDtypeStruct((r, c), BF16))
        nbytes += blk * c * (4 + 2)
    return in_specs, out_specs, out_shapes, nbytes


N_ROPE_INPUTS = 3


def _rope_kernel(*refs, n_cast):
    pos_ref, freq_ref, sign_ref = refs[:N_ROPE_INPUTS]
    cast_in = refs[N_ROPE_INPUTS:N_ROPE_INPUTS + n_cast]
    tab_ref = refs[N_ROPE_INPUTS + n_cast]
    cast_out = refs[N_ROPE_INPUTS + n_cast + 1:]
    _round_blocks(cast_in, cast_out)

    rows = pos_ref.shape[0]
    half_rows, half = rows // 2, RET_QK_DIM // 2
    pos = pos_ref[...].astype(F32)
    low = lax.broadcasted_iota(jnp.int32, (half_rows, RET_QK_DIM), 1) < half
    ang = jnp.where(low, pos[:half_rows], pos[half_rows:]) * freq_ref[...]
    cos_p, sin_p = jnp.cos(ang), jnp.sin(ang)
    cos_s, sin_s = pltpu.roll(cos_p, half, 1), pltpu.roll(sin_p, half, 1)
    sign = sign_ref[...]
    tab_ref[:half_rows, :RET_QK_DIM] = jnp.where(low, cos_p, cos_s)
    tab_ref[half_rows:, :RET_QK_DIM] = jnp.where(low, cos_s, cos_p)
    tab_ref[:half_rows, RET_QK_DIM:] = jnp.where(low, sin_p, sin_s) * sign
    tab_ref[half_rows:, RET_QK_DIM:] = jnp.where(low, sin_s, sin_p) * sign


def _rope_tables(positions, cast_jobs):
    b, s = positions.shape
    half = RET_QK_DIM // 2
    inv_freq = ROPE_BASE ** (-jnp.arange(half, dtype=F32) / half)
    freq = jnp.concatenate([inv_freq, inv_freq]).reshape(1, RET_QK_DIM)
    sign = jnp.concatenate([-jnp.ones((half,), F32), jnp.ones((half,), F32)]).reshape(1, RET_QK_DIM)
    rows = ROPE_ROWS
    steps = s // rows
    cast_in, cast_out, cast_shapes, cast_bytes = _round_block_specs(cast_jobs, b * steps, lambda i, j: i * steps + j)
    table = jax.ShapeDtypeStruct((b, s, 2 * RET_QK_DIM), F32)
    table_spec = pl.BlockSpec((None, rows, 2 * RET_QK_DIM), lambda i, j: (i, j, 0))
    return pl.pallas_call(
        functools.partial(_rope_kernel, n_cast=len(cast_jobs)),
        grid=(b, steps),
        in_specs=[
            pl.BlockSpec((None, rows, 1), lambda i, j: (i, j, 0)),
            pl.BlockSpec((1, RET_QK_DIM), lambda i, j: (0, 0)),
            pl.BlockSpec((1, RET_QK_DIM), lambda i, j: (0, 0)),
        ] + cast_in,
        out_specs=[table_spec] + cast_out,
        out_shape=[table] + cast_shapes,
        compiler_params=pltpu.CompilerParams(
            dimension_semantics=("arbitrary", "arbitrary"),
            vmem_limit_bytes=_vmem_limit(0, cast_bytes + 4 * rows * 3 * RET_QK_DIM, 0, 16 * 4 * rows * RET_QK_DIM)),
        name="rope_tables",
    )(positions.reshape(b, s, 1), freq, sign, *[w for w, _ in cast_jobs])


GELU_C0 = 0.7978845608028654
GELU_C1 = GELU_C0 * 0.044715


def _sigmoid(x):
    return 0.5 * jnp.tanh(0.5 * x) + 0.5


def _gelu_tanh(x):
    half_x = 0.5 * x
    return half_x + half_x * jnp.tanh(x * (GELU_C0 + GELU_C1 * (x * x)))


N_MIXER_INPUTS = 13


def _mixer_kernel(*refs, n_cast):
    ins = refs[:N_MIXER_INPUTS]
    cast_in = refs[N_MIXER_INPUTS:N_MIXER_INPUTS + n_cast]
    o_ref = refs[N_MIXER_INPUTS + n_cast]
    cast_out = refs[N_MIXER_INPUTS + n_cast + 1:N_MIXER_INPUTS + 2 * n_cast + 1]
    scratch = refs[N_MIXER_INPUTS + 2 * n_cast + 1:]
    _mixer_step(*ins, o_ref, *scratch)
    _round_blocks(cast_in, cast_out)


def _mixer_step(dchunk_ref, x_ref, mod_ref, prm_ref, rope_ref, win_ref, wbr_ref, wout_ref,
                wpool_ref, ws_ref, bst_ref, dintra_ref, dqk_ref,
                o_ref,
                state_ref, ptail_ref, qb_ref, qd_ref, kb_ref, kd_ref, vb_ref, sg_ref,
                u_ref, vn_ref, y_ref, m_ref):
    rows, d = x_ref.shape
    n_chunks = rows // CHUNK
    qk_w = RET_HEADS * RET_QK_DIM
    v_w = RET_HEADS * RET_V_DIM
    gdim = d // GMLP_GROUPS
    pdim = d // len(POOL_WINDOWS)
    o_q, o_k, o_v = 0, qk_w, 2 * qk_w
    o_g = o_v + v_w
    o_u = o_g + v_w
    o_vs = o_u + d
    o_p = o_vs + d
    o_gate = o_p + d
    seq_step = pl.program_id(1)
    row_slices = [slice(c * CHUNK, (c + 1) * CHUNK) for c in range(n_chunks)]
    qk_slices = [slice(hd * RET_QK_DIM, (hd + 1) * RET_QK_DIM) for hd in range(RET_HEADS)]
    v_slices = [slice(hd * RET_V_DIM, (hd + 1) * RET_V_DIM) for hd in range(RET_HEADS)]

    @pl.when(seq_step == 0)
    def _():
        state_ref[...] = jnp.zeros_like(state_ref)
        ptail_ref[...] = jnp.zeros_like(ptail_ref)

    x = x_ref[...]
    batch_row = pl.ds(pl.program_id(0), 1)
    sh1, sc1, gt1 = mod_ref[0, batch_row, :], mod_ref[1, batch_row, :], mod_ref[2, batch_row, :]
    h = (_rms(x) * (prm_ref[0:1, :] * (1.0 + sc1)) + sh1).astype(BF16)

    def proj(lo, width):
        return _dot(h, win_ref[:, lo:lo + width])

    def gate(n):
        return _sigmoid(proj(o_gate + n * d, d))

    cosv, sinv = rope_ref[:, :RET_QK_DIM], rope_ref[:, RET_QK_DIM:]
    qf, kf = proj(o_q, qk_w), proj(o_k, qk_w)
    k_scale = RET_QK_DIM ** -0.5
    for hd, sl in enumerate(qk_slices):
        q_h, k_h = qf[:, sl], kf[:, sl]
        q_r = q_h * cosv + pltpu.roll(q_h, RET_QK_DIM // 2, 1) * sinv
        k_r = (k_h * cosv + pltpu.roll(k_h, RET_QK_DIM // 2, 1) * sinv) * k_scale
        qb_ref[:, sl] = q_r.astype(BF16)
        qd_ref[:, sl] = (q_r * dqk_ref[0, hd]).astype(BF16)
        kb_ref[:, sl] = k_r.astype(BF16)
        kd_ref[:, sl] = (k_r * dqk_ref[1, hd]).astype(BF16)
    vb_ref[...] = proj(o_v, v_w).astype(BF16)
    g = proj(o_g, v_w)
    half_g = 0.5 * g
    sg_ref[...] = half_g + half_g * jnp.tanh(half_g)
    vs = _gelu_tanh(proj(o_vs, d))
    vn_ref[...] = (_rms(vs) * prm_ref[1:2, :]).astype(BF16)

    scores, kv = {}, {}
    for c, rs in enumerate(row_slices):
        for hd, sl in enumerate(qk_slices):
            s = lax.dot_general(qb_ref[rs, sl], kb_ref[rs, sl], (((1,), (1,)), ((), ())),
                                preferred_element_type=F32)
            scores[c, hd] = (s * dintra_ref[hd]).astype(BF16)
            kv[c, hd] = lax.dot_general(kd_ref[rs, sl], vb_ref[rs, v_slices[hd]], (((0,), (0,)), ((), ())),
                                        preferred_element_type=F32)

    u_ref[...] = _gelu_tanh(proj(o_u, d))

    state = [state_ref[hd] for hd in range(RET_HEADS)]
    ret = {}
    for c, rs in enumerate(row_slices):
        for hd, sl in enumerate(qk_slices):
            lhs = jnp.concatenate([scores[c, hd], qd_ref[rs, sl]], axis=1)
            rhs = jnp.concatenate([vb_ref[rs, v_slices[hd]], state[hd].astype(BF16)], axis=0)
            ret[c, hd] = _dot(lhs, rhs)
            state[hd] = dchunk_ref[hd] * state[hd] + kv[c, hd]
    for hd in range(RET_HEADS):
        state_ref[hd] = state[hd]

    p = proj(o_p, d)
    ext = jnp.concatenate([ptail_ref[...], p], axis=0)
    ptail_ref[...] = p[rows - POOL_HISTORY:, :]
    pos = seq_step * rows + lax.broadcasted_iota(jnp.int32, (rows, pdim), 0)
    pooled = []
    for gi, window in enumerate(POOL_WINDOWS):
        psl = slice(gi * pdim, (gi + 1) * pdim)
        acc = ext[:, psl]
        span = 1
        while span < window:
            acc = acc + pltpu.roll(acc, span, 0)
            span *= 2
        count = jnp.minimum(pos + 1, window).astype(F32)
        pooled.append((acc[POOL_HISTORY:, :] / count - p[:, psl]).astype(BF16))

    for c, rs in enumerate(row_slices):
        for hd, vsl in enumerate(v_slices):
            y_ref[0, rs, vsl] = (sg_ref[rs, vsl] * _rms(ret[c, hd])).astype(BF16)

    tri = lax.broadcasted_iota(jnp.int32, (CHUNK, CHUNK), 0) >= lax.broadcasted_iota(jnp.int32, (CHUNK, CHUNK), 1)
    for gi in range(GMLP_GROUPS):
        gsl = slice(gi * gdim, (gi + 1) * gdim)
        w_tri = jnp.where(tri, ws_ref[gi], 0.0).astype(BF16)
        bias = jnp.broadcast_to(bst_ref[:, gi:gi + 1], (CHUNK, gdim))
        for rs in row_slices:
            mixed = _dot(w_tri, vn_ref[rs, gsl]) + bias
            y_ref[1, rs, gsl] = (u_ref[rs, gsl] * mixed).astype(BF16)

    m_ref[...] = gate(0) * _dot(y_ref[0], wbr_ref[0])

    for gi in range(len(POOL_WINDOWS)):
        psl = slice(gi * pdim, (gi + 1) * pdim)
        mixed = _dot(pooled[gi], wpool_ref[gi]) + prm_ref[2:3, psl]
        y_ref[2, :, psl] = (mixed * prm_ref[3:4, psl]).astype(BF16)

    m_ref[...] += gate(1) * _dot(y_ref[1], wbr_ref[1])
    m_ref[...] += gate(2) * _dot(y_ref[2], wbr_ref[2])
    o_ref[...] = x + gt1 * _dot(m_ref[...].astype(BF16), wout_ref[...])


def _mixer(layer, x, mod, prm, rope_t, w_in, w_branch, w_out, w_pool, ws, bs_t, dintra, dqk, dchunk, cast_jobs):
    b, s, d = x.shape
    rows = MIXER_ROWS
    steps = s // rows
    d_in = w_in.shape[1]
    v_w = RET_HEADS * RET_V_DIM
    qk_w = RET_HEADS * RET_QK_DIM
    row_spec = lambda width: pl.BlockSpec((None, rows, width), lambda i, j: (i, j, 0))
    lres = functools.partial(_layer_resident, layer)
    in_specs = [
        pl.BlockSpec(memory_space=pltpu.SMEM),
        row_spec(d),
        lres(mod.shape[1:]),
        lres(prm.shape[1:]),
        row_spec(2 * RET_QK_DIM),
        _resident((d, d_in)), _resident((N_BRANCH, d, d)), _resident((d, d)),
        _resident(w_pool.shape), lres(ws.shape[1:]), lres(bs_t.shape[1:]),
        _resident(dintra.shape), _resident(dqk.shape),
    ]
    assert len(in_specs) == N_MIXER_INPUTS
    cast_in, cast_out, cast_shapes, cast_bytes = _round_block_specs(cast_jobs, b * steps, lambda i, j: i * steps + j)
    in_specs += cast_in
    out_specs = [row_spec(d)] + cast_out
    out_shape = [jax.ShapeDtypeStruct((b, s, d), F32)] + cast_shapes
    scratch = [
        pltpu.VMEM((RET_HEADS, RET_QK_DIM, RET_V_DIM), F32),
        pltpu.VMEM((POOL_HISTORY, d), F32),
        pltpu.VMEM((rows, qk_w), BF16), pltpu.VMEM((rows, qk_w), BF16),
        pltpu.VMEM((rows, qk_w), BF16), pltpu.VMEM((rows, qk_w), BF16),
        pltpu.VMEM((rows, v_w), BF16),
        pltpu.VMEM((rows, v_w), F32),
        pltpu.VMEM((rows, d), F32),
        pltpu.VMEM((rows, d), BF16),
        pltpu.VMEM((N_BRANCH, rows, d), BF16),
        pltpu.VMEM((rows, d), F32),
    ]
    resident = 2 * (d * d_in + N_BRANCH * d * d + d * d + w_pool.size) + 4 * (
        ws[0].size + bs_t[0].size + dintra.size + dqk.size + mod[0].size + prm[0].size)
    streamed = 4 * rows * (2 * d + 2 * RET_QK_DIM) + cast_bytes
    scratch_bytes = 4 * (RET_HEADS * RET_QK_DIM * RET_V_DIM + POOL_HISTORY * d) + rows * (
        2 * 4 * qk_w + 2 * v_w + 4 * v_w + 4 * d + 2 * d + 2 * N_BRANCH * d + 4 * d)
    temps = 8 * 4 * rows * d
    return pl.pallas_call(
        functools.partial(_mixer_kernel, n_cast=len(cast_jobs)),
        grid=(b, steps),
        in_specs=in_specs,
        out_specs=out_specs,
        out_shape=out_shape,
        scratch_shapes=scratch,
        compiler_params=pltpu.CompilerParams(
            dimension_semantics=("arbitrary", "arbitrary"),
            vmem_limit_bytes=_vmem_limit(resident, streamed, scratch_bytes, temps)),
        name="token_mixer",
    )(dchunk, x, mod, prm, rope_t, w_in, w_branch, w_out, w_pool, ws, bs_t, dintra, dqk,
      *[w for w, _ in cast_jobs])


def _ffn_kernel(x_ref, mod_ref, n2_ref, w1_ref, w2_ref, fn_ref, o_ref, hid_ref, *, final):
    x = x_ref[...]
    batch_row = pl.ds(pl.program_id(0), 1)
    sh2, sc2, gt2 = mod_ref[3, batch_row, :], mod_ref[4, batch_row, :], mod_ref[5, batch_row, :]
    h = (_rms(x) * (n2_ref[...] * (1.0 + sc2)) + sh2).astype(BF16)
    d_ff = w1_ref.shape[1]
    for lo in range(0, d_ff, FF_BLOCK):
        hid = _dot(h, w1_ref[:, lo:lo + FF_BLOCK])
        hid_ref[:, lo:lo + FF_BLOCK] = jnp.square(jnp.maximum(hid, 0.0)).astype(BF16)
    y = x + gt2 * _dot(hid_ref[...], w2_ref[...])
    if final:
        y = _rms(y) * fn_ref[...]
    o_ref[...] = y


def _ffn(layer, x, mod, norm2, w1, w2, final_norm, final):
    b, s, d = x.shape
    rows = FFN_ROWS
    d_ff = w1.shape[1]
    row_spec = pl.BlockSpec((None, rows, d), lambda i, j: (i, j, 0))
    lres = functools.partial(_layer_resident, layer)
    resident = 2 * 2 * d * d_ff + 4 * 2 * d + 4 * mod[0].size
    streamed = 4 * 2 * rows * d
    temps = rows * (2 * d + 6 * FF_BLOCK + 3 * 4 * d)
    return pl.pallas_call(
        functools.partial(_ffn_kernel, final=final),
        grid=(b, s // rows),
        in_specs=[
            row_spec,
            lres(mod.shape[1:]),
            lres((1, d)), _resident((d, d_ff)), _resident((d_ff, d)), _resident((1, d)),
        ],
        out_specs=row_spec,
        out_shape=jax.ShapeDtypeStruct((b, s, d), F32),
        scratch_shapes=[pltpu.VMEM((rows, d_ff), BF16)],
        compiler_params=pltpu.CompilerParams(
            dimension_semantics=("arbitrary", "arbitrary"),
            vmem_limit_bytes=_vmem_limit(resident, streamed, 2 * rows * d_ff, temps)),
        name="channel_mlp",
    )(x, mod, norm2, w1, w2, final_norm)


def _retention_decay_tables(rows):
    log_gamma = jnp.log1p(-jnp.power(2.0, -5.0 - jnp.arange(RET_HEADS, dtype=F32)))
    pos = jnp.arange(CHUNK, dtype=F32)
    rel = pos[:, None] - pos[None, :]
    causal = rel >= 0
    intra = jnp.where(causal[None], jnp.exp(log_gamma[:, None, None] * jnp.where(causal, rel, 0.0)[None]), 0.0)
    decay_q = jnp.exp(log_gamma[:, None] * (pos + 1.0)[None])
    decay_k = jnp.exp(log_gamma[:, None] * (CHUNK - 1.0 - pos)[None])
    decay_chunk = jnp.exp(log_gamma * CHUNK)
    reps = rows // CHUNK
    widen = lambda t: jnp.broadcast_to(jnp.tile(t, (1, reps))[:, :, None], (RET_HEADS, rows, RET_QK_DIM))
    return intra, jnp.stack([widen(decay_q), widen(decay_k)]), decay_chunk


def kernel(x, c, positions, w_ada, b_ada, norm1, norm2, w_in, ws_gmlp, bs_gmlp, vnorm_gmlp, w_pool, b_pool,
           pool_scale, w_branch, w_out, w_ff1, w_ff2, final_norm):
    depth = w_in.shape[0]
    b, s, d = x.shape
    assert s % MIXER_ROWS == 0 and s % FFN_ROWS == 0 and s % ROPE_ROWS == 0 and MIXER_ROWS % CHUNK == 0
    assert w_ff1.shape[2] % FF_BLOCK == 0

    mod = _ada_rows(c, w_ada, b_ada)
    dintra, dqk, dchunk = _retention_decay_tables(MIXER_ROWS)
    rows_of = lambda v: v.reshape(depth, 1, d)
    prm = jnp.concatenate([rows_of(v) for v in (norm1, vnorm_gmlp, b_pool, pool_scale)], axis=1)
    norm2 = rows_of(norm2)
    bs_t = jnp.swapaxes(bs_gmlp, 1, 2)

    mixer_w = (w_in, w_branch, w_out, w_pool)
    as_rows = lambda w: w.reshape(depth, -1, w.shape[-1])
    rope_t, *next_mixer = _rope_tables(positions, [(as_rows(w), 0) for w in mixer_w])
    for l in range(depth):
        mixer_bf16 = [w.reshape(ref.shape[1:]) for w, ref in zip(next_mixer, mixer_w)]
        jobs = [(as_rows(w_ff1), l), (as_rows(w_ff2), l)]
        if l + 1 < depth:
            jobs += [(as_rows(w), l + 1) for w in mixer_w]
        x, ff1, ff2, *next_mixer = _mixer(l, x, mod, prm, rope_t, *mixer_bf16, ws_gmlp, bs_t,
                                          dintra, dqk, dchunk, jobs)
        x = _ffn(l, x, mod, norm2, ff1, ff2, final_norm.reshape(1, d), final=(l == depth - 1))
    return x
```

```python
import functools

import jax
import jax.numpy as jnp
from jax import lax
from jax.experimental import pallas as pl
from jax.experimental.pallas import tpu as pltpu

CHUNK = 128
RET_HEADS = 4
RET_QK_DIM = 128
RET_V_DIM = 256
GMLP_GROUPS = 4
POOL_WINDOWS = (2, 4, 8, 16)
POOL_HISTORY = 16
N_BRANCH = 3
N_MOD = 6
ROPE_BASE = 10000.0
EPS = 1e-6

VMEM_BYTES_V7X = 64 * 1024 * 1024
VMEM_REQUEST_CAP_V7X = VMEM_BYTES_V7X - 4 * 1024 * 1024
BF16_ROW_TILE = 16

MIXER_ROWS = 256
FFN_ROWS = 1024
FF_BLOCK = 1024
ROPE_ROWS = 2048

F32 = jnp.float32
BF16 = jnp.bfloat16


def _dot(a, b):
    return jnp.dot(a, b, preferred_element_type=F32)


def _rms(x):
    return x * lax.rsqrt(jnp.mean(x * x, axis=-1, keepdims=True) + EPS)


def _resident(shape):
    zeros = (0,) * len(shape)
    return pl.BlockSpec(shape, lambda *_: zeros, pipeline_mode=pl.Buffered(1))


def _layer_resident(layer, shape):
    index = (layer,) + (0,) * len(shape)
    return pl.BlockSpec((None,) + tuple(shape), lambda *_: index, pipeline_mode=pl.Buffered(1))


def _vmem_limit(resident_bytes, streamed_bytes, scratch_bytes, temp_bytes):
    need = resident_bytes + 2 * streamed_bytes + scratch_bytes + temp_bytes
    return min(int(need), VMEM_REQUEST_CAP_V7X)


def _ada_kernel(c_ref, w_ref, b_ref, o_ref):
    c = c_ref[...]
    c_act = (c * jax.nn.sigmoid(c)).astype(BF16)
    o_ref[...] = _dot(c_act, w_ref[...].astype(BF16)) + b_ref[...]


def _ada_rows(c, w_ada, b_ada):
    depth, d, n = w_ada.shape
    b = c.shape[0]
    assert n == N_MOD * d
    return pl.pallas_call(
        _ada_kernel,
        grid=(depth, N_MOD),
        in_specs=[
            pl.BlockSpec((b, d), lambda l, j: (0, 0)),
            pl.BlockSpec((None, d, d), lambda l, j: (l, 0, j)),
            pl.BlockSpec((None, 1, d), lambda l, j: (l, 0, j)),
        ],
        out_specs=pl.BlockSpec((None, None, b, d), lambda l, j: (l, j, 0, 0)),
        out_shape=jax.ShapeDtypeStruct((depth, N_MOD, b, d), F32),
        compiler_params=pltpu.CompilerParams(
            dimension_semantics=("arbitrary", "arbitrary")),
        name="ada_rows",
    )(c, w_ada, b_ada.reshape(depth, 1, n))


PACKED = jnp.int32
ROWS_PER_WORD = 2


def _pack_rows(w_bf16):
    return pltpu.bitcast(w_bf16, PACKED)


def _unpack_rows(w_packed):
    return pltpu.bitcast(w_packed, BF16)


def _round_blocks(cast_in, cast_out):
    for src_ref, dst_ref in zip(cast_in, cast_out):
        dst_ref[...] = _pack_rows(src_ref[...].astype(BF16))


def _round_block_specs(cast_jobs, n_steps, flat_step):
    in_specs, out_specs, out_shapes, nbytes = [], [], [], 0
    for w, layer in cast_jobs:
        _, r, c = w.shape
        blk = r // n_steps
        assert blk * n_steps == r and blk % BF16_ROW_TILE == 0
        in_specs.append(pl.BlockSpec((None, blk, c), lambda i, j, layer=layer: (layer, flat_step(i, j), 0)))
        out_specs.append(pl.BlockSpec((blk // ROWS_PER_WORD, c), lambda i, j: (flat_step(i, j), 0)))
        out_shapes.append(jax.ShapeDtypeStruct((r // ROWS_PER_WORD, c), PACKED))
        nbytes += blk * c * (4 + 2)
    return in_specs, out_specs, out_shapes, nbytes


N_ROPE_INPUTS = 3


def _rope_kernel(*refs, n_cast):
    pos_ref, freq_ref, sign_ref = refs[:N_ROPE_INPUTS]
    cast_in = refs[N_ROPE_INPUTS:N_ROPE_INPUTS + n_cast]
    tab_ref = refs[N_ROPE_INPUTS + n_cast]
    cast_out = refs[N_ROPE_INPUTS + n_cast + 1:]
    _round_blocks(cast_in, cast_out)

    rows = pos_ref.shape[0]
    half_rows, half = rows // 2, RET_QK_DIM // 2
    pos = pos_ref[...].astype(F32)
    low = lax.broadcasted_iota(jnp.int32, (half_rows, RET_QK_DIM), 1) < half
    ang = jnp.where(low, pos[:half_rows], pos[half_rows:]) * freq_ref[...]
    cos_p, sin_p = jnp.cos(ang), jnp.sin(ang)
    cos_s, sin_s = pltpu.roll(cos_p, half, 1), pltpu.roll(sin_p, half, 1)
    sign = sign_ref[...]
    tab_ref[:half_rows, :RET_QK_DIM] = jnp.where(low, cos_p, cos_s)
    tab_ref[half_rows:, :RET_QK_DIM] = jnp.where(low, cos_s, cos_p)
    tab_ref[:half_rows, RET_QK_DIM:] = jnp.where(low, sin_p, sin_s) * sign
    tab_ref[half_rows:, RET_QK_DIM:] = jnp.where(low, sin_s, sin_p) * sign


def _rope_tables(positions, cast_jobs):
    b, s = positions.shape
    half = RET_QK_DIM // 2
    inv_freq = ROPE_BASE ** (-jnp.arange(half, dtype=F32) / half)
    freq = jnp.concatenate([inv_freq, inv_freq]).reshape(1, RET_QK_DIM)
    sign = jnp.concatenate([-jnp.ones((half,), F32), jnp.ones((half,), F32)]).reshape(1, RET_QK_DIM)
    rows = ROPE_ROWS
    steps = s // rows
    cast_in, cast_out, cast_shapes, cast_bytes = _round_block_specs(cast_jobs, b * steps, lambda i, j: i * steps + j)
    table = jax.ShapeDtypeStruct((b, s, 2 * RET_QK_DIM), F32)
    table_spec = pl.BlockSpec((None, rows, 2 * RET_QK_DIM), lambda i, j: (i, j, 0))
    return pl.pallas_call(
        functools.partial(_rope_kernel, n_cast=len(cast_jobs)),
        grid=(b, steps),
        in_specs=[
            pl.BlockSpec((None, rows, 1), lambda i, j: (i, j, 0)),
            pl.BlockSpec((1, RET_QK_DIM), lambda i, j: (0, 0)),
            pl.BlockSpec((1, RET_QK_DIM), lambda i, j: (0, 0)),
        ] + cast_in,
        out_specs=[table_spec] + cast_out,
        out_shape=[table] + cast_shapes,
        compiler_params=pltpu.CompilerParams(
            dimension_semantics=("arbitrary", "arbitrary"),
            vmem_limit_bytes=_vmem_limit(0, cast_bytes + 4 * rows * 3 * RET_QK_DIM, 0, 16 * 4 * rows * RET_QK_DIM)),
        name="rope_tables",
    )(positions.reshape(b, s, 1), freq, sign, *[w for w, _ in cast_jobs])


GELU_C0 = 0.7978845608028654
GELU_C1 = GELU_C0 * 0.044715


def _sigmoid(x):
    return 0.5 * jnp.tanh(0.5 * x) + 0.5


def _gelu_tanh(x):
    half_x = 0.5 * x
    return half_x + half_x * jnp.tanh(x * (GELU_C0 + GELU_C1 * (x * x)))


N_MIXER_INPUTS = 13


def _mixer_kernel(*refs, n_cast):
    ins = refs[:N_MIXER_INPUTS]
    cast_in = refs[N_MIXER_INPUTS:N_MIXER_INPUTS + n_cast]
    o_ref = refs[N_MIXER_INPUTS + n_cast]
    cast_out = refs[N_MIXER_INPUTS + n_cast + 1:N_MIXER_INPUTS + 2 * n_cast + 1]
    scratch = refs[N_MIXER_INPUTS + 2 * n_cast + 1:]
    _mixer_step(*ins, o_ref, *scratch)
    _round_blocks(cast_in, cast_out)


def _mixer_step(dchunk_ref, x_ref, mod_ref, prm_ref, rope_ref, win_ref, wbr_ref, wout_ref,
                wpool_ref, ws_ref, bst_ref, dintra_ref, dqk_ref,
                o_ref,
                state_ref, ptail_ref, qb_ref, qd_ref, kb_ref, kd_ref, vb_ref, sg_ref,
                u_ref, vn_ref, y_ref, m_ref):
    rows, d = x_ref.shape
    n_chunks = rows // CHUNK
    qk_w = RET_HEADS * RET_QK_DIM
    v_w = RET_HEADS * RET_V_DIM
    gdim = d // GMLP_GROUPS
    pdim = d // len(POOL_WINDOWS)
    o_q, o_k, o_v = 0, qk_w, 2 * qk_w
    o_g = o_v + v_w
    o_u = o_g + v_w
    o_vs = o_u + d
    o_p = o_vs + d
    o_gate = o_p + d
    seq_step = pl.program_id(1)
    row_slices = [slice(c * CHUNK, (c + 1) * CHUNK) for c in range(n_chunks)]
    qk_slices = [slice(hd * RET_QK_DIM, (hd + 1) * RET_QK_DIM) for hd in range(RET_HEADS)]
    v_slices = [slice(hd * RET_V_DIM, (hd + 1) * RET_V_DIM) for hd in range(RET_HEADS)]

    @pl.when(seq_step == 0)
    def _():
        state_ref[...] = jnp.zeros_like(state_ref)
        ptail_ref[...] = jnp.zeros_like(ptail_ref)

    x = x_ref[...]
    batch_row = pl.ds(pl.program_id(0), 1)
    sh1, sc1, gt1 = mod_ref[0, batch_row, :], mod_ref[1, batch_row, :], mod_ref[2, batch_row, :]
    h = (_rms(x) * (prm_ref[0:1, :] * (1.0 + sc1)) + sh1).astype(BF16)

    def proj(lo, width):
        return _dot(h, _unpack_rows(win_ref[:, lo:lo + width]))

    def gate(n):
        return _sigmoid(proj(o_gate + n * d, d))

    cosv, sinv = rope_ref[:, :RET_QK_DIM], rope_ref[:, RET_QK_DIM:]
    qf, kf = proj(o_q, qk_w), proj(o_k, qk_w)
    k_scale = RET_QK_DIM ** -0.5
    for hd, sl in enumerate(qk_slices):
        q_h, k_h = qf[:, sl], kf[:, sl]
        q_r = q_h * cosv + pltpu.roll(q_h, RET_QK_DIM // 2, 1) * sinv
        k_r = (k_h * cosv + pltpu.roll(k_h, RET_QK_DIM // 2, 1) * sinv) * k_scale
        qb_ref[:, sl] = q_r.astype(BF16)
        qd_ref[:, sl] = (q_r * dqk_ref[0, hd]).astype(BF16)
        kb_ref[:, sl] = k_r.astype(BF16)
        kd_ref[:, sl] = (k_r * dqk_ref[1, hd]).astype(BF16)
    vb_ref[...] = proj(o_v, v_w).astype(BF16)
    g = proj(o_g, v_w)
    half_g = 0.5 * g
    sg_ref[...] = half_g + half_g * jnp.tanh(half_g)
    vs = _gelu_tanh(proj(o_vs, d))
    vn_ref[...] = (_rms(vs) * prm_ref[1:2, :]).astype(BF16)

    scores, kv = {}, {}
    for c, rs in enumerate(row_slices):
        for hd, sl in enumerate(qk_slices):
            s = lax.dot_general(qb_ref[rs, sl], kb_ref[rs, sl], (((1,), (1,)), ((), ())),
                                preferred_element_type=F32)
            scores[c, hd] = (s * dintra_ref[hd]).astype(BF16)
            kv[c, hd] = lax.dot_general(kd_ref[rs, sl], vb_ref[rs, v_slices[hd]], (((0,), (0,)), ((), ())),
                                        preferred_element_type=F32)

    u_ref[...] = _gelu_tanh(proj(o_u, d))

    state = [state_ref[hd] for hd in range(RET_HEADS)]
    ret = {}
    for c, rs in enumerate(row_slices):
        for hd, sl in enumerate(qk_slices):
            lhs = jnp.concatenate([scores[c, hd], qd_ref[rs, sl]], axis=1)
            rhs = jnp.concatenate([vb_ref[rs, v_slices[hd]], state[hd].astype(BF16)], axis=0)
            ret[c, hd] = _dot(lhs, rhs)
            state[hd] = dchunk_ref[hd] * state[hd] + kv[c, hd]
    for hd in range(RET_HEADS):
        state_ref[hd] = state[hd]

    p = proj(o_p, d)
    ext = jnp.concatenate([ptail_ref[...], p], axis=0)
    ptail_ref[...] = p[rows - POOL_HISTORY:, :]
    pos = seq_step * rows + lax.broadcasted_iota(jnp.int32, (rows, pdim), 0)
    pooled = []
    for gi, window in enumerate(POOL_WINDOWS):
        psl = slice(gi * pdim, (gi + 1) * pdim)
        acc = ext[:, psl]
        span = 1
        while span < window:
            acc = acc + pltpu.roll(acc, span, 0)
            span *= 2
        count = jnp.minimum(pos + 1, window).astype(F32)
        pooled.append((acc[POOL_HISTORY:, :] / count - p[:, psl]).astype(BF16))

    for c, rs in enumerate(row_slices):
        for hd, vsl in enumerate(v_slices):
            y_ref[0, rs, vsl] = (sg_ref[rs, vsl] * _rms(ret[c, hd])).astype(BF16)

    tri = lax.broadcasted_iota(jnp.int32, (CHUNK, CHUNK), 0) >= lax.broadcasted_iota(jnp.int32, (CHUNK, CHUNK), 1)
    for gi in range(GMLP_GROUPS):
        gsl = slice(gi * gdim, (gi + 1) * gdim)
        w_tri = jnp.where(tri, ws_ref[gi], 0.0).astype(BF16)
        bias = jnp.broadcast_to(bst_ref[:, gi:gi + 1], (CHUNK, gdim))
        for rs in row_slices:
            mixed = _dot(w_tri, vn_ref[rs, gsl]) + bias
            y_ref[1, rs, gsl] = (u_ref[rs, gsl] * mixed).astype(BF16)

    m_ref[...] = gate(0) * _dot(y_ref[0], _unpack_rows(wbr_ref[0]))

    for gi in range(len(POOL_WINDOWS)):
        psl = slice(gi * pdim, (gi + 1) * pdim)
        mixed = _dot(pooled[gi], _unpack_rows(wpool_ref[gi])) + prm_ref[2:3, psl]
        y_ref[2, :, psl] = (mixed * prm_ref[3:4, psl]).astype(BF16)

    m_ref[...] += gate(1) * _dot(y_ref[1], _unpack_rows(wbr_ref[1]))
    m_ref[...] += gate(2) * _dot(y_ref[2], _unpack_rows(wbr_ref[2]))
    o_ref[...] = x + gt1 * _dot(m_ref[...].astype(BF16), _unpack_rows(wout_ref[...]))


def _mixer(layer, x, mod, prm, rope_t, w_in, w_branch, w_out, w_pool, ws, bs_t, dintra, dqk, dchunk, cast_jobs):
    b, s, d = x.shape
    rows = MIXER_ROWS
    steps = s // rows
    d_in = w_in.shape[1]
    v_w = RET_HEADS * RET_V_DIM
    qk_w = RET_HEADS * RET_QK_DIM
    row_spec = lambda width: pl.BlockSpec((None, rows, width), lambda i, j: (i, j, 0))
    lres = functools.partial(_layer_resident, layer)
    in_specs = [
        pl.BlockSpec(memory_space=pltpu.SMEM),
        row_spec(d),
        lres(mod.shape[1:]),
        lres(prm.shape[1:]),
        row_spec(2 * RET_QK_DIM),
        _resident(w_in.shape), _resident(w_branch.shape), _resident(w_out.shape),
        _resident(w_pool.shape), lres(ws.shape[1:]), lres(bs_t.shape[1:]),
        _resident(dintra.shape), _resident(dqk.shape),
    ]
    assert len(in_specs) == N_MIXER_INPUTS
    cast_in, cast_out, cast_shapes, cast_bytes = _round_block_specs(cast_jobs, b * steps, lambda i, j: i * steps + j)
    in_specs += cast_in
    out_specs = [row_spec(d)] + cast_out
    out_shape = [jax.ShapeDtypeStruct((b, s, d), F32)] + cast_shapes
    scratch = [
        pltpu.VMEM((RET_HEADS, RET_QK_DIM, RET_V_DIM), F32),
        pltpu.VMEM((POOL_HISTORY, d), F32),
        pltpu.VMEM((rows, qk_w), BF16), pltpu.VMEM((rows, qk_w), BF16),
        pltpu.VMEM((rows, qk_w), BF16), pltpu.VMEM((rows, qk_w), BF16),
        pltpu.VMEM((rows, v_w), BF16),
        pltpu.VMEM((rows, v_w), F32),
        pltpu.VMEM((rows, d), F32),
        pltpu.VMEM((rows, d), BF16),
        pltpu.VMEM((N_BRANCH, rows, d), BF16),
        pltpu.VMEM((rows, d), F32),
    ]
    resident = 4 * (w_in.size + w_branch.size + w_out.size + w_pool.size) + 4 * (
        ws[0].size + bs_t[0].size + dintra.size + dqk.size + mod[0].size + prm[0].size)
    streamed = 4 * rows * (2 * d + 2 * RET_QK_DIM) + cast_bytes
    scratch_bytes = 4 * (RET_HEADS * RET_QK_DIM * RET_V_DIM + POOL_HISTORY * d) + rows * (
        2 * 4 * qk_w + 2 * v_w + 4 * v_w + 4 * d + 2 * d + 2 * N_BRANCH * d + 4 * d)
    temps = 8 * 4 * rows * d
    return pl.pallas_call(
        functools.partial(_mixer_kernel, n_cast=len(cast_jobs)),
        grid=(b, steps),
        in_specs=in_specs,
        out_specs=out_specs,
        out_shape=out_shape,
        scratch_shapes=scratch,
        compiler_params=pltpu.CompilerParams(
            dimension_semantics=("arbitrary", "arbitrary"),
            vmem_limit_bytes=_vmem_limit(resident, streamed, scratch_bytes, temps)),
        name="token_mixer",
    )(dchunk, x, mod, prm, rope_t, w_in, w_branch, w_out, w_pool, ws, bs_t, dintra, dqk,
      *[w for w, _ in cast_jobs])


def _ffn_kernel(x_ref, mod_ref, n2_ref, w1_ref, w2_ref, fn_ref, o_ref, hid_ref, *, final):
    x = x_ref[...]
    batch_row = pl.ds(pl.program_id(0), 1)
    sh2, sc2, gt2 = mod_ref[3, batch_row, :], mod_ref[4, batch_row, :], mod_ref[5, batch_row, :]
    h = (_rms(x) * (n2_ref[...] * (1.0 + sc2)) + sh2).astype(BF16)
    d_ff = w1_ref.shape[1]
    for lo in range(0, d_ff, FF_BLOCK):
        hid = _dot(h, _unpack_rows(w1_ref[:, lo:lo + FF_BLOCK]))
        hid_ref[:, lo:lo + FF_BLOCK] = jnp.square(jnp.maximum(hid, 0.0)).astype(BF16)
    y = x + gt2 * _dot(hid_ref[...], _unpack_rows(w2_ref[...]))
    if final:
        y = _rms(y) * fn_ref[...]
    o_ref[...] = y


def _ffn(layer, x, mod, norm2, w1, w2, final_norm, final):
    b, s, d = x.shape
    rows = FFN_ROWS
    d_ff = w1.shape[1]
    row_spec = pl.BlockSpec((None, rows, d), lambda i, j: (i, j, 0))
    lres = functools.partial(_layer_resident, layer)
    resident = 2 * 2 * d * d_ff + 4 * 2 * d + 4 * mod[0].size
    streamed = 4 * 2 * rows * d
    temps = rows * (2 * d + 6 * FF_BLOCK + 3 * 4 * d)
    return pl.pallas_call(
        functools.partial(_ffn_kernel, final=final),
        grid=(b, s // rows),
        in_specs=[
            row_spec,
            lres(mod.shape[1:]),
            lres((1, d)), _resident(w1.shape), _resident(w2.shape), _resident((1, d)),
        ],
        out_specs=row_spec,
        out_shape=jax.ShapeDtypeStruct((b, s, d), F32),
        scratch_shapes=[pltpu.VMEM((rows, d_ff), BF16)],
        compiler_params=pltpu.CompilerParams(
            dimension_semantics=("arbitrary", "arbitrary"),
            vmem_limit_bytes=_vmem_limit(resident, streamed, 2 * rows * d_ff, temps)),
        name="channel_mlp",
    )(x, mod, norm2, w1, w2, final_norm)


def _retention_decay_tables(rows):
    log_gamma = jnp.log1p(-jnp.power(2.0, -5.0 - jnp.arange(RET_HEADS, dtype=F32)))
    pos = jnp.arange(CHUNK, dtype=F32)
    rel = pos[:, None] - pos[None, :]
    causal = rel >= 0
    intra = jnp.where(causal[None], jnp.exp(log_gamma[:, None, None] * jnp.where(causal, rel, 0.0)[None]), 0.0)
    decay_q = jnp.exp(log_gamma[:, None] * (pos + 1.0)[None])
    decay_k = jnp.exp(log_gamma[:, None] * (CHUNK - 1.0 - pos)[None])
    decay_chunk = jnp.exp(log_gamma * CHUNK)
    reps = rows // CHUNK
    widen = lambda t: jnp.broadcast_to(jnp.tile(t, (1, reps))[:, :, None], (RET_HEADS, rows, RET_QK_DIM))
    return intra, jnp.stack([widen(decay_q), widen(decay_k)]), decay_chunk


def kernel(x, c, positions, w_ada, b_ada, norm1, norm2, w_in, ws_gmlp, bs_gmlp, vnorm_gmlp, w_pool, b_pool,
           pool_scale, w_branch, w_out, w_ff1, w_ff2, final_norm):
    depth = w_in.shape[0]
    b, s, d = x.shape
    assert s % MIXER_ROWS == 0 and s % FFN_ROWS == 0 and s % ROPE_ROWS == 0 and MIXER_ROWS % CHUNK == 0
    assert w_ff1.shape[2] % FF_BLOCK == 0

    mod = _ada_rows(c, w_ada, b_ada)
    dintra, dqk, dchunk = _retention_decay_tables(MIXER_ROWS)
    rows_of = lambda v: v.reshape(depth, 1, d)
    prm = jnp.concatenate([rows_of(v) for v in (norm1, vnorm_gmlp, b_pool, pool_scale)], axis=1)
    norm2 = rows_of(norm2)
    bs_t = jnp.swapaxes(bs_gmlp, 1, 2)

    mixer_w = (w_in, w_branch, w_out, w_pool)
    as_rows = lambda w: w.reshape(depth, -1, w.shape[-1])
    rope_t, *next_mixer = _rope_tables(positions, [(as_rows(w), 0) for w in mixer_w])
    for l in range(depth):
        mixer_bf16 = [w.reshape(ref.shape[1:-2] + (ref.shape[-2] // ROWS_PER_WORD, ref.shape[-1]))
                      for w, ref in zip(next_mixer, mixer_w)]
        jobs = [(as_rows(w_ff1), l), (as_rows(w_ff2), l)]
        if l + 1 < depth:
            jobs += [(as_rows(w), l + 1) for w in mixer_w]
        x, ff1, ff2, *next_mixer = _mixer(l, x, mod, prm, rope_t, *mixer_bf16, ws_gmlp, bs_t,
                                          dintra, dqk, dchunk, jobs)
        x = _ffn(l, x, mod, norm2, ff1, ff2, final_norm.reshape(1, d), final=(l == depth - 1))
    return x
```

```python
import functools

import jax
import jax.numpy as jnp
from jax import lax
from jax.experimental import pallas as pl
from jax.experimental.pallas import tpu as pltpu

CHUNK = 128
RET_HEADS = 4
RET_QK_DIM = 128
RET_V_DIM = 256
GMLP_GROUPS = 4
POOL_WINDOWS = (2, 4, 8, 16)
POOL_HISTORY = 16
N_BRANCH = 3
N_MOD = 6
ROPE_BASE = 10000.0
EPS = 1e-6

VMEM_BYTES_V7X = 64 * 1024 * 1024
VMEM_REQUEST_CAP_V7X = VMEM_BYTES_V7X - 4 * 1024 * 1024
BF16_ROW_TILE = 16

MIXER_ROWS = 256
FFN_ROWS = 1024
FF_BLOCK = 1024
GATE_BLOCK = 256
ROPE_ROWS = 2048

F32 = jnp.float32
BF16 = jnp.bfloat16


def _dot(a, b):
    return jnp.dot(a, b, preferred_element_type=F32)


def _rms(x):
    return x * lax.rsqrt(jnp.mean(x * x, axis=-1, keepdims=True) + EPS)


def _resident(shape):
    zeros = (0,) * len(shape)
    return pl.BlockSpec(shape, lambda *_: zeros, pipeline_mode=pl.Buffered(1))


def _layer_resident(layer, shape):
    index = (layer,) + (0,) * len(shape)
    return pl.BlockSpec((None,) + tuple(shape), lambda *_: index, pipeline_mode=pl.Buffered(1))


def _vmem_limit(resident_bytes, streamed_bytes, scratch_bytes, temp_bytes):
    need = resident_bytes + 2 * streamed_bytes + scratch_bytes + temp_bytes
    return min(int(need), VMEM_REQUEST_CAP_V7X)


def _ada_kernel(c_ref, w_ref, b_ref, o_ref):
    c = c_ref[...]
    c_act = (c * jax.nn.sigmoid(c)).astype(BF16)
    o_ref[...] = _dot(c_act, w_ref[...].astype(BF16)) + b_ref[...]


def _ada_rows(c, w_ada, b_ada):
    depth, d, n = w_ada.shape
    b = c.shape[0]
    assert n == N_MOD * d
    return pl.pallas_call(
        _ada_kernel,
        grid=(depth, N_MOD),
        in_specs=[
            pl.BlockSpec((b, d), lambda l, j: (0, 0)),
            pl.BlockSpec((None, d, d), lambda l, j: (l, 0, j)),
            pl.BlockSpec((None, 1, d), lambda l, j: (l, 0, j)),
        ],
        out_specs=pl.BlockSpec((None, None, b, d), lambda l, j: (l, j, 0, 0)),
        out_shape=jax.ShapeDtypeStruct((depth, N_MOD, b, d), F32),
        compiler_params=pltpu.CompilerParams(
            dimension_semantics=("arbitrary", "arbitrary")),
        name="ada_rows",
    )(c, w_ada, b_ada.reshape(depth, 1, n))


PACKED = jnp.int32
ROWS_PER_WORD = 2


def _pack_rows(w_bf16):
    return pltpu.bitcast(w_bf16, PACKED)


def _unpack_rows(w_packed):
    return pltpu.bitcast(w_packed, BF16)


def _round_blocks(cast_in, cast_out):
    for src_ref, dst_ref in zip(cast_in, cast_out):
        dst_ref[...] = _pack_rows(src_ref[...].astype(BF16))


def _round_block_specs(cast_jobs, n_steps, flat_step):
    in_specs, out_specs, out_shapes, nbytes = [], [], [], 0
    for w, layer in cast_jobs:
        _, r, c = w.shape
        blk = r // n_steps
        assert blk * n_steps == r and blk % BF16_ROW_TILE == 0
        in_specs.append(pl.BlockSpec((None, blk, c), lambda i, j, layer=layer: (layer, flat_step(i, j), 0)))
        out_specs.append(pl.BlockSpec((blk // ROWS_PER_WORD, c), lambda i, j: (flat_step(i, j), 0)))
        out_shapes.append(jax.ShapeDtypeStruct((r // ROWS_PER_WORD, c), PACKED))
        nbytes += blk * c * (4 + 2)
    return in_specs, out_specs, out_shapes, nbytes


N_ROPE_INPUTS = 3


def _rope_kernel(*refs, n_cast):
    pos_ref, freq_ref, sign_ref = refs[:N_ROPE_INPUTS]
    cast_in = refs[N_ROPE_INPUTS:N_ROPE_INPUTS + n_cast]
    tab_ref = refs[N_ROPE_INPUTS + n_cast]
    cast_out = refs[N_ROPE_INPUTS + n_cast + 1:]
    _round_blocks(cast_in, cast_out)

    rows = pos_ref.shape[0]
    half_rows, half = rows // 2, RET_QK_DIM // 2
    pos = pos_ref[...].astype(F32)
    low = lax.broadcasted_iota(jnp.int32, (half_rows, RET_QK_DIM), 1) < half
    ang = jnp.where(low, pos[:half_rows], pos[half_rows:]) * freq_ref[...]
    cos_p, sin_p = jnp.cos(ang), jnp.sin(ang)
    cos_s, sin_s = pltpu.roll(cos_p, half, 1), pltpu.roll(sin_p, half, 1)
    sign = sign_ref[...]
    tab_ref[:half_rows, :RET_QK_DIM] = jnp.where(low, cos_p, cos_s)
    tab_ref[half_rows:, :RET_QK_DIM] = jnp.where(low, cos_s, cos_p)
    tab_ref[:half_rows, RET_QK_DIM:] = jnp.where(low, sin_p, sin_s) * sign
    tab_ref[half_rows:, RET_QK_DIM:] = jnp.where(low, sin_s, sin_p) * sign


def _rope_tables(positions, cast_jobs):
    b, s = positions.shape
    half = RET_QK_DIM // 2
    inv_freq = ROPE_BASE ** (-jnp.arange(half, dtype=F32) / half)
    freq = jnp.concatenate([inv_freq, inv_freq]).reshape(1, RET_QK_DIM)
    sign = jnp.concatenate([-jnp.ones((half,), F32), jnp.ones((half,), F32)]).reshape(1, RET_QK_DIM)
    rows = ROPE_ROWS
    steps = s // rows
    cast_in, cast_out, cast_shapes, cast_bytes = _round_block_specs(cast_jobs, b * steps, lambda i, j: i * steps + j)
    table = jax.ShapeDtypeStruct((b, s, 2 * RET_QK_DIM), F32)
    table_spec = pl.BlockSpec((None, rows, 2 * RET_QK_DIM), lambda i, j: (i, j, 0))
    return pl.pallas_call(
        functools.partial(_rope_kernel, n_cast=len(cast_jobs)),
        grid=(b, steps),
        in_specs=[
            pl.BlockSpec((None, rows, 1), lambda i, j: (i, j, 0)),
            pl.BlockSpec((1, RET_QK_DIM), lambda i, j: (0, 0)),
            pl.BlockSpec((1, RET_QK_DIM), lambda i, j: (0, 0)),
        ] + cast_in,
        out_specs=[table_spec] + cast_out,
        out_shape=[table] + cast_shapes,
        compiler_params=pltpu.CompilerParams(
            dimension_semantics=("arbitrary", "arbitrary"),
            vmem_limit_bytes=_vmem_limit(0, cast_bytes + 4 * rows * 3 * RET_QK_DIM, 0, 16 * 4 * rows * RET_QK_DIM)),
        name="rope_tables",
    )(positions.reshape(b, s, 1), freq, sign, *[w for w, _ in cast_jobs])


GELU_C0 = 0.7978845608028654
GELU_C1 = GELU_C0 * 0.044715


def _sigmoid(x):
    return 0.5 * jnp.tanh(0.5 * x) + 0.5


def _gelu_tanh(x):
    half_x = 0.5 * x
    return half_x + half_x * jnp.tanh(x * (GELU_C0 + GELU_C1 * (x * x)))


N_MIXER_INPUTS = 13


def _mixer_kernel(*refs, n_cast):
    ins = refs[:N_MIXER_INPUTS]
    cast_in = refs[N_MIXER_INPUTS:N_MIXER_INPUTS + n_cast]
    o_ref = refs[N_MIXER_INPUTS + n_cast]
    cast_out = refs[N_MIXER_INPUTS + n_cast + 1:N_MIXER_INPUTS + 2 * n_cast + 1]
    scratch = refs[N_MIXER_INPUTS + 2 * n_cast + 1:]
    _mixer_step(*ins, o_ref, *scratch)
    _round_blocks(cast_in, cast_out)


def _mixer_step(dchunk_ref, x_ref, mod_ref, prm_ref, rope_ref, win_ref, wbr_ref, wout_ref,
                wpool_ref, ws_ref, bst_ref, dintra_ref, dqk_ref,
                o_ref,
                state_ref, ptail_ref, qb_ref, qd_ref, kb_ref, kd_ref, vb_ref, u_ref, vn_ref, y_ref, m_ref):
    rows, d = x_ref.shape
    n_chunks = rows // CHUNK
    qk_w = RET_HEADS * RET_QK_DIM
    v_w = RET_HEADS * RET_V_DIM
    gdim = d // GMLP_GROUPS
    pdim = d // len(POOL_WINDOWS)
    o_q, o_k, o_v = 0, qk_w, 2 * qk_w
    o_g = o_v + v_w
    o_u = o_g + v_w
    o_vs = o_u + d
    o_p = o_vs + d
    o_gate = o_p + d
    seq_step = pl.program_id(1)
    row_slices = [slice(c * CHUNK, (c + 1) * CHUNK) for c in range(n_chunks)]
    qk_slices = [slice(hd * RET_QK_DIM, (hd + 1) * RET_QK_DIM) for hd in range(RET_HEADS)]
    v_slices = [slice(hd * RET_V_DIM, (hd + 1) * RET_V_DIM) for hd in range(RET_HEADS)]

    @pl.when(seq_step == 0)
    def _():
        state_ref[...] = jnp.zeros_like(state_ref)
        ptail_ref[...] = jnp.zeros_like(ptail_ref)

    x = x_ref[...]
    batch_row = pl.ds(pl.program_id(0), 1)
    sh1, sc1, gt1 = mod_ref[0, batch_row, :], mod_ref[1, batch_row, :], mod_ref[2, batch_row, :]
    h = (_rms(x) * (prm_ref[0:1, :] * (1.0 + sc1)) + sh1).astype(BF16)

    def proj(lo, width):
        return _dot(h, _unpack_rows(win_ref[:, lo:lo + width]))

    def add_gated_branch(n, first=False):
        for lo in range(0, d, GATE_BLOCK):
            cols = slice(lo, lo + GATE_BLOCK)
            gated = _sigmoid(proj(o_gate + n * d + lo, GATE_BLOCK)) * _dot(y_ref[n], _unpack_rows(wbr_ref[n, :, cols]))
            m_ref[:, cols] = gated if first else m_ref[:, cols] + gated

    cosv, sinv = rope_ref[:, :RET_QK_DIM], rope_ref[:, RET_QK_DIM:]
    qf, kf = proj(o_q, qk_w), proj(o_k, qk_w)
    k_scale = RET_QK_DIM ** -0.5
    for hd, sl in enumerate(qk_slices):
        q_h, k_h = qf[:, sl], kf[:, sl]
        q_r = q_h * cosv + pltpu.roll(q_h, RET_QK_DIM // 2, 1) * sinv
        k_r = (k_h * cosv + pltpu.roll(k_h, RET_QK_DIM // 2, 1) * sinv) * k_scale
        qb_ref[:, sl] = q_r.astype(BF16)
        qd_ref[:, sl] = (q_r * dqk_ref[0, hd]).astype(BF16)
        kb_ref[:, sl] = k_r.astype(BF16)
        kd_ref[:, sl] = (k_r * dqk_ref[1, hd]).astype(BF16)
    vb_ref[...] = proj(o_v, v_w).astype(BF16)
    vs = _gelu_tanh(proj(o_vs, d))
    vn_ref[...] = (_rms(vs) * prm_ref[1:2, :]).astype(BF16)

    scores, kv = {}, {}
    for c, rs in enumerate(row_slices):
        for hd, sl in enumerate(qk_slices):
            s = lax.dot_general(qb_ref[rs, sl], kb_ref[rs, sl], (((1,), (1,)), ((), ())),
                                preferred_element_type=F32)
            scores[c, hd] = (s * dintra_ref[hd]).astype(BF16)
            kv[c, hd] = lax.dot_general(kd_ref[rs, sl], vb_ref[rs, v_slices[hd]], (((0,), (0,)), ((), ())),
                                        preferred_element_type=F32)

    u_ref[...] = _gelu_tanh(proj(o_u, d))

    for hd, (sl, vsl) in enumerate(zip(qk_slices, v_slices)):
        state = state_ref[hd]
        ret = []
        for c, rs in enumerate(row_slices):
            lhs = jnp.concatenate([scores[c, hd], qd_ref[rs, sl]], axis=1)
            rhs = jnp.concatenate([vb_ref[rs, vsl], state.astype(BF16)], axis=0)
            ret.append(_dot(lhs, rhs))
            state = dchunk_ref[hd] * state + kv[c, hd]
        state_ref[hd] = state
        half_g = 0.5 * proj(o_g + hd * RET_V_DIM, RET_V_DIM)
        silu_g = half_g + half_g * jnp.tanh(half_g)
        for c, rs in enumerate(row_slices):
            y_ref[0, rs, vsl] = (silu_g[rs, :] * _rms(ret[c])).astype(BF16)

    p = proj(o_p, d)
    ext = jnp.concatenate([ptail_ref[...], p], axis=0)
    ptail_ref[...] = p[rows - POOL_HISTORY:, :]
    pos = seq_step * rows + lax.broadcasted_iota(jnp.int32, (rows, pdim), 0)
    pooled = []
    for gi, window in enumerate(POOL_WINDOWS):
        psl = slice(gi * pdim, (gi + 1) * pdim)
        acc = ext[:, psl]
        span = 1
        while span < window:
            acc = acc + pltpu.roll(acc, span, 0)
            span *= 2
        count = jnp.minimum(pos + 1, window).astype(F32)
        pooled.append((acc[POOL_HISTORY:, :] / count - p[:, psl]).astype(BF16))

    tri = lax.broadcasted_iota(jnp.int32, (CHUNK, CHUNK), 0) >= lax.broadcasted_iota(jnp.int32, (CHUNK, CHUNK), 1)
    for gi in range(GMLP_GROUPS):
        gsl = slice(gi * gdim, (gi + 1) * gdim)
        w_tri = jnp.where(tri, ws_ref[gi], 0.0).astype(BF16)
        bias = jnp.broadcast_to(bst_ref[:, gi:gi + 1], (CHUNK, gdim))
        for rs in row_slices:
            mixed = _dot(w_tri, vn_ref[rs, gsl]) + bias
            y_ref[1, rs, gsl] = (u_ref[rs, gsl] * mixed).astype(BF16)

    add_gated_branch(0, first=True)

    for gi in range(len(POOL_WINDOWS)):
        psl = slice(gi * pdim, (gi + 1) * pdim)
        mixed = _dot(pooled[gi], _unpack_rows(wpool_ref[gi])) + prm_ref[2:3, psl]
        y_ref[2, :, psl] = (mixed * prm_ref[3:4, psl]).astype(BF16)

    add_gated_branch(1)
    add_gated_branch(2)
    o_ref[...] = x + gt1 * _dot(m_ref[...].astype(BF16), _unpack_rows(wout_ref[...]))


def _mixer(layer, x, mod, prm, rope_t, w_in, w_branch, w_out, w_pool, ws, bs_t, dintra, dqk, dchunk, cast_jobs):
    b, s, d = x.shape
    rows = MIXER_ROWS
    steps = s // rows
    d_in = w_in.shape[1]
    v_w = RET_HEADS * RET_V_DIM
    qk_w = RET_HEADS * RET_QK_DIM
    row_spec = lambda width: pl.BlockSpec((None, rows, width), lambda i, j: (i, j, 0))
    lres = functools.partial(_layer_resident, layer)
    in_specs = [
        pl.BlockSpec(memory_space=pltpu.SMEM),
        row_spec(d),
        lres(mod.shape[1:]),
        lres(prm.shape[1:]),
        row_spec(2 * RET_QK_DIM),
        _resident(w_in.shape), _resident(w_branch.shape), _resident(w_out.shape),
        _resident(w_pool.shape), lres(ws.shape[1:]), lres(bs_t.shape[1:]),
        _resident(dintra.shape), _resident(dqk.shape),
    ]
    assert len(in_specs) == N_MIXER_INPUTS
    cast_in, cast_out, cast_shapes, cast_bytes = _round_block_specs(cast_jobs, b * steps, lambda i, j: i * steps + j)
    in_specs += cast_in
    out_specs = [row_spec(d)] + cast_out
    out_shape = [jax.ShapeDtypeStruct((b, s, d), F32)] + cast_shapes
    scratch = [
        pltpu.VMEM((RET_HEADS, RET_QK_DIM, RET_V_DIM), F32),
        pltpu.VMEM((POOL_HISTORY, d), F32),
        pltpu.VMEM((rows, qk_w), BF16), pltpu.VMEM((rows, qk_w), BF16),
        pltpu.VMEM((rows, qk_w), BF16), pltpu.VMEM((rows, qk_w), BF16),
        pltpu.VMEM((rows, v_w), BF16),
        pltpu.VMEM((rows, d), F32),
        pltpu.VMEM((rows, d), BF16),
        pltpu.VMEM((N_BRANCH, rows, d), BF16),
        pltpu.VMEM((rows, d), F32),
    ]
    resident = 4 * (w_in.size + w_branch.size + w_out.size + w_pool.size) + 4 * (
        ws[0].size + bs_t[0].size + dintra.size + dqk.size + mod[0].size + prm[0].size)
    streamed = 4 * rows * (2 * d + 2 * RET_QK_DIM) + cast_bytes
    scratch_bytes = 4 * (RET_HEADS * RET_QK_DIM * RET_V_DIM + POOL_HISTORY * d) + rows * (
        2 * 4 * qk_w + 2 * v_w + 4 * d + 2 * d + 2 * N_BRANCH * d + 4 * d)
    temps = 8 * 4 * rows * d
    return pl.pallas_call(
        functools.partial(_mixer_kernel, n_cast=len(cast_jobs)),
        grid=(b, steps),
        in_specs=in_specs,
        out_specs=out_specs,
        out_shape=out_shape,
        scratch_shapes=scratch,
        compiler_params=pltpu.CompilerParams(
            dimension_semantics=("arbitrary", "arbitrary"),
            vmem_limit_bytes=_vmem_limit(resident, streamed, scratch_bytes, temps)),
        name="token_mixer",
    )(dchunk, x, mod, prm, rope_t, w_in, w_branch, w_out, w_pool, ws, bs_t, dintra, dqk,
      *[w for w, _ in cast_jobs])


def _ffn_kernel(x_ref, mod_ref, n2_ref, w1_ref, w2_ref, fn_ref, o_ref, hid_ref, *, final):
    x = x_ref[...]
    batch_row = pl.ds(pl.program_id(0), 1)
    sh2, sc2, gt2 = mod_ref[3, batch_row, :], mod_ref[4, batch_row, :], mod_ref[5, batch_row, :]
    h = (_rms(x) * (n2_ref[...] * (1.0 + sc2)) + sh2).astype(BF16)
    d_ff = w1_ref.shape[1]
    for lo in range(0, d_ff, FF_BLOCK):
        hid = _dot(h, _unpack_rows(w1_ref[:, lo:lo + FF_BLOCK]))
        hid_ref[:, lo:lo + FF_BLOCK] = jnp.square(jnp.maximum(hid, 0.0)).astype(BF16)
    y = x + gt2 * _dot(hid_ref[...], _unpack_rows(w2_ref[...]))
    if final:
        y = _rms(y) * fn_ref[...]
    o_ref[...] = y


def _ffn(layer, x, mod, norm2, w1, w2, final_norm, final):
    b, s, d = x.shape
    rows = FFN_ROWS
    d_ff = w1.shape[1]
    row_spec = pl.BlockSpec((None, rows, d), lambda i, j: (i, j, 0))
    lres = functools.partial(_layer_resident, layer)
    resident = 2 * 2 * d * d_ff + 4 * 2 * d + 4 * mod[0].size
    streamed = 4 * 2 * rows * d
    temps = rows * (2 * d + 6 * FF_BLOCK + 3 * 4 * d)
    return pl.pallas_call(
        functools.partial(_ffn_kernel, final=final),
        grid=(b, s // rows),
        in_specs=[
            row_spec,
            lres(mod.shape[1:]),
            lres((1, d)), _resident(w1.shape), _resident(w2.shape), _resident((1, d)),
        ],
        out_specs=row_spec,
        out_shape=jax.ShapeDtypeStruct((b, s, d), F32),
        scratch_shapes=[pltpu.VMEM((rows, d_ff), BF16)],
        compiler_params=pltpu.CompilerParams(
            dimension_semantics=("arbitrary", "arbitrary"),
            vmem_limit_bytes=_vmem_limit(resident, streamed, 2 * rows * d_ff, temps)),
        name="channel_mlp",
    )(x, mod, norm2, w1, w2, final_norm)


def _retention_decay_tables(rows):
    log_gamma = jnp.log1p(-jnp.power(2.0, -5.0 - jnp.arange(RET_HEADS, dtype=F32)))
    pos = jnp.arange(CHUNK, dtype=F32)
    rel = pos[:, None] - pos[None, :]
    causal = rel >= 0
    intra = jnp.where(causal[None], jnp.exp(log_gamma[:, None, None] * jnp.where(causal, rel, 0.0)[None]), 0.0)
    decay_q = jnp.exp(log_gamma[:, None] * (pos + 1.0)[None])
    decay_k = jnp.exp(log_gamma[:, None] * (CHUNK - 1.0 - pos)[None])
    decay_chunk = jnp.exp(log_gamma * CHUNK)
    reps = rows // CHUNK
    widen = lambda t: jnp.broadcast_to(jnp.tile(t, (1, reps))[:, :, None], (RET_HEADS, rows, RET_QK_DIM))
    return intra, jnp.stack([widen(decay_q), widen(decay_k)]), decay_chunk


def kernel(x, c, positions, w_ada, b_ada, norm1, norm2, w_in, ws_gmlp, bs_gmlp, vnorm_gmlp, w_pool, b_pool,
           pool_scale, w_branch, w_out, w_ff1, w_ff2, final_norm):
    depth = w_in.shape[0]
    b, s, d = x.shape
    assert s % MIXER_ROWS == 0 and s % FFN_ROWS == 0 and s % ROPE_ROWS == 0 and MIXER_ROWS % CHUNK == 0
    assert w_ff1.shape[2] % FF_BLOCK == 0

    mod = _ada_rows(c, w_ada, b_ada)
    dintra, dqk, dchunk = _retention_decay_tables(MIXER_ROWS)
    rows_of = lambda v: v.reshape(depth, 1, d)
    prm = jnp.concatenate([rows_of(v) for v in (norm1, vnorm_gmlp, b_pool, pool_scale)], axis=1)
    norm2 = rows_of(norm2)
    bs_t = jnp.swapaxes(bs_gmlp, 1, 2)

    mixer_w = (w_in, w_branch, w_out, w_pool)
    as_rows = lambda w: w.reshape(depth, -1, w.shape[-1])
    rope_t, *next_mixer = _rope_tables(positions, [(as_rows(w), 0) for w in mixer_w])
    for l in range(depth):
        mixer_bf16 = [w.reshape(ref.shape[1:-2] + (ref.shape[-2] // ROWS_PER_WORD, ref.shape[-1]))
                      for w, ref in zip(next_mixer, mixer_w)]
        jobs = [(as_rows(w_ff1), l), (as_rows(w_ff2), l)]
        if l + 1 < depth:
            jobs += [(as_rows(w), l + 1) for w in mixer_w]
        x, ff1, ff2, *next_mixer = _mixer(l, x, mod, prm, rope_t, *mixer_bf16, ws_gmlp, bs_t,
                                          dintra, dqk, dchunk, jobs)
        x = _ffn(l, x, mod, norm2, ff1, ff2, final_norm.reshape(1, d), final=(l == depth - 1))
    return x
```

```python
import functools

import jax
import jax.numpy as jnp
from jax import lax
from jax.experimental import pallas as pl
from jax.experimental.pallas import tpu as pltpu

CHUNK = 128
RET_HEADS = 4
RET_QK_DIM = 128
RET_V_DIM = 256
GMLP_GROUPS = 4
POOL_WINDOWS = (2, 4, 8, 16)
POOL_HISTORY = 16
N_BRANCH = 3
N_MOD = 6
ROPE_BASE = 10000.0
EPS = 1e-6

VMEM_BYTES_V7X = 64 * 1024 * 1024
VMEM_REQUEST_CAP_V7X = VMEM_BYTES_V7X - 4 * 1024 * 1024
BF16_ROW_TILE = 16

MIXER_ROWS = 512
FFN_ROWS = 1024
FF_BLOCK = 1024
GATE_BLOCK = 256
ROPE_ROWS = 2048

F32 = jnp.float32
BF16 = jnp.bfloat16


def _dot(a, b):
    return jnp.dot(a, b, preferred_element_type=F32)


def _rms(x):
    return x * lax.rsqrt(jnp.mean(x * x, axis=-1, keepdims=True) + EPS)


def _resident(shape):
    zeros = (0,) * len(shape)
    return pl.BlockSpec(shape, lambda *_: zeros, pipeline_mode=pl.Buffered(1))


def _layer_resident(layer, shape):
    index = (layer,) + (0,) * len(shape)
    return pl.BlockSpec((None,) + tuple(shape), lambda *_: index, pipeline_mode=pl.Buffered(1))


def _vmem_limit(resident_bytes, streamed_bytes, scratch_bytes, temp_bytes):
    need = resident_bytes + 2 * streamed_bytes + scratch_bytes + temp_bytes
    return min(int(need), VMEM_REQUEST_CAP_V7X)


def _ada_kernel(c_ref, w_ref, b_ref, o_ref):
    c = c_ref[...]
    c_act = (c * jax.nn.sigmoid(c)).astype(BF16)
    o_ref[...] = _dot(c_act, w_ref[...].astype(BF16)) + b_ref[...]


def _ada_rows(c, w_ada, b_ada):
    depth, d, n = w_ada.shape
    b = c.shape[0]
    assert n == N_MOD * d
    return pl.pallas_call(
        _ada_kernel,
        grid=(depth, N_MOD),
        in_specs=[
            pl.BlockSpec((b, d), lambda l, j: (0, 0)),
            pl.BlockSpec((None, d, d), lambda l, j: (l, 0, j)),
            pl.BlockSpec((None, 1, d), lambda l, j: (l, 0, j)),
        ],
        out_specs=pl.BlockSpec((None, None, b, d), lambda l, j: (l, j, 0, 0)),
        out_shape=jax.ShapeDtypeStruct((depth, N_MOD, b, d), F32),
        compiler_params=pltpu.CompilerParams(
            dimension_semantics=("arbitrary", "arbitrary")),
        name="ada_rows",
    )(c, w_ada, b_ada.reshape(depth, 1, n))


PACKED = jnp.int32
ROWS_PER_WORD = 2


def _pack_rows(w_bf16):
    return pltpu.bitcast(w_bf16, PACKED)


def _unpack_rows(w_packed):
    return pltpu.bitcast(w_packed, BF16)


def _round_blocks(cast_in, cast_out):
    for src_ref, dst_ref in zip(cast_in, cast_out):
        dst_ref[...] = _pack_rows(src_ref[...].astype(BF16))


def _round_block_specs(cast_jobs, n_steps, flat_step):
    in_specs, out_specs, out_shapes, nbytes = [], [], [], 0
    for w, layer in cast_jobs:
        _, r, c = w.shape
        blk = r // n_steps
        assert blk * n_steps == r and blk % BF16_ROW_TILE == 0
        in_specs.append(pl.BlockSpec((None, blk, c), lambda i, j, layer=layer: (layer, flat_step(i, j), 0)))
        out_specs.append(pl.BlockSpec((blk // ROWS_PER_WORD, c), lambda i, j: (flat_step(i, j), 0)))
        out_shapes.append(jax.ShapeDtypeStruct((r // ROWS_PER_WORD, c), PACKED))
        nbytes += blk * c * (4 + 2)
    return in_specs, out_specs, out_shapes, nbytes


N_ROPE_INPUTS = 3


def _rope_kernel(*refs, n_cast):
    pos_ref, freq_ref, sign_ref = refs[:N_ROPE_INPUTS]
    cast_in = refs[N_ROPE_INPUTS:N_ROPE_INPUTS + n_cast]
    tab_ref = refs[N_ROPE_INPUTS + n_cast]
    cast_out = refs[N_ROPE_INPUTS + n_cast + 1:]
    _round_blocks(cast_in, cast_out)

    rows = pos_ref.shape[0]
    half_rows, half = rows // 2, RET_QK_DIM // 2
    pos = pos_ref[...].astype(F32)
    low = lax.broadcasted_iota(jnp.int32, (half_rows, RET_QK_DIM), 1) < half
    ang = jnp.where(low, pos[:half_rows], pos[half_rows:]) * freq_ref[...]
    cos_p, sin_p = jnp.cos(ang), jnp.sin(ang)
    cos_s, sin_s = pltpu.roll(cos_p, half, 1), pltpu.roll(sin_p, half, 1)
    sign = sign_ref[...]
    tab_ref[:half_rows, :RET_QK_DIM] = jnp.where(low, cos_p, cos_s)
    tab_ref[half_rows:, :RET_QK_DIM] = jnp.where(low, cos_s, cos_p)
    tab_ref[:half_rows, RET_QK_DIM:] = jnp.where(low, sin_p, sin_s) * sign
    tab_ref[half_rows:, RET_QK_DIM:] = jnp.where(low, sin_s, sin_p) * sign


def _rope_tables(positions, cast_jobs):
    b, s = positions.shape
    half = RET_QK_DIM // 2
    inv_freq = ROPE_BASE ** (-jnp.arange(half, dtype=F32) / half)
    freq = jnp.concatenate([inv_freq, inv_freq]).reshape(1, RET_QK_DIM)
    sign = jnp.concatenate([-jnp.ones((half,), F32), jnp.ones((half,), F32)]).reshape(1, RET_QK_DIM)
    rows = ROPE_ROWS
    steps = s // rows
    cast_in, cast_out, cast_shapes, cast_bytes = _round_block_specs(cast_jobs, b * steps, lambda i, j: i * steps + j)
    table = jax.ShapeDtypeStruct((b, s, 2 * RET_QK_DIM), F32)
    table_spec = pl.BlockSpec((None, rows, 2 * RET_QK_DIM), lambda i, j: (i, j, 0))
    return pl.pallas_call(
        functools.partial(_rope_kernel, n_cast=len(cast_jobs)),
        grid=(b, steps),
        in_specs=[
            pl.BlockSpec((None, rows, 1), lambda i, j: (i, j, 0)),
            pl.BlockSpec((1, RET_QK_DIM), lambda i, j: (0, 0)),
            pl.BlockSpec((1, RET_QK_DIM), lambda i, j: (0, 0)),
        ] + cast_in,
        out_specs=[table_spec] + cast_out,
        out_shape=[table] + cast_shapes,
        compiler_params=pltpu.CompilerParams(
            dimension_semantics=("arbitrary", "arbitrary"),
            vmem_limit_bytes=_vmem_limit(0, cast_bytes + 4 * rows * 3 * RET_QK_DIM, 0, 16 * 4 * rows * RET_QK_DIM)),
        name="rope_tables",
    )(positions.reshape(b, s, 1), freq, sign, *[w for w, _ in cast_jobs])


GELU_C0 = 0.7978845608028654
GELU_C1 = GELU_C0 * 0.044715


def _sigmoid(x):
    return 0.5 * jnp.tanh(0.5 * x) + 0.5


def _gelu_tanh(x):
    half_x = 0.5 * x
    return half_x + half_x * jnp.tanh(x * (GELU_C0 + GELU_C1 * (x * x)))


N_MIXER_INPUTS = 13


def _mixer_kernel(*refs, n_cast):
    ins = refs[:N_MIXER_INPUTS]
    cast_in = refs[N_MIXER_INPUTS:N_MIXER_INPUTS + n_cast]
    o_ref = refs[N_MIXER_INPUTS + n_cast]
    cast_out = refs[N_MIXER_INPUTS + n_cast + 1:N_MIXER_INPUTS + 2 * n_cast + 1]
    scratch = refs[N_MIXER_INPUTS + 2 * n_cast + 1:]
    _mixer_step(*ins, o_ref, *scratch)
    _round_blocks(cast_in, cast_out)


def _mixer_step(dchunk_ref, x_ref, mod_ref, prm_ref, rope_ref, win_ref, wbr_ref, wout_ref,
                wpool_ref, ws_ref, bst_ref, dintra_ref, dqk_ref,
                o_ref,
                state_ref, ptail_ref, qb_ref, qd_ref, kb_ref, kd_ref, vb_ref, u_ref, vn_ref, y_ref, m_ref):
    rows, d = x_ref.shape
    n_chunks = rows // CHUNK
    qk_w = RET_HEADS * RET_QK_DIM
    v_w = RET_HEADS * RET_V_DIM
    gdim = d // GMLP_GROUPS
    pdim = d // len(POOL_WINDOWS)
    o_q, o_k, o_v = 0, qk_w, 2 * qk_w
    o_g = o_v + v_w
    o_u = o_g + v_w
    o_vs = o_u + d
    o_p = o_vs + d
    o_gate = o_p + d
    seq_step = pl.program_id(1)
    row_slices = [slice(c * CHUNK, (c + 1) * CHUNK) for c in range(n_chunks)]
    qk_slices = [slice(hd * RET_QK_DIM, (hd + 1) * RET_QK_DIM) for hd in range(RET_HEADS)]
    v_slices = [slice(hd * RET_V_DIM, (hd + 1) * RET_V_DIM) for hd in range(RET_HEADS)]

    @pl.when(seq_step == 0)
    def _():
        state_ref[...] = jnp.zeros_like(state_ref)
        ptail_ref[...] = jnp.zeros_like(ptail_ref)

    x = x_ref[...]
    batch_row = pl.ds(pl.program_id(0), 1)
    sh1, sc1, gt1 = mod_ref[0, batch_row, :], mod_ref[1, batch_row, :], mod_ref[2, batch_row, :]
    h = (_rms(x) * (prm_ref[0:1, :] * (1.0 + sc1)) + sh1).astype(BF16)

    def proj(lo, width):
        return _dot(h, _unpack_rows(win_ref[:, lo:lo + width]))

    def add_gated_branch(n, first=False):
        for lo in range(0, d, GATE_BLOCK):
            cols = slice(lo, lo + GATE_BLOCK)
            gated = _sigmoid(proj(o_gate + n * d + lo, GATE_BLOCK)) * _dot(y_ref[n], _unpack_rows(wbr_ref[n, :, cols]))
            m_ref[:, cols] = gated if first else m_ref[:, cols] + gated

    cosv, sinv = rope_ref[:, :RET_QK_DIM], rope_ref[:, RET_QK_DIM:]
    qf, kf = proj(o_q, qk_w), proj(o_k, qk_w)
    k_scale = RET_QK_DIM ** -0.5
    for hd, sl in enumerate(qk_slices):
        q_h, k_h = qf[:, sl], kf[:, sl]
        q_r = q_h * cosv + pltpu.roll(q_h, RET_QK_DIM // 2, 1) * sinv
        k_r = (k_h * cosv + pltpu.roll(k_h, RET_QK_DIM // 2, 1) * sinv) * k_scale
        qb_ref[:, sl] = q_r.astype(BF16)
        qd_ref[:, sl] = (q_r * dqk_ref[0, hd]).astype(BF16)
        kb_ref[:, sl] = k_r.astype(BF16)
        kd_ref[:, sl] = (k_r * dqk_ref[1, hd]).astype(BF16)
    vb_ref[...] = proj(o_v, v_w).astype(BF16)
    vs = _gelu_tanh(proj(o_vs, d))
    vn_ref[...] = (_rms(vs) * prm_ref[1:2, :]).astype(BF16)

    scores, kv = {}, {}
    for c, rs in enumerate(row_slices):
        for hd, sl in enumerate(qk_slices):
            s = lax.dot_general(qb_ref[rs, sl], kb_ref[rs, sl], (((1,), (1,)), ((), ())),
                                preferred_element_type=F32)
            scores[c, hd] = (s * dintra_ref[hd]).astype(BF16)
            kv[c, hd] = lax.dot_general(kd_ref[rs, sl], vb_ref[rs, v_slices[hd]], (((0,), (0,)), ((), ())),
                                        preferred_element_type=F32)

    u_ref[...] = _gelu_tanh(proj(o_u, d))

    for hd, (sl, vsl) in enumerate(zip(qk_slices, v_slices)):
        state = state_ref[hd]
        ret = []
        for c, rs in enumerate(row_slices):
            lhs = jnp.concatenate([scores[c, hd], qd_ref[rs, sl]], axis=1)
            rhs = jnp.concatenate([vb_ref[rs, vsl], state.astype(BF16)], axis=0)
            ret.append(_dot(lhs, rhs))
            state = dchunk_ref[hd] * state + kv[c, hd]
        state_ref[hd] = state
        half_g = 0.5 * proj(o_g + hd * RET_V_DIM, RET_V_DIM)
        silu_g = half_g + half_g * jnp.tanh(half_g)
        for c, rs in enumerate(row_slices):
            y_ref[0, rs, vsl] = (silu_g[rs, :] * _rms(ret[c])).astype(BF16)

    p = proj(o_p, d)
    ext = jnp.concatenate([ptail_ref[...], p], axis=0)
    ptail_ref[...] = p[rows - POOL_HISTORY:, :]
    pos = seq_step * rows + lax.broadcasted_iota(jnp.int32, (rows, pdim), 0)
    pooled = []
    for gi, window in enumerate(POOL_WINDOWS):
        psl = slice(gi * pdim, (gi + 1) * pdim)
        acc = ext[:, psl]
        span = 1
        while span < window:
            acc = acc + pltpu.roll(acc, span, 0)
            span *= 2
        count = jnp.minimum(pos + 1, window).astype(F32)
        pooled.append((acc[POOL_HISTORY:, :] / count - p[:, psl]).astype(BF16))

    tri = lax.broadcasted_iota(jnp.int32, (CHUNK, CHUNK), 0) >= lax.broadcasted_iota(jnp.int32, (CHUNK, CHUNK), 1)
    for gi in range(GMLP_GROUPS):
        gsl = slice(gi * gdim, (gi + 1) * gdim)
        w_tri = jnp.where(tri, ws_ref[gi], 0.0).astype(BF16)
        bias = jnp.broadcast_to(bst_ref[:, gi:gi + 1], (CHUNK, gdim))
        for rs in row_slices:
            mixed = _dot(w_tri, vn_ref[rs, gsl]) + bias
            y_ref[1, rs, gsl] = (u_ref[rs, gsl] * mixed).astype(BF16)

    add_gated_branch(0, first=True)

    for gi in range(len(POOL_WINDOWS)):
        psl = slice(gi * pdim, (gi + 1) * pdim)
        mixed = _dot(pooled[gi], _unpack_rows(wpool_ref[gi])) + prm_ref[2:3, psl]
        y_ref[2, :, psl] = (mixed * prm_ref[3:4, psl]).astype(BF16)

    add_gated_branch(1)
    add_gated_branch(2)
    o_ref[...] = x + gt1 * _dot(m_ref[...].astype(BF16), _unpack_rows(wout_ref[...]))


def _mixer(layer, x, mod, prm, rope_t, w_in, w_branch, w_out, w_pool, ws, bs_t, dintra, dqk, dchunk, cast_jobs):
    b, s, d = x.shape
    rows = MIXER_ROWS
    steps = s // rows
    d_in = w_in.shape[1]
    v_w = RET_HEADS * RET_V_DIM
    qk_w = RET_HEADS * RET_QK_DIM
    row_spec = lambda width: pl.BlockSpec((None, rows, width), lambda i, j: (i, j, 0))
    lres = functools.partial(_layer_resident, layer)
    in_specs = [
        pl.BlockSpec(memory_space=pltpu.SMEM),
        row_spec(d),
        lres(mod.shape[1:]),
        lres(prm.shape[1:]),
        row_spec(2 * RET_QK_DIM),
        _resident(w_in.shape), _resident(w_branch.shape), _resident(w_out.shape),
        _resident(w_pool.shape), lres(ws.shape[1:]), lres(bs_t.shape[1:]),
        _resident(dintra.shape), _resident(dqk.shape),
    ]
    assert len(in_specs) == N_MIXER_INPUTS
    cast_in, cast_out, cast_shapes, cast_bytes = _round_block_specs(cast_jobs, b * steps, lambda i, j: i * steps + j)
    in_specs += cast_in
    out_specs = [row_spec(d)] + cast_out
    out_shape = [jax.ShapeDtypeStruct((b, s, d), F32)] + cast_shapes
    scratch = [
        pltpu.VMEM((RET_HEADS, RET_QK_DIM, RET_V_DIM), F32),
        pltpu.VMEM((POOL_HISTORY, d), F32),
        pltpu.VMEM((rows, qk_w), BF16), pltpu.VMEM((rows, qk_w), BF16),
        pltpu.VMEM((rows, qk_w), BF16), pltpu.VMEM((rows, qk_w), BF16),
        pltpu.VMEM((rows, v_w), BF16),
        pltpu.VMEM((rows, d), F32),
        pltpu.VMEM((rows, d), BF16),
        pltpu.VMEM((N_BRANCH, rows, d), BF16),
        pltpu.VMEM((rows, d), F32),
    ]
    resident = 4 * (w_in.size + w_branch.size + w_out.size + w_pool.size) + 4 * (
        ws[0].size + bs_t[0].size + dintra.size + dqk.size + mod[0].size + prm[0].size)
    streamed = 4 * rows * (2 * d + 2 * RET_QK_DIM) + cast_bytes
    scratch_bytes = 4 * (RET_HEADS * RET_QK_DIM * RET_V_DIM + POOL_HISTORY * d) + rows * (
        2 * 4 * qk_w + 2 * v_w + 4 * d + 2 * d + 2 * N_BRANCH * d + 4 * d)
    temps = 8 * 4 * rows * d
    return pl.pallas_call(
        functools.partial(_mixer_kernel, n_cast=len(cast_jobs)),
        grid=(b, steps),
        in_specs=in_specs,
        out_specs=out_specs,
        out_shape=out_shape,
        scratch_shapes=scratch,
        compiler_params=pltpu.CompilerParams(
            dimension_semantics=("arbitrary", "arbitrary"),
            vmem_limit_bytes=_vmem_limit(resident, streamed, scratch_bytes, temps)),
        name="token_mixer",
    )(dchunk, x, mod, prm, rope_t, w_in, w_branch, w_out, w_pool, ws, bs_t, dintra, dqk,
      *[w for w, _ in cast_jobs])


def _ffn_kernel(x_ref, mod_ref, n2_ref, w1_ref, w2_ref, fn_ref, o_ref, hid_ref, *, final):
    x = x_ref[...]
    batch_row = pl.ds(pl.program_id(0), 1)
    sh2, sc2, gt2 = mod_ref[3, batch_row, :], mod_ref[4, batch_row, :], mod_ref[5, batch_row, :]
    h = (_rms(x) * (n2_ref[...] * (1.0 + sc2)) + sh2).astype(BF16)
    d_ff = w1_ref.shape[1]
    for lo in range(0, d_ff, FF_BLOCK):
        hid = _dot(h, _unpack_rows(w1_ref[:, lo:lo + FF_BLOCK]))
        hid_ref[:, lo:lo + FF_BLOCK] = jnp.square(jnp.maximum(hid, 0.0)).astype(BF16)
    y = x + gt2 * _dot(hid_ref[...], _unpack_rows(w2_ref[...]))
    if final:
        y = _rms(y) * fn_ref[...]
    o_ref[...] = y


def _ffn(layer, x, mod, norm2, w1, w2, final_norm, final):
    b, s, d = x.shape
    rows = FFN_ROWS
    d_ff = w1.shape[1]
    row_spec = pl.BlockSpec((None, rows, d), lambda i, j: (i, j, 0))
    lres = functools.partial(_layer_resident, layer)
    resident = 2 * 2 * d * d_ff + 4 * 2 * d + 4 * mod[0].size
    streamed = 4 * 2 * rows * d
    temps = rows * (2 * d + 6 * FF_BLOCK + 3 * 4 * d)
    return pl.pallas_call(
        functools.partial(_ffn_kernel, final=final),
        grid=(b, s // rows),
        in_specs=[
            row_spec,
            lres(mod.shape[1:]),
            lres((1, d)), _resident(w1.shape), _resident(w2.shape), _resident((1, d)),
        ],
        out_specs=row_spec,
        out_shape=jax.ShapeDtypeStruct((b, s, d), F32),
        scratch_shapes=[pltpu.VMEM((rows, d_ff), BF16)],
        compiler_params=pltpu.CompilerParams(
            dimension_semantics=("arbitrary", "arbitrary"),
            vmem_limit_bytes=_vmem_limit(resident, streamed, 2 * rows * d_ff, temps)),
        name="channel_mlp",
    )(x, mod, norm2, w1, w2, final_norm)


def _retention_decay_tables(rows):
    log_gamma = jnp.log1p(-jnp.power(2.0, -5.0 - jnp.arange(RET_HEADS, dtype=F32)))
    pos = jnp.arange(CHUNK, dtype=F32)
    rel = pos[:, None] - pos[None, :]
    causal = rel >= 0
    intra = jnp.where(causal[None], jnp.exp(log_gamma[:, None, None] * jnp.where(causal, rel, 0.0)[None]), 0.0)
    decay_q = jnp.exp(log_gamma[:, None] * (pos + 1.0)[None])
    decay_k = jnp.exp(log_gamma[:, None] * (CHUNK - 1.0 - pos)[None])
    decay_chunk = jnp.exp(log_gamma * CHUNK)
    reps = rows // CHUNK
    widen = lambda t: jnp.broadcast_to(jnp.tile(t, (1, reps))[:, :, None], (RET_HEADS, rows, RET_QK_DIM))
    return intra, jnp.stack([widen(decay_q), widen(decay_k)]), decay_chunk


def kernel(x, c, positions, w_ada, b_ada, norm1, norm2, w_in, ws_gmlp, bs_gmlp, vnorm_gmlp, w_pool, b_pool,
           pool_scale, w_branch, w_out, w_ff1, w_ff2, final_norm):
    depth = w_in.shape[0]
    b, s, d = x.shape
    assert s % MIXER_ROWS == 0 and s % FFN_ROWS == 0 and s % ROPE_ROWS == 0 and MIXER_ROWS % CHUNK == 0
    assert w_ff1.shape[2] % FF_BLOCK == 0

    mod = _ada_rows(c, w_ada, b_ada)
    dintra, dqk, dchunk = _retention_decay_tables(MIXER_ROWS)
    rows_of = lambda v: v.reshape(depth, 1, d)
    prm = jnp.concatenate([rows_of(v) for v in (norm1, vnorm_gmlp, b_pool, pool_scale)], axis=1)
    norm2 = rows_of(norm2)
    bs_t = jnp.swapaxes(bs_gmlp, 1, 2)

    mixer_w = (w_in, w_branch, w_out, w_pool)
    as_rows = lambda w: w.reshape(depth, -1, w.shape[-1])
    rope_t, *next_mixer = _rope_tables(positions, [(as_rows(w), 0) for w in mixer_w])
    for l in range(depth):
        mixer_bf16 = [w.reshape(ref.shape[1:-2] + (ref.shape[-2] // ROWS_PER_WORD, ref.shape[-1]))
                      for w, ref in zip(next_mixer, mixer_w)]
        jobs = [(as_rows(w_ff1), l), (as_rows(w_ff2), l)]
        if l + 1 < depth:
            jobs += [(as_rows(w), l + 1) for w in mixer_w]
        x, ff1, ff2, *next_mixer = _mixer(l, x, mod, prm, rope_t, *mixer_bf16, ws_gmlp, bs_t,
                                          dintra, dqk, dchunk, jobs)
        x = _ffn(l, x, mod, norm2, ff1, ff2, final_norm.reshape(1, d), final=(l == depth - 1))
    return x
```

```python
import functools

import jax
import jax.numpy as jnp
from jax import lax
from jax.experimental import pallas as pl
from jax.experimental.pallas import tpu as pltpu

CHUNK = 128
RET_HEADS = 4
RET_QK_DIM = 128
RET_V_DIM = 256
GMLP_GROUPS = 4
POOL_WINDOWS = (2, 4, 8, 16)
POOL_HISTORY = 16
N_BRANCH = 3
N_MOD = 6
ROPE_BASE = 10000.0
EPS = 1e-6

VMEM_BYTES_V7X = 64 * 1024 * 1024
VMEM_REQUEST_CAP_V7X = VMEM_BYTES_V7X - 4 * 1024 * 1024
BF16_ROW_TILE = 16

MIXER_ROWS = 256
FFN_ROWS = 1024
FF_BLOCK = 1024
GATE_BLOCK = 256
ROPE_ROWS = 2048

F32 = jnp.float32
BF16 = jnp.bfloat16


def _dot(a, b):
    return jnp.dot(a, b, preferred_element_type=F32)


def _rms(x):
    return x * lax.rsqrt(jnp.mean(x * x, axis=-1, keepdims=True) + EPS)


def _resident(shape):
    zeros = (0,) * len(shape)
    return pl.BlockSpec(shape, lambda *_: zeros, pipeline_mode=pl.Buffered(1))


def _layer_resident(layer, shape):
    index = (layer,) + (0,) * len(shape)
    return pl.BlockSpec((None,) + tuple(shape), lambda *_: index, pipeline_mode=pl.Buffered(1))


def _vmem_limit(resident_bytes, streamed_bytes, scratch_bytes, temp_bytes):
    need = resident_bytes + 2 * streamed_bytes + scratch_bytes + temp_bytes
    return min(int(need), VMEM_REQUEST_CAP_V7X)


def _ada_kernel(c_ref, w_ref, b_ref, o_ref):
    c = c_ref[...]
    c_act = (c * jax.nn.sigmoid(c)).astype(BF16)
    o_ref[...] = _dot(c_act, w_ref[...].astype(BF16)) + b_ref[...]


def _ada_rows(c, w_ada, b_ada):
    depth, d, n = w_ada.shape
    b = c.shape[0]
    assert n == N_MOD * d
    return pl.pallas_call(
        _ada_kernel,
        grid=(depth, N_MOD),
        in_specs=[
            pl.BlockSpec((b, d), lambda l, j: (0, 0)),
            pl.BlockSpec((None, d, d), lambda l, j: (l, 0, j)),
            pl.BlockSpec((None, 1, d), lambda l, j: (l, 0, j)),
        ],
        out_specs=pl.BlockSpec((None, None, b, d), lambda l, j: (l, j, 0, 0)),
        out_shape=jax.ShapeDtypeStruct((depth, N_MOD, b, d), F32),
        compiler_params=pltpu.CompilerParams(
            dimension_semantics=("arbitrary", "arbitrary")),
        name="ada_rows",
    )(c, w_ada, b_ada.reshape(depth, 1, n))


PACKED = jnp.int32
ROWS_PER_WORD = 2


def _pack_rows(w_bf16):
    return pltpu.bitcast(w_bf16, PACKED)


def _unpack_rows(w_packed):
    return pltpu.bitcast(w_packed, BF16)


def _round_blocks(cast_in, cast_out):
    for src_ref, dst_ref in zip(cast_in, cast_out):
        dst_ref[...] = _pack_rows(src_ref[...].astype(BF16))


def _round_block_specs(cast_jobs, n_steps, flat_step):
    in_specs, out_specs, out_shapes, nbytes = [], [], [], 0
    for w, layer in cast_jobs:
        _, r, c = w.shape
        blk = r // n_steps
        assert blk * n_steps == r and blk % BF16_ROW_TILE == 0
        in_specs.append(pl.BlockSpec((None, blk, c), lambda i, j, layer=layer: (layer, flat_step(i, j), 0)))
        out_specs.append(pl.BlockSpec((blk // ROWS_PER_WORD, c), lambda i, j: (flat_step(i, j), 0)))
        out_shapes.append(jax.ShapeDtypeStruct((r // ROWS_PER_WORD, c), PACKED))
        nbytes += blk * c * (4 + 2)
    return in_specs, out_specs, out_shapes, nbytes


N_ROPE_INPUTS = 3


def _rope_kernel(*refs, n_cast):
    pos_ref, freq_ref, sign_ref = refs[:N_ROPE_INPUTS]
    cast_in = refs[N_ROPE_INPUTS:N_ROPE_INPUTS + n_cast]
    tab_ref = refs[N_ROPE_INPUTS + n_cast]
    cast_out = refs[N_ROPE_INPUTS + n_cast + 1:]
    _round_blocks(cast_in, cast_out)

    rows = pos_ref.shape[0]
    half_rows, half = rows // 2, RET_QK_DIM // 2
    pos = pos_ref[...].astype(F32)
    low = lax.broadcasted_iota(jnp.int32, (half_rows, RET_QK_DIM), 1) < half
    ang = jnp.where(low, pos[:half_rows], pos[half_rows:]) * freq_ref[...]
    cos_p, sin_p = jnp.cos(ang), jnp.sin(ang)
    cos_s, sin_s = pltpu.roll(cos_p, half, 1), pltpu.roll(sin_p, half, 1)
    sign = sign_ref[...]
    tab_ref[:half_rows, :RET_QK_DIM] = jnp.where(low, cos_p, cos_s)
    tab_ref[half_rows:, :RET_QK_DIM] = jnp.where(low, cos_s, cos_p)
    tab_ref[:half_rows, RET_QK_DIM:] = jnp.where(low, sin_p, sin_s) * sign
    tab_ref[half_rows:, RET_QK_DIM:] = jnp.where(low, sin_s, sin_p) * sign


def _rope_tables(positions, cast_jobs):
    b, s = positions.shape
    half = RET_QK_DIM // 2
    inv_freq = ROPE_BASE ** (-jnp.arange(half, dtype=F32) / half)
    freq = jnp.concatenate([inv_freq, inv_freq]).reshape(1, RET_QK_DIM)
    sign = jnp.concatenate([-jnp.ones((half,), F32), jnp.ones((half,), F32)]).reshape(1, RET_QK_DIM)
    rows = ROPE_ROWS
    steps = s // rows
    cast_in, cast_out, cast_shapes, cast_bytes = _round_block_specs(cast_jobs, b * steps, lambda i, j: i * steps + j)
    table = jax.ShapeDtypeStruct((b, s, 2 * RET_QK_DIM), F32)
    table_spec = pl.BlockSpec((None, rows, 2 * RET_QK_DIM), lambda i, j: (i, j, 0))
    return pl.pallas_call(
        functools.partial(_rope_kernel, n_cast=len(cast_jobs)),
        grid=(b, steps),
        in_specs=[
            pl.BlockSpec((None, rows, 1), lambda i, j: (i, j, 0)),
            pl.BlockSpec((1, RET_QK_DIM), lambda i, j: (0, 0)),
            pl.BlockSpec((1, RET_QK_DIM), lambda i, j: (0, 0)),
        ] + cast_in,
        out_specs=[table_spec] + cast_out,
        out_shape=[table] + cast_shapes,
        compiler_params=pltpu.CompilerParams(
            dimension_semantics=("arbitrary", "arbitrary"),
            vmem_limit_bytes=_vmem_limit(0, cast_bytes + 4 * rows * 3 * RET_QK_DIM, 0, 16 * 4 * rows * RET_QK_DIM)),
        name="rope_tables",
    )(positions.reshape(b, s, 1), freq, sign, *[w for w, _ in cast_jobs])


GELU_C0 = 0.7978845608028654
GELU_C1 = GELU_C0 * 0.044715


def _sigmoid(x):
    return 0.5 * jnp.tanh(0.5 * x) + 0.5


def _gelu_tanh(x):
    half_x = 0.5 * x
    return half_x + half_x * jnp.tanh(x * (GELU_C0 + GELU_C1 * (x * x)))


N_MIXER_INPUTS = 13


def _mixer_kernel(*refs, n_cast):
    ins = refs[:N_MIXER_INPUTS]
    cast_in = refs[N_MIXER_INPUTS:N_MIXER_INPUTS + n_cast]
    o_ref = refs[N_MIXER_INPUTS + n_cast]
    cast_out = refs[N_MIXER_INPUTS + n_cast + 1:N_MIXER_INPUTS + 2 * n_cast + 1]
    scratch = refs[N_MIXER_INPUTS + 2 * n_cast + 1:]
    _mixer_step(*ins, o_ref, *scratch)
    _round_blocks(cast_in, cast_out)


def _mixer_step(dchunk_ref, x_ref, mod_ref, prm_ref, rope_ref, win_ref, wbr_ref, wout_ref,
                wpool_ref, ws_ref, bst_ref, dintra_ref, dqk_ref,
                o_ref,
                state_ref, ptail_ref, qb_ref, qd_ref, kb_ref, kd_ref, vb_ref, u_ref, vn_ref, y_ref, m_ref):
    rows, d = x_ref.shape
    n_chunks = rows // CHUNK
    qk_w = RET_HEADS * RET_QK_DIM
    v_w = RET_HEADS * RET_V_DIM
    gdim = d // GMLP_GROUPS
    pdim = d // len(POOL_WINDOWS)
    o_q, o_k, o_v = 0, qk_w, 2 * qk_w
    o_g = o_v + v_w
    o_u = o_g + v_w
    o_vs = o_u + d
    o_p = o_vs + d
    o_gate = o_p + d
    seq_step = pl.program_id(1)
    row_slices = [slice(c * CHUNK, (c + 1) * CHUNK) for c in range(n_chunks)]
    qk_slices = [slice(hd * RET_QK_DIM, (hd + 1) * RET_QK_DIM) for hd in range(RET_HEADS)]
    v_slices = [slice(hd * RET_V_DIM, (hd + 1) * RET_V_DIM) for hd in range(RET_HEADS)]

    @pl.when(seq_step == 0)
    def _():
        state_ref[...] = jnp.zeros_like(state_ref)
        ptail_ref[...] = jnp.zeros_like(ptail_ref)

    x = x_ref[...]
    batch_row = pl.ds(pl.program_id(0), 1)
    sh1, sc1, gt1 = mod_ref[0, batch_row, :], mod_ref[1, batch_row, :], mod_ref[2, batch_row, :]
    h = (_rms(x) * (prm_ref[0:1, :] * (1.0 + sc1)) + sh1).astype(BF16)

    def proj(lo, width):
        return _dot(h, _unpack_rows(win_ref[:, lo:lo + width]))

    def merge_branches():
        for lo in range(0, d, GATE_BLOCK):
            cols = slice(lo, lo + GATE_BLOCK)
            merged = None
            for n in range(N_BRANCH):
                gated = (_sigmoid(proj(o_gate + n * d + lo, GATE_BLOCK))
                         * _dot(y_ref[n], _unpack_rows(wbr_ref[n, :, cols])))
                merged = gated if merged is None else merged + gated
            m_ref[:, cols] = merged.astype(BF16)

    cosv, sinv = rope_ref[:, :RET_QK_DIM], rope_ref[:, RET_QK_DIM:]
    qf, kf = proj(o_q, qk_w), proj(o_k, qk_w)
    k_scale = RET_QK_DIM ** -0.5
    for hd, sl in enumerate(qk_slices):
        q_h, k_h = qf[:, sl], kf[:, sl]
        q_r = q_h * cosv + pltpu.roll(q_h, RET_QK_DIM // 2, 1) * sinv
        k_r = (k_h * cosv + pltpu.roll(k_h, RET_QK_DIM // 2, 1) * sinv) * k_scale
        qb_ref[:, sl] = q_r.astype(BF16)
        qd_ref[:, sl] = (q_r * dqk_ref[0, hd]).astype(BF16)
        kb_ref[:, sl] = k_r.astype(BF16)
        kd_ref[:, sl] = (k_r * dqk_ref[1, hd]).astype(BF16)
    vb_ref[...] = proj(o_v, v_w).astype(BF16)
    vs = _gelu_tanh(proj(o_vs, d))
    vn_ref[...] = (_rms(vs) * prm_ref[1:2, :]).astype(BF16)

    scores, kv = {}, {}
    for c, rs in enumerate(row_slices):
        for hd, sl in enumerate(qk_slices):
            s = lax.dot_general(qb_ref[rs, sl], kb_ref[rs, sl], (((1,), (1,)), ((), ())),
                                preferred_element_type=F32)
            scores[c, hd] = (s * dintra_ref[hd]).astype(BF16)
            kv[c, hd] = lax.dot_general(kd_ref[rs, sl], vb_ref[rs, v_slices[hd]], (((0,), (0,)), ((), ())),
                                        preferred_element_type=F32)

    u_ref[...] = _gelu_tanh(proj(o_u, d))

    for hd, (sl, vsl) in enumerate(zip(qk_slices, v_slices)):
        state = state_ref[hd]
        ret = []
        for c, rs in enumerate(row_slices):
            lhs = jnp.concatenate([scores[c, hd], qd_ref[rs, sl]], axis=1)
            rhs = jnp.concatenate([vb_ref[rs, vsl], state.astype(BF16)], axis=0)
            ret.append(_dot(lhs, rhs))
            state = dchunk_ref[hd] * state + kv[c, hd]
        state_ref[hd] = state
        half_g = 0.5 * proj(o_g + hd * RET_V_DIM, RET_V_DIM)
        silu_g = half_g + half_g * jnp.tanh(half_g)
        for c, rs in enumerate(row_slices):
            y_ref[0, rs, vsl] = (silu_g[rs, :] * _rms(ret[c])).astype(BF16)

    p = proj(o_p, d)
    ext = jnp.concatenate([ptail_ref[...], p], axis=0)
    ptail_ref[...] = p[rows - POOL_HISTORY:, :]
    pos = seq_step * rows + lax.broadcasted_iota(jnp.int32, (rows, pdim), 0)
    pooled = []
    for gi, window in enumerate(POOL_WINDOWS):
        psl = slice(gi * pdim, (gi + 1) * pdim)
        acc = ext[:, psl]
        span = 1
        while span < window:
            acc = acc + pltpu.roll(acc, span, 0)
            span *= 2
        count = jnp.minimum(pos + 1, window).astype(F32)
        pooled.append((acc[POOL_HISTORY:, :] / count - p[:, psl]).astype(BF16))

    tri = lax.broadcasted_iota(jnp.int32, (CHUNK, CHUNK), 0) >= lax.broadcasted_iota(jnp.int32, (CHUNK, CHUNK), 1)
    for gi in range(GMLP_GROUPS):
        gsl = slice(gi * gdim, (gi + 1) * gdim)
        w_tri = jnp.where(tri, ws_ref[gi], 0.0).astype(BF16)
        bias = jnp.broadcast_to(bst_ref[:, gi:gi + 1], (CHUNK, gdim))
        for rs in row_slices:
            mixed = _dot(w_tri, vn_ref[rs, gsl]) + bias
            y_ref[1, rs, gsl] = (u_ref[rs, gsl] * mixed).astype(BF16)

    for gi in range(len(POOL_WINDOWS)):
        psl = slice(gi * pdim, (gi + 1) * pdim)
        mixed = _dot(pooled[gi], _unpack_rows(wpool_ref[gi])) + prm_ref[2:3, psl]
        y_ref[2, :, psl] = (mixed * prm_ref[3:4, psl]).astype(BF16)

    merge_branches()
    o_ref[...] = x + gt1 * _dot(m_ref[...], _unpack_rows(wout_ref[...]))


def _mixer(layer, x, mod, prm, rope_t, w_in, w_branch, w_out, w_pool, ws, bs_t, dintra, dqk, dchunk, cast_jobs):
    b, s, d = x.shape
    rows = MIXER_ROWS
    steps = s // rows
    d_in = w_in.shape[1]
    v_w = RET_HEADS * RET_V_DIM
    qk_w = RET_HEADS * RET_QK_DIM
    row_spec = lambda width: pl.BlockSpec((None, rows, width), lambda i, j: (i, j, 0))
    lres = functools.partial(_layer_resident, layer)
    in_specs = [
        pl.BlockSpec(memory_space=pltpu.SMEM),
        row_spec(d),
        lres(mod.shape[1:]),
        lres(prm.shape[1:]),
        row_spec(2 * RET_QK_DIM),
        _resident(w_in.shape), _resident(w_branch.shape), _resident(w_out.shape),
        _resident(w_pool.shape), lres(ws.shape[1:]), lres(bs_t.shape[1:]),
        _resident(dintra.shape), _resident(dqk.shape),
    ]
    assert len(in_specs) == N_MIXER_INPUTS
    cast_in, cast_out, cast_shapes, cast_bytes = _round_block_specs(cast_jobs, b * steps, lambda i, j: i * steps + j)
    in_specs += cast_in
    out_specs = [row_spec(d)] + cast_out
    out_shape = [jax.ShapeDtypeStruct((b, s, d), F32)] + cast_shapes
    scratch = [
        pltpu.VMEM((RET_HEADS, RET_QK_DIM, RET_V_DIM), F32),
        pltpu.VMEM((POOL_HISTORY, d), F32),
        pltpu.VMEM((rows, qk_w), BF16), pltpu.VMEM((rows, qk_w), BF16),
        pltpu.VMEM((rows, qk_w), BF16), pltpu.VMEM((rows, qk_w), BF16),
        pltpu.VMEM((rows, v_w), BF16),
        pltpu.VMEM((rows, d), F32),
        pltpu.VMEM((rows, d), BF16),
        pltpu.VMEM((N_BRANCH, rows, d), BF16),
        pltpu.VMEM((rows, d), BF16),
    ]
    resident = 4 * (w_in.size + w_branch.size + w_out.size + w_pool.size) + 4 * (
        ws[0].size + bs_t[0].size + dintra.size + dqk.size + mod[0].size + prm[0].size)
    streamed = 4 * rows * (2 * d + 2 * RET_QK_DIM) + cast_bytes
    scratch_bytes = 4 * (RET_HEADS * RET_QK_DIM * RET_V_DIM + POOL_HISTORY * d) + rows * (
        2 * 4 * qk_w + 2 * v_w + 4 * d + 2 * d + 2 * N_BRANCH * d + 2 * d)
    temps = 8 * 4 * rows * d
    return pl.pallas_call(
        functools.partial(_mixer_kernel, n_cast=len(cast_jobs)),
        grid=(b, steps),
        in_specs=in_specs,
        out_specs=out_specs,
        out_shape=out_shape,
        scratch_shapes=scratch,
        compiler_params=pltpu.CompilerParams(
            dimension_semantics=("arbitrary", "arbitrary"),
            vmem_limit_bytes=_vmem_limit(resident, streamed, scratch_bytes, temps)),
        name="token_mixer",
    )(dchunk, x, mod, prm, rope_t, w_in, w_branch, w_out, w_pool, ws, bs_t, dintra, dqk,
      *[w for w, _ in cast_jobs])


def _ffn_kernel(x_ref, mod_ref, n2_ref, w1_ref, w2_ref, fn_ref, o_ref, hid_ref, *, final):
    x = x_ref[...]
    batch_row = pl.ds(pl.program_id(0), 1)
    sh2, sc2, gt2 = mod_ref[3, batch_row, :], mod_ref[4, batch_row, :], mod_ref[5, batch_row, :]
    h = (_rms(x) * (n2_ref[...] * (1.0 + sc2)) + sh2).astype(BF16)
    d_ff = w1_ref.shape[1]
    for lo in range(0, d_ff, FF_BLOCK):
        hid = _dot(h, _unpack_rows(w1_ref[:, lo:lo + FF_BLOCK]))
        hid_ref[:, lo:lo + FF_BLOCK] = jnp.square(jnp.maximum(hid, 0.0)).astype(BF16)
    y = x + gt2 * _dot(hid_ref[...], _unpack_rows(w2_ref[...]))
    if final:
        y = _rms(y) * fn_ref[...]
    o_ref[...] = y


def _ffn(layer, x, mod, norm2, w1, w2, final_norm, final):
    b, s, d = x.shape
    rows = FFN_ROWS
    d_ff = w1.shape[1]
    row_spec = pl.BlockSpec((None, rows, d), lambda i, j: (i, j, 0))
    lres = functools.partial(_layer_resident, layer)
    resident = 2 * 2 * d * d_ff + 4 * 2 * d + 4 * mod[0].size
    streamed = 4 * 2 * rows * d
    temps = rows * (2 * d + 6 * FF_BLOCK + 3 * 4 * d)
    return pl.pallas_call(
        functools.partial(_ffn_kernel, final=final),
        grid=(b, s // rows),
        in_specs=[
            row_spec,
            lres(mod.shape[1:]),
            lres((1, d)), _resident(w1.shape), _resident(w2.shape), _resident((1, d)),
        ],
        out_specs=row_spec,
        out_shape=jax.ShapeDtypeStruct((b, s, d), F32),
        scratch_shapes=[pltpu.VMEM((rows, d_ff), BF16)],
        compiler_params=pltpu.CompilerParams(
            dimension_semantics=("arbitrary", "arbitrary"),
            vmem_limit_bytes=_vmem_limit(resident, streamed, 2 * rows * d_ff, temps)),
        name="channel_mlp",
    )(x, mod, norm2, w1, w2, final_norm)


def _retention_decay_tables(rows):
    log_gamma = jnp.log1p(-jnp.power(2.0, -5.0 - jnp.arange(RET_HEADS, dtype=F32)))
    pos = jnp.arange(CHUNK, dtype=F32)
    rel = pos[:, None] - pos[None, :]
    causal = rel >= 0
    intra = jnp.where(causal[None], jnp.exp(log_gamma[:, None, None] * jnp.where(causal, rel, 0.0)[None]), 0.0)
    decay_q = jnp.exp(log_gamma[:, None] * (pos + 1.0)[None])
    decay_k = jnp.exp(log_gamma[:, None] * (CHUNK - 1.0 - pos)[None])
    decay_chunk = jnp.exp(log_gamma * CHUNK)
    reps = rows // CHUNK
    widen = lambda t: jnp.broadcast_to(jnp.tile(t, (1, reps))[:, :, None], (RET_HEADS, rows, RET_QK_DIM))
    return intra, jnp.stack([widen(decay_q), widen(decay_k)]), decay_chunk


def kernel(x, c, positions, w_ada, b_ada, norm1, norm2, w_in, ws_gmlp, bs_gmlp, vnorm_gmlp, w_pool, b_pool,
           pool_scale, w_branch, w_out, w_ff1, w_ff2, final_norm):
    depth = w_in.shape[0]
    b, s, d = x.shape
    assert s % MIXER_ROWS == 0 and s % FFN_ROWS == 0 and s % ROPE_ROWS == 0 and MIXER_ROWS % CHUNK == 0
    assert w_ff1.shape[2] % FF_BLOCK == 0

    mod = _ada_rows(c, w_ada, b_ada)
    dintra, dqk, dchunk = _retention_decay_tables(MIXER_ROWS)
    rows_of = lambda v: v.reshape(depth, 1, d)
    prm = jnp.concatenate([rows_of(v) for v in (norm1, vnorm_gmlp, b_pool, pool_scale)], axis=1)
    norm2 = rows_of(norm2)
    bs_t = jnp.swapaxes(bs_gmlp, 1, 2)

    mixer_w = (w_in, w_branch, w_out, w_pool)
    as_rows = lambda w: w.reshape(depth, -1, w.shape[-1])
    rope_t, *next_mixer = _rope_tables(positions, [(as_rows(w), 0) for w in mixer_w])
    for l in range(depth):
        mixer_bf16 = [w.reshape(ref.shape[1:-2] + (ref.shape[-2] // ROWS_PER_WORD, ref.shape[-1]))
                      for w, ref in zip(next_mixer, mixer_w)]
        jobs = [(as_rows(w_ff1), l), (as_rows(w_ff2), l)]
        if l + 1 < depth:
            jobs += [(as_rows(w), l + 1) for w in mixer_w]
        x, ff1, ff2, *next_mixer = _mixer(l, x, mod, prm, rope_t, *mixer_bf16, ws_gmlp, bs_t,
                                          dintra, dqk, dchunk, jobs)
        x = _ffn(l, x, mod, norm2, ff1, ff2, final_norm.reshape(1, d), final=(l == depth - 1))
    return x
```

```python
import functools

import jax
import jax.numpy as jnp
from jax import lax
from jax.experimental import pallas as pl
from jax.experimental.pallas import tpu as pltpu

CHUNK = 128
RET_HEADS = 4
RET_QK_DIM = 128
RET_V_DIM = 256
GMLP_GROUPS = 4
POOL_WINDOWS = (2, 4, 8, 16)
POOL_HISTORY = 16
N_BRANCH = 3
N_MOD = 6
ROPE_BASE = 10000.0
EPS = 1e-6

VMEM_BYTES_V7X = 64 * 1024 * 1024
VMEM_REQUEST_CAP_V7X = VMEM_BYTES_V7X - 4 * 1024 * 1024
BF16_ROW_TILE = 16

MIXER_ROWS = 256
FFN_ROWS = 1024
FF_BLOCK = 1024
GATE_BLOCK = 256
ROPE_ROWS = 2048

F32 = jnp.float32
BF16 = jnp.bfloat16


def _dot(a, b):
    return jnp.dot(a, b, preferred_element_type=F32)


def _rms(x):
    return x * lax.rsqrt(jnp.mean(x * x, axis=-1, keepdims=True) + EPS)


def _resident(shape):
    zeros = (0,) * len(shape)
    return pl.BlockSpec(shape, lambda *_: zeros, pipeline_mode=pl.Buffered(1))


def _layer_resident(layer, shape):
    index = (layer,) + (0,) * len(shape)
    return pl.BlockSpec((None,) + tuple(shape), lambda *_: index, pipeline_mode=pl.Buffered(1))


def _vmem_limit(resident_bytes, streamed_bytes, scratch_bytes, temp_bytes):
    need = resident_bytes + 2 * streamed_bytes + scratch_bytes + temp_bytes
    return min(int(need), VMEM_REQUEST_CAP_V7X)


def _ada_kernel(c_ref, w_ref, b_ref, o_ref):
    c = c_ref[...]
    c_act = (c * jax.nn.sigmoid(c)).astype(BF16)
    o_ref[...] = _dot(c_act, w_ref[...].astype(BF16)) + b_ref[...]


def _ada_rows(c, w_ada, b_ada):
    depth, d, n = w_ada.shape
    b = c.shape[0]
    assert n == N_MOD * d
    return pl.pallas_call(
        _ada_kernel,
        grid=(depth, N_MOD),
        in_specs=[
            pl.BlockSpec((b, d), lambda l, j: (0, 0)),
            pl.BlockSpec((None, d, d), lambda l, j: (l, 0, j)),
            pl.BlockSpec((None, 1, d), lambda l, j: (l, 0, j)),
        ],
        out_specs=pl.BlockSpec((None, None, b, d), lambda l, j: (l, j, 0, 0)),
        out_shape=jax.ShapeDtypeStruct((depth, N_MOD, b, d), F32),
        compiler_params=pltpu.CompilerParams(
            dimension_semantics=("arbitrary", "arbitrary")),
        name="ada_rows",
    )(c, w_ada, b_ada.reshape(depth, 1, n))


PACKED = jnp.int32
ROWS_PER_WORD = 2


def _pack_rows(w_bf16):
    return pltpu.bitcast(w_bf16, PACKED)


def _unpack_rows(w_packed):
    return pltpu.bitcast(w_packed, BF16)


def _round_blocks(cast_in, cast_out):
    for src_ref, dst_ref in zip(cast_in, cast_out):
        dst_ref[...] = _pack_rows(src_ref[...].astype(BF16))


def _round_block_specs(cast_jobs, n_steps, flat_step):
    in_specs, out_specs, out_shapes, nbytes = [], [], [], 0
    for w, layer in cast_jobs:
        _, r, c = w.shape
        blk = r // n_steps
        assert blk * n_steps == r and blk % BF16_ROW_TILE == 0
        in_specs.append(pl.BlockSpec((None, blk, c), lambda i, j, layer=layer: (layer, flat_step(i, j), 0)))
        out_specs.append(pl.BlockSpec((blk // ROWS_PER_WORD, c), lambda i, j: (flat_step(i, j), 0)))
        out_shapes.append(jax.ShapeDtypeStruct((r // ROWS_PER_WORD, c), PACKED))
        nbytes += blk * c * (4 + 2)
    return in_specs, out_specs, out_shapes, nbytes


N_ROPE_INPUTS = 3


def _rope_kernel(*refs, n_cast):
    pos_ref, freq_ref, sign_ref = refs[:N_ROPE_INPUTS]
    cast_in = refs[N_ROPE_INPUTS:N_ROPE_INPUTS + n_cast]
    tab_ref = refs[N_ROPE_INPUTS + n_cast]
    cast_out = refs[N_ROPE_INPUTS + n_cast + 1:]
    _round_blocks(cast_in, cast_out)

    rows = pos_ref.shape[0]
    half_rows, half = rows // 2, RET_QK_DIM // 2
    pos = pos_ref[...].astype(F32)
    low = lax.broadcasted_iota(jnp.int32, (half_rows, RET_QK_DIM), 1) < half
    ang = jnp.where(low, pos[:half_rows], pos[half_rows:]) * freq_ref[...]
    cos_p, sin_p = jnp.cos(ang), jnp.sin(ang)
    cos_s, sin_s = pltpu.roll(cos_p, half, 1), pltpu.roll(sin_p, half, 1)
    sign = sign_ref[...]
    tab_ref[:half_rows, :RET_QK_DIM] = jnp.where(low, cos_p, cos_s)
    tab_ref[half_rows:, :RET_QK_DIM] = jnp.where(low, cos_s, cos_p)
    tab_ref[:half_rows, RET_QK_DIM:] = jnp.where(low, sin_p, sin_s) * sign
    tab_ref[half_rows:, RET_QK_DIM:] = jnp.where(low, sin_s, sin_p) * sign


def _rope_tables(positions, cast_jobs):
    b, s = positions.shape
    half = RET_QK_DIM // 2
    inv_freq = ROPE_BASE ** (-jnp.arange(half, dtype=F32) / half)
    freq = jnp.concatenate([inv_freq, inv_freq]).reshape(1, RET_QK_DIM)
    sign = jnp.concatenate([-jnp.ones((half,), F32), jnp.ones((half,), F32)]).reshape(1, RET_QK_DIM)
    rows = ROPE_ROWS
    steps = s // rows
    cast_in, cast_out, cast_shapes, cast_bytes = _round_block_specs(cast_jobs, b * steps, lambda i, j: i * steps + j)
    table = jax.ShapeDtypeStruct((b, s, 2 * RET_QK_DIM), F32)
    table_spec = pl.BlockSpec((None, rows, 2 * RET_QK_DIM), lambda i, j: (i, j, 0))
    return pl.pallas_call(
        functools.partial(_rope_kernel, n_cast=len(cast_jobs)),
        grid=(b, steps),
        in_specs=[
            pl.BlockSpec((None, rows, 1), lambda i, j: (i, j, 0)),
            pl.BlockSpec((1, RET_QK_DIM), lambda i, j: (0, 0)),
            pl.BlockSpec((1, RET_QK_DIM), lambda i, j: (0, 0)),
        ] + cast_in,
        out_specs=[table_spec] + cast_out,
        out_shape=[table] + cast_shapes,
        compiler_params=pltpu.CompilerParams(
            dimension_semantics=("arbitrary", "arbitrary"),
            vmem_limit_bytes=_vmem_limit(0, cast_bytes + 4 * rows * 3 * RET_QK_DIM, 0, 16 * 4 * rows * RET_QK_DIM)),
        name="rope_tables",
    )(positions.reshape(b, s, 1), freq, sign, *[w for w, _ in cast_jobs])


GELU_C0 = 0.7978845608028654
GELU_C1 = GELU_C0 * 0.044715


def _sigmoid(x):
    return 0.5 * jnp.tanh(0.5 * x) + 0.5


def _gelu_tanh(x):
    half_x = 0.5 * x
    return half_x + half_x * jnp.tanh(x * (GELU_C0 + GELU_C1 * (x * x)))


N_MIXER_INPUTS = 13


def _mixer_kernel(*refs, n_cast):
    ins = refs[:N_MIXER_INPUTS]
    cast_in = refs[N_MIXER_INPUTS:N_MIXER_INPUTS + n_cast]
    o_ref = refs[N_MIXER_INPUTS + n_cast]
    cast_out = refs[N_MIXER_INPUTS + n_cast + 1:N_MIXER_INPUTS + 2 * n_cast + 1]
    scratch = refs[N_MIXER_INPUTS + 2 * n_cast + 1:]
    _mixer_step(*ins, o_ref, *scratch)
    _round_blocks(cast_in, cast_out)


def _mixer_step(dchunk_ref, x_ref, mod_ref, prm_ref, rope_ref, win_ref, wbr_ref, wout_ref,
                wpool_ref, ws_ref, bst_ref, dintra_ref, dqk_ref,
                o_ref,
                state_ref, ptail_ref, qb_ref, qd_ref, kb_ref, kd_ref, vb_ref, u_ref, vn_ref, y_ref, m_ref):
    rows, d = x_ref.shape
    n_chunks = rows // CHUNK
    qk_w = RET_HEADS * RET_QK_DIM
    v_w = RET_HEADS * RET_V_DIM
    gdim = d // GMLP_GROUPS
    pdim = d // len(POOL_WINDOWS)
    o_q, o_k, o_v = 0, qk_w, 2 * qk_w
    o_g = o_v + v_w
    o_u = o_g + v_w
    o_vs = o_u + d
    o_p = o_vs + d
    o_gate = o_p + d
    seq_step = pl.program_id(1)
    row_slices = [slice(c * CHUNK, (c + 1) * CHUNK) for c in range(n_chunks)]
    qk_slices = [slice(hd * RET_QK_DIM, (hd + 1) * RET_QK_DIM) for hd in range(RET_HEADS)]
    v_slices = [slice(hd * RET_V_DIM, (hd + 1) * RET_V_DIM) for hd in range(RET_HEADS)]

    @pl.when(seq_step == 0)
    def _():
        state_ref[...] = jnp.zeros_like(state_ref)
        ptail_ref[...] = jnp.zeros_like(ptail_ref)

    x = x_ref[...]
    batch_row = pl.ds(pl.program_id(0), 1)
    sh1, sc1, gt1 = mod_ref[0, batch_row, :], mod_ref[1, batch_row, :], mod_ref[2, batch_row, :]
    h = (_rms(x) * (prm_ref[0:1, :] * (1.0 + sc1)) + sh1).astype(BF16)

    def proj(lo, width):
        return _dot(h, _unpack_rows(win_ref[:, lo:lo + width]))

    def add_gated_branch(n, first=False):
        for lo in range(0, d, GATE_BLOCK):
            cols = slice(lo, lo + GATE_BLOCK)
            gated = _sigmoid(proj(o_gate + n * d + lo, GATE_BLOCK)) * _dot(y_ref[n], _unpack_rows(wbr_ref[n, :, cols]))
            m_ref[:, cols] = gated if first else m_ref[:, cols] + gated

    cosv, sinv = rope_ref[:, :RET_QK_DIM], rope_ref[:, RET_QK_DIM:]
    qf, kf = proj(o_q, qk_w), proj(o_k, qk_w)
    k_scale = RET_QK_DIM ** -0.5
    for hd, sl in enumerate(qk_slices):
        q_h, k_h = qf[:, sl], kf[:, sl]
        q_r = q_h * cosv + pltpu.roll(q_h, RET_QK_DIM // 2, 1) * sinv
        k_r = (k_h * cosv + pltpu.roll(k_h, RET_QK_DIM // 2, 1) * sinv) * k_scale
        qb_ref[:, sl] = q_r.astype(BF16)
        qd_ref[:, sl] = (q_r * dqk_ref[0, hd]).astype(BF16)
        kb_ref[:, sl] = k_r.astype(BF16)
        kd_ref[:, sl] = (k_r * dqk_ref[1, hd]).astype(BF16)
    vb_ref[...] = proj(o_v, v_w).astype(BF16)
    vs = _gelu_tanh(proj(o_vs, d))
    vn_ref[...] = (_rms(vs) * prm_ref[1:2, :]).astype(BF16)

    u_ref[...] = _gelu_tanh(proj(o_u, d))

    scores, kv = {}, {}
    for c, rs in enumerate(row_slices):
        for hd, sl in enumerate(qk_slices):
            s = lax.dot_general(qb_ref[rs, sl], kb_ref[rs, sl], (((1,), (1,)), ((), ())),
                                preferred_element_type=F32)
            scores[c, hd] = (s * dintra_ref[hd]).astype(BF16)
            kv[c, hd] = lax.dot_general(kd_ref[rs, sl], vb_ref[rs, v_slices[hd]], (((0,), (0,)), ((), ())),
                                        preferred_element_type=F32)

    for hd, (sl, vsl) in enumerate(zip(qk_slices, v_slices)):
        state = state_ref[hd]
        ret = []
        for c, rs in enumerate(row_slices):
            lhs = jnp.concatenate([scores[c, hd], qd_ref[rs, sl]], axis=1)
            rhs = jnp.concatenate([vb_ref[rs, vsl], state.astype(BF16)], axis=0)
            ret.append(_dot(lhs, rhs))
            state = dchunk_ref[hd] * state + kv[c, hd]
        state_ref[hd] = state
        half_g = 0.5 * proj(o_g + hd * RET_V_DIM, RET_V_DIM)
        silu_g = half_g + half_g * jnp.tanh(half_g)
        for c, rs in enumerate(row_slices):
            y_ref[0, rs, vsl] = (silu_g[rs, :] * _rms(ret[c])).astype(BF16)

    p = proj(o_p, d)
    ext = jnp.concatenate([ptail_ref[...], p], axis=0)
    ptail_ref[...] = p[rows - POOL_HISTORY:, :]
    pos = seq_step * rows + lax.broadcasted_iota(jnp.int32, (rows, pdim), 0)
    pooled = []
    for gi, window in enumerate(POOL_WINDOWS):
        psl = slice(gi * pdim, (gi + 1) * pdim)
        acc = ext[:, psl]
        span = 1
        while span < window:
            acc = acc + pltpu.roll(acc, span, 0)
            span *= 2
        count = jnp.minimum(pos + 1, window).astype(F32)
        pooled.append((acc[POOL_HISTORY:, :] / count - p[:, psl]).astype(BF16))

    tri = lax.broadcasted_iota(jnp.int32, (CHUNK, CHUNK), 0) >= lax.broadcasted_iota(jnp.int32, (CHUNK, CHUNK), 1)
    for gi in range(GMLP_GROUPS):
        gsl = slice(gi * gdim, (gi + 1) * gdim)
        w_tri = jnp.where(tri, ws_ref[gi], 0.0).astype(BF16)
        bias = jnp.broadcast_to(bst_ref[:, gi:gi + 1], (CHUNK, gdim))
        for rs in row_slices:
            mixed = _dot(w_tri, vn_ref[rs, gsl]) + bias
            y_ref[1, rs, gsl] = (u_ref[rs, gsl] * mixed).astype(BF16)

    add_gated_branch(1, first=True)

    for gi in range(len(POOL_WINDOWS)):
        psl = slice(gi * pdim, (gi + 1) * pdim)
        mixed = _dot(pooled[gi], _unpack_rows(wpool_ref[gi])) + prm_ref[2:3, psl]
        y_ref[2, :, psl] = (mixed * prm_ref[3:4, psl]).astype(BF16)

    add_gated_branch(0)
    add_gated_branch(2)
    o_ref[...] = x + gt1 * _dot(m_ref[...].astype(BF16), _unpack_rows(wout_ref[...]))


def _mixer(layer, x, mod, prm, rope_t, w_in, w_branch, w_out, w_pool, ws, bs_t, dintra, dqk, dchunk, cast_jobs):
    b, s, d = x.shape
    rows = MIXER_ROWS
    steps = s // rows
    d_in = w_in.shape[1]
    v_w = RET_HEADS * RET_V_DIM
    qk_w = RET_HEADS * RET_QK_DIM
    row_spec = lambda width: pl.BlockSpec((None, rows, width), lambda i, j: (i, j, 0))
    lres = functools.partial(_layer_resident, layer)
    in_specs = [
        pl.BlockSpec(memory_space=pltpu.SMEM),
        row_spec(d),
        lres(mod.shape[1:]),
        lres(prm.shape[1:]),
        row_spec(2 * RET_QK_DIM),
        _resident(w_in.shape), _resident(w_branch.shape), _resident(w_out.shape),
        _resident(w_pool.shape), lres(ws.shape[1:]), lres(bs_t.shape[1:]),
        _resident(dintra.shape), _resident(dqk.shape),
    ]
    assert len(in_specs) == N_MIXER_INPUTS
    cast_in, cast_out, cast_shapes, cast_bytes = _round_block_specs(cast_jobs, b * steps, lambda i, j: i * steps + j)
    in_specs += cast_in
    out_specs = [row_spec(d)] + cast_out
    out_shape = [jax.ShapeDtypeStruct((b, s, d), F32)] + cast_shapes
    scratch = [
        pltpu.VMEM((RET_HEADS, RET_QK_DIM, RET_V_DIM), F32),
        pltpu.VMEM((POOL_HISTORY, d), F32),
        pltpu.VMEM((rows, qk_w), BF16), pltpu.VMEM((rows, qk_w), BF16),
        pltpu.VMEM((rows, qk_w), BF16), pltpu.VMEM((rows, qk_w), BF16),
        pltpu.VMEM((rows, v_w), BF16),
        pltpu.VMEM((rows, d), F32),
        pltpu.VMEM((rows, d), BF16),
        pltpu.VMEM((N_BRANCH, rows, d), BF16),
        pltpu.VMEM((rows, d), F32),
    ]
    resident = 4 * (w_in.size + w_branch.size + w_out.size + w_pool.size) + 4 * (
        ws[0].size + bs_t[0].size + dintra.size + dqk.size + mod[0].size + prm[0].size)
    streamed = 4 * rows * (2 * d + 2 * RET_QK_DIM) + cast_bytes
    scratch_bytes = 4 * (RET_HEADS * RET_QK_DIM * RET_V_DIM + POOL_HISTORY * d) + rows * (
        2 * 4 * qk_w + 2 * v_w + 4 * d + 2 * d + 2 * N_BRANCH * d + 4 * d)
    temps = 8 * 4 * rows * d
    return pl.pallas_call(
        functools.partial(_mixer_kernel, n_cast=len(cast_jobs)),
        grid=(b, steps),
        in_specs=in_specs,
        out_specs=out_specs,
        out_shape=out_shape,
        scratch_shapes=scratch,
        compiler_params=pltpu.CompilerParams(
            dimension_semantics=("arbitrary", "arbitrary"),
            vmem_limit_bytes=_vmem_limit(resident, streamed, scratch_bytes, temps)),
        name="token_mixer",
    )(dchunk, x, mod, prm, rope_t, w_in, w_branch, w_out, w_pool, ws, bs_t, dintra, dqk,
      *[w for w, _ in cast_jobs])


def _ffn_kernel(x_ref, mod_ref, n2_ref, w1_ref, w2_ref, fn_ref, o_ref, hid_ref, *, final):
    x = x_ref[...]
    batch_row = pl.ds(pl.program_id(0), 1)
    sh2, sc2, gt2 = mod_ref[3, batch_row, :], mod_ref[4, batch_row, :], mod_ref[5, batch_row, :]
    h = (_rms(x) * (n2_ref[...] * (1.0 + sc2)) + sh2).astype(BF16)
    d_ff = w1_ref.shape[1]
    for lo in range(0, d_ff, FF_BLOCK):
        hid = _dot(h, _unpack_rows(w1_ref[:, lo:lo + FF_BLOCK]))
        hid_ref[:, lo:lo + FF_BLOCK] = jnp.square(jnp.maximum(hid, 0.0)).astype(BF16)
    y = x + gt2 * _dot(hid_ref[...], _unpack_rows(w2_ref[...]))
    if final:
        y = _rms(y) * fn_ref[...]
    o_ref[...] = y


def _ffn(layer, x, mod, norm2, w1, w2, final_norm, final):
    b, s, d = x.shape
    rows = FFN_ROWS
    d_ff = w1.shape[1]
    row_spec = pl.BlockSpec((None, rows, d), lambda i, j: (i, j, 0))
    lres = functools.partial(_layer_resident, layer)
    resident = 2 * 2 * d * d_ff + 4 * 2 * d + 4 * mod[0].size
    streamed = 4 * 2 * rows * d
    temps = rows * (2 * d + 6 * FF_BLOCK + 3 * 4 * d)
    return pl.pallas_call(
        functools.partial(_ffn_kernel, final=final),
        grid=(b, s // rows),
        in_specs=[
            row_spec,
            lres(mod.shape[1:]),
            lres((1, d)), _resident(w1.shape), _resident(w2.shape), _resident((1, d)),
        ],
        out_specs=row_spec,
        out_shape=jax.ShapeDtypeStruct((b, s, d), F32),
        scratch_shapes=[pltpu.VMEM((rows, d_ff), BF16)],
        compiler_params=pltpu.CompilerParams(
            dimension_semantics=("arbitrary", "arbitrary"),
            vmem_limit_bytes=_vmem_limit(resident, streamed, 2 * rows * d_ff, temps)),
        name="channel_mlp",
    )(x, mod, norm2, w1, w2, final_norm)


def _retention_decay_tables(rows):
    log_gamma = jnp.log1p(-jnp.power(2.0, -5.0 - jnp.arange(RET_HEADS, dtype=F32)))
    pos = jnp.arange(CHUNK, dtype=F32)
    rel = pos[:, None] - pos[None, :]
    causal = rel >= 0
    intra = jnp.where(causal[None], jnp.exp(log_gamma[:, None, None] * jnp.where(causal, rel, 0.0)[None]), 0.0)
    decay_q = jnp.exp(log_gamma[:, None] * (pos + 1.0)[None])
    decay_k = jnp.exp(log_gamma[:, None] * (CHUNK - 1.0 - pos)[None])
    decay_chunk = jnp.exp(log_gamma * CHUNK)
    reps = rows // CHUNK
    widen = lambda t: jnp.broadcast_to(jnp.tile(t, (1, reps))[:, :, None], (RET_HEADS, rows, RET_QK_DIM))
    return intra, jnp.stack([widen(decay_q), widen(decay_k)]), decay_chunk


def kernel(x, c, positions, w_ada, b_ada, norm1, norm2, w_in, ws_gmlp, bs_gmlp, vnorm_gmlp, w_pool, b_pool,
           pool_scale, w_branch, w_out, w_ff1, w_ff2, final_norm):
    depth = w_in.shape[0]
    b, s, d = x.shape
    assert s % MIXER_ROWS == 0 and s % FFN_ROWS == 0 and s % ROPE_ROWS == 0 and MIXER_ROWS % CHUNK == 0
    assert w_ff1.shape[2] % FF_BLOCK == 0

    mod = _ada_rows(c, w_ada, b_ada)
    dintra, dqk, dchunk = _retention_decay_tables(MIXER_ROWS)
    rows_of = lambda v: v.reshape(depth, 1, d)
    prm = jnp.concatenate([rows_of(v) for v in (norm1, vnorm_gmlp, b_pool, pool_scale)], axis=1)
    norm2 = rows_of(norm2)
    bs_t = jnp.swapaxes(bs_gmlp, 1, 2)

    mixer_w = (w_in, w_branch, w_out, w_pool)
    as_rows = lambda w: w.reshape(depth, -1, w.shape[-1])
    rope_t, *next_mixer = _rope_tables(positions, [(as_rows(w), 0) for w in mixer_w])
    for l in range(depth):
        mixer_bf16 = [w.reshape(ref.shape[1:-2] + (ref.shape[-2] // ROWS_PER_WORD, ref.shape[-1]))
                      for w, ref in zip(next_mixer, mixer_w)]
        jobs = [(as_rows(w_ff1), l), (as_rows(w_ff2), l)]
        if l + 1 < depth:
            jobs += [(as_rows(w), l + 1) for w in mixer_w]
        x, ff1, ff2, *next_mixer = _mixer(l, x, mod, prm, rope_t, *mixer_bf16, ws_gmlp, bs_t,
                                          dintra, dqk, dchunk, jobs)
        x = _ffn(l, x, mod, norm2, ff1, ff2, final_norm.reshape(1, d), final=(l == depth - 1))
    return x
```

```python
import functools

import jax
import jax.numpy as jnp
from jax import lax
from jax.experimental import pallas as pl
from jax.experimental.pallas import tpu as pltpu

CHUNK = 128
RET_HEADS = 4
RET_QK_DIM = 128
RET_V_DIM = 256
GMLP_GROUPS = 4
POOL_WINDOWS = (2, 4, 8, 16)
POOL_HISTORY = 16
N_BRANCH = 3
N_MOD = 6
ROPE_BASE = 10000.0
EPS = 1e-6

VMEM_BYTES_V7X = 64 * 1024 * 1024
VMEM_REQUEST_CAP_V7X = VMEM_BYTES_V7X - 4 * 1024 * 1024
BF16_ROW_TILE = 16

MIXER_ROWS = 256
FFN_ROWS = 1024
FF_BLOCK = 1024
GATE_BLOCK = 256
PREP_STEPS = 16

F32 = jnp.float32
BF16 = jnp.bfloat16


def _dot(a, b):
    return jnp.dot(a, b, preferred_element_type=F32)


def _rms(x):
    return x * lax.rsqrt(jnp.mean(x * x, axis=-1, keepdims=True) + EPS)


def _resident(shape):
    zeros = (0,) * len(shape)
    return pl.BlockSpec(shape, lambda *_: zeros, pipeline_mode=pl.Buffered(1))


def _layer_resident(layer, shape):
    index = (layer,) + (0,) * len(shape)
    return pl.BlockSpec((None,) + tuple(shape), lambda *_: index, pipeline_mode=pl.Buffered(1))


def _vmem_limit(resident_bytes, streamed_bytes, scratch_bytes, temp_bytes):
    need = resident_bytes + 2 * streamed_bytes + scratch_bytes + temp_bytes
    return min(int(need), VMEM_REQUEST_CAP_V7X)


PACKED = jnp.int32
ROWS_PER_WORD = 2


def _pack_rows(w_bf16):
    return pltpu.bitcast(w_bf16, PACKED)


def _unpack_rows(w_packed):
    return pltpu.bitcast(w_packed, BF16)


def _round_blocks(cast_in, cast_out):
    for src_ref, dst_ref in zip(cast_in, cast_out):
        dst_ref[...] = _pack_rows(src_ref[...].astype(BF16))


def _round_block_specs(cast_jobs, n_steps, flat_step):
    in_specs, out_specs, out_shapes, nbytes = [], [], [], 0
    for w, layer in cast_jobs:
        _, r, c = w.shape
        blk = r // n_steps
        assert blk * n_steps == r and blk % BF16_ROW_TILE == 0
        in_specs.append(pl.BlockSpec((None, blk, c), lambda *idx, layer=layer: (layer, flat_step(*idx), 0)))
        out_specs.append(pl.BlockSpec((blk // ROWS_PER_WORD, c), lambda *idx: (flat_step(*idx), 0)))
        out_shapes.append(jax.ShapeDtypeStruct((r // ROWS_PER_WORD, c), PACKED))
        nbytes += blk * c * (4 + 2)
    return in_specs, out_specs, out_shapes, nbytes


N_PREP_INPUTS = 6


def _prep_kernel(*refs, n_cast):
    c_ref, wada_ref, bada_ref, pos_ref, freq_ref, sign_ref = refs[:N_PREP_INPUTS]
    cast_in = refs[N_PREP_INPUTS:N_PREP_INPUTS + n_cast]
    mod_ref, tab_ref = refs[N_PREP_INPUTS + n_cast:N_PREP_INPUTS + n_cast + 2]
    cast_out = refs[N_PREP_INPUTS + n_cast + 2:]
    _round_blocks(cast_in, cast_out)

    c = c_ref[...]
    c_act = (c * jax.nn.sigmoid(c)).astype(BF16)
    mod_ref[...] = _dot(c_act, wada_ref[...].astype(BF16)) + bada_ref[...]

    rows = pos_ref.shape[0]
    half_rows, half = rows // 2, RET_QK_DIM // 2
    pos = pos_ref[...].astype(F32)
    low = lax.broadcasted_iota(jnp.int32, (half_rows, RET_QK_DIM), 1) < half
    ang = jnp.where(low, pos[:half_rows], pos[half_rows:]) * freq_ref[...]
    cos_p, sin_p = jnp.cos(ang), jnp.sin(ang)
    cos_s, sin_s = pltpu.roll(cos_p, half, 1), pltpu.roll(sin_p, half, 1)
    sign = sign_ref[...]
    tab_ref[:half_rows, :RET_QK_DIM] = jnp.where(low, cos_p, cos_s)
    tab_ref[half_rows:, :RET_QK_DIM] = jnp.where(low, cos_s, cos_p)
    tab_ref[:half_rows, RET_QK_DIM:] = jnp.where(low, sin_p, sin_s) * sign
    tab_ref[half_rows:, RET_QK_DIM:] = jnp.where(low, sin_s, sin_p) * sign


def _prepare(c, w_ada, b_ada, positions, cast_jobs):
    depth, d, n = w_ada.shape
    bsz, s = positions.shape
    steps = PREP_STEPS
    per_layer = steps // depth
    cols, rows = n // per_layer, bsz * s // steps
    assert per_layer * depth == steps and cols * per_layer == n and cols % RET_QK_DIM == 0
    assert rows * steps == bsz * s and rows % (2 * BF16_ROW_TILE) == 0
    half = RET_QK_DIM // 2
    inv_freq = ROPE_BASE ** (-jnp.arange(half, dtype=F32) / half)
    freq = jnp.concatenate([inv_freq, inv_freq]).reshape(1, RET_QK_DIM)
    sign = jnp.concatenate([-jnp.ones((half,), F32), jnp.ones((half,), F32)]).reshape(1, RET_QK_DIM)
    cast_in, cast_out, cast_shapes, cast_bytes = _round_block_specs(cast_jobs, steps, lambda t: t)
    ada_block = lambda lead: pl.BlockSpec((None, lead, cols), lambda t: (t // per_layer, 0, t % per_layer))
    const = lambda shape: pl.BlockSpec(shape, lambda t: (0,) * len(shape))
    streamed = cast_bytes + 4 * (d * cols + cols + bsz * cols + rows * (RET_QK_DIM + 2 * RET_QK_DIM))
    mod, table, *rounded = pl.pallas_call(
        functools.partial(_prep_kernel, n_cast=len(cast_jobs)),
        grid=(steps,),
        in_specs=[
            const((bsz, d)), ada_block(d), ada_block(1),
            pl.BlockSpec((rows, 1), lambda t: (t, 0)), const((1, RET_QK_DIM)), const((1, RET_QK_DIM)),
        ] + cast_in,
        out_specs=[ada_block(bsz), pl.BlockSpec((rows, 2 * RET_QK_DIM), lambda t: (t, 0))] + cast_out,
        out_shape=[jax.ShapeDtypeStruct((depth, bsz, n), F32),
                   jax.ShapeDtypeStruct((bsz * s, 2 * RET_QK_DIM), F32)] + cast_shapes,
        compiler_params=pltpu.CompilerParams(
            dimension_semantics=("arbitrary",),
            vmem_limit_bytes=_vmem_limit(4 * bsz * d, streamed, 0, 2 * d * cols + 16 * 4 * rows * RET_QK_DIM)),
        name="prepare",
    )(c, w_ada, b_ada.reshape(depth, 1, n), positions.reshape(bsz * s, 1), freq, sign, *[w for w, _ in cast_jobs])
    return (mod, table.reshape(bsz, s, 2 * RET_QK_DIM), *rounded)


GELU_C0 = 0.7978845608028654
GELU_C1 = GELU_C0 * 0.044715


def _sigmoid(x):
    return 0.5 * jnp.tanh(0.5 * x) + 0.5


def _gelu_tanh(x):
    half_x = 0.5 * x
    return half_x + half_x * jnp.tanh(x * (GELU_C0 + GELU_C1 * (x * x)))


N_MIXER_INPUTS = 13


def _mixer_kernel(*refs, n_cast):
    ins = refs[:N_MIXER_INPUTS]
    cast_in = refs[N_MIXER_INPUTS:N_MIXER_INPUTS + n_cast]
    o_ref = refs[N_MIXER_INPUTS + n_cast]
    cast_out = refs[N_MIXER_INPUTS + n_cast + 1:N_MIXER_INPUTS + 2 * n_cast + 1]
    scratch = refs[N_MIXER_INPUTS + 2 * n_cast + 1:]
    _mixer_step(*ins, o_ref, *scratch)
    _round_blocks(cast_in, cast_out)


def _mixer_step(dchunk_ref, x_ref, mod_ref, prm_ref, rope_ref, win_ref, wbr_ref, wout_ref,
                wpool_ref, ws_ref, bst_ref, dintra_ref, dqk_ref,
                o_ref,
                state_ref, ptail_ref, qb_ref, qd_ref, kb_ref, kd_ref, vb_ref, u_ref, vn_ref, y_ref, m_ref):
    rows, d = x_ref.shape
    n_chunks = rows // CHUNK
    qk_w = RET_HEADS * RET_QK_DIM
    v_w = RET_HEADS * RET_V_DIM
    gdim = d // GMLP_GROUPS
    pdim = d // len(POOL_WINDOWS)
    o_q, o_k, o_v = 0, qk_w, 2 * qk_w
    o_g = o_v + v_w
    o_u = o_g + v_w
    o_vs = o_u + d
    o_p = o_vs + d
    o_gate = o_p + d
    seq_step = pl.program_id(1)
    row_slices = [slice(c * CHUNK, (c + 1) * CHUNK) for c in range(n_chunks)]
    qk_slices = [slice(hd * RET_QK_DIM, (hd + 1) * RET_QK_DIM) for hd in range(RET_HEADS)]
    v_slices = [slice(hd * RET_V_DIM, (hd + 1) * RET_V_DIM) for hd in range(RET_HEADS)]

    @pl.when(seq_step == 0)
    def _():
        state_ref[...] = jnp.zeros_like(state_ref)
        ptail_ref[...] = jnp.zeros_like(ptail_ref)

    x = x_ref[...]
    batch_row = pl.ds(pl.program_id(0), 1)
    sh1, sc1, gt1 = (mod_ref[batch_row, k * d:(k + 1) * d] for k in (0, 1, 2))
    h = (_rms(x) * (prm_ref[0:1, :] * (1.0 + sc1)) + sh1).astype(BF16)

    def proj(lo, width):
        return _dot(h, _unpack_rows(win_ref[:, lo:lo + width]))

    def add_gated_branch(n, first=False):
        for lo in range(0, d, GATE_BLOCK):
            cols = slice(lo, lo + GATE_BLOCK)
            gated = _sigmoid(proj(o_gate + n * d + lo, GATE_BLOCK)) * _dot(y_ref[n], _unpack_rows(wbr_ref[n, :, cols]))
            m_ref[:, cols] = gated if first else m_ref[:, cols] + gated

    cosv, sinv = rope_ref[:, :RET_QK_DIM], rope_ref[:, RET_QK_DIM:]
    qf, kf = proj(o_q, qk_w), proj(o_k, qk_w)
    k_scale = RET_QK_DIM ** -0.5
    for hd, sl in enumerate(qk_slices):
        q_h, k_h = qf[:, sl], kf[:, sl]
        q_r = q_h * cosv + pltpu.roll(q_h, RET_QK_DIM // 2, 1) * sinv
        k_r = (k_h * cosv + pltpu.roll(k_h, RET_QK_DIM // 2, 1) * sinv) * k_scale
        qb_ref[:, sl] = q_r.astype(BF16)
        qd_ref[:, sl] = (q_r * dqk_ref[0, hd]).astype(BF16)
        kb_ref[:, sl] = k_r.astype(BF16)
        kd_ref[:, sl] = (k_r * dqk_ref[1, hd]).astype(BF16)
    vb_ref[...] = proj(o_v, v_w).astype(BF16)
    vs = _gelu_tanh(proj(o_vs, d))
    vn_ref[...] = (_rms(vs) * prm_ref[1:2, :]).astype(BF16)

    scores, kv = {}, {}
    for c, rs in enumerate(row_slices):
        for hd, sl in enumerate(qk_slices):
            s = lax.dot_general(qb_ref[rs, sl], kb_ref[rs, sl], (((1,), (1,)), ((), ())),
                                preferred_element_type=F32)
            scores[c, hd] = (s * dintra_ref[hd]).astype(BF16)
            kv[c, hd] = lax.dot_general(kd_ref[rs, sl], vb_ref[rs, v_slices[hd]], (((0,), (0,)), ((), ())),
                                        preferred_element_type=F32)

    u_ref[...] = _gelu_tanh(proj(o_u, d))

    for hd, (sl, vsl) in enumerate(zip(qk_slices, v_slices)):
        state = state_ref[hd]
        ret = []
        for c, rs in enumerate(row_slices):
            lhs = jnp.concatenate([scores[c, hd], qd_ref[rs, sl]], axis=1)
            rhs = jnp.concatenate([vb_ref[rs, vsl], state.astype(BF16)], axis=0)
            ret.append(_dot(lhs, rhs))
            state = dchunk_ref[hd] * state + kv[c, hd]
        state_ref[hd] = state
        half_g = 0.5 * proj(o_g + hd * RET_V_DIM, RET_V_DIM)
        silu_g = half_g + half_g * jnp.tanh(half_g)
        for c, rs in enumerate(row_slices):
            y_ref[0, rs, vsl] = (silu_g[rs, :] * _rms(ret[c])).astype(BF16)

    p = proj(o_p, d)
    ext = jnp.concatenate([ptail_ref[...], p], axis=0)
    ptail_ref[...] = p[rows - POOL_HISTORY:, :]
    pos = seq_step * rows + lax.broadcasted_iota(jnp.int32, (rows, pdim), 0)
    pooled = []
    for gi, window in enumerate(POOL_WINDOWS):
        psl = slice(gi * pdim, (gi + 1) * pdim)
        acc = ext[:, psl]
        span = 1
        while span < window:
            acc = acc + pltpu.roll(acc, span, 0)
            span *= 2
        count = jnp.minimum(pos + 1, window).astype(F32)
        pooled.append((acc[POOL_HISTORY:, :] / count - p[:, psl]).astype(BF16))

    tri = lax.broadcasted_iota(jnp.int32, (CHUNK, CHUNK), 0) >= lax.broadcasted_iota(jnp.int32, (CHUNK, CHUNK), 1)
    for gi in range(GMLP_GROUPS):
        gsl = slice(gi * gdim, (gi + 1) * gdim)
        w_tri = jnp.where(tri, ws_ref[gi], 0.0).astype(BF16)
        bias = jnp.broadcast_to(bst_ref[:, gi:gi + 1], (CHUNK, gdim))
        for rs in row_slices:
            mixed = _dot(w_tri, vn_ref[rs, gsl]) + bias
            y_ref[1, rs, gsl] = (u_ref[rs, gsl] * mixed).astype(BF16)

    add_gated_branch(0, first=True)

    for gi in range(len(POOL_WINDOWS)):
        psl = slice(gi * pdim, (gi + 1) * pdim)
        mixed = _dot(pooled[gi], _unpack_rows(wpool_ref[gi])) + prm_ref[2:3, psl]
        y_ref[2, :, psl] = (mixed * prm_ref[3:4, psl]).astype(BF16)

    add_gated_branch(1)
    add_gated_branch(2)
    o_ref[...] = x + gt1 * _dot(m_ref[...].astype(BF16), _unpack_rows(wout_ref[...]))


def _mixer(layer, x, mod, prm, rope_t, w_in, w_branch, w_out, w_pool, ws, bs_t, dintra, dqk, dchunk, cast_jobs):
    b, s, d = x.shape
    rows = MIXER_ROWS
    steps = s // rows
    d_in = w_in.shape[1]
    v_w = RET_HEADS * RET_V_DIM
    qk_w = RET_HEADS * RET_QK_DIM
    row_spec = lambda width: pl.BlockSpec((None, rows, width), lambda i, j: (i, j, 0))
    lres = functools.partial(_layer_resident, layer)
    in_specs = [
        pl.BlockSpec(memory_space=pltpu.SMEM),
        row_spec(d),
        lres(mod.shape[1:]),
        lres(prm.shape[1:]),
        row_spec(2 * RET_QK_DIM),
        _resident(w_in.shape), _resident(w_branch.shape), _resident(w_out.shape),
        _resident(w_pool.shape), lres(ws.shape[1:]), lres(bs_t.shape[1:]),
        _resident(dintra.shape), _resident(dqk.shape),
    ]
    assert len(in_specs) == N_MIXER_INPUTS
    cast_in, cast_out, cast_shapes, cast_bytes = _round_block_specs(cast_jobs, b * steps, lambda i, j: i * steps + j)
    in_specs += cast_in
    out_specs = [row_spec(d)] + cast_out
    out_shape = [jax.ShapeDtypeStruct((b, s, d), F32)] + cast_shapes
    scratch = [
        pltpu.VMEM((RET_HEADS, RET_QK_DIM, RET_V_DIM), F32),
        pltpu.VMEM((POOL_HISTORY, d), F32),
        pltpu.VMEM((rows, qk_w), BF16), pltpu.VMEM((rows, qk_w), BF16),
        pltpu.VMEM((rows, qk_w), BF16), pltpu.VMEM((rows, qk_w), BF16),
        pltpu.VMEM((rows, v_w), BF16),
        pltpu.VMEM((rows, d), F32),
        pltpu.VMEM((rows, d), BF16),
        pltpu.VMEM((N_BRANCH, rows, d), BF16),
        pltpu.VMEM((rows, d), F32),
    ]
    resident = 4 * (w_in.size + w_branch.size + w_out.size + w_pool.size) + 4 * (
        ws[0].size + bs_t[0].size + dintra.size + dqk.size + mod[0].size + prm[0].size)
    streamed = 4 * rows * (2 * d + 2 * RET_QK_DIM) + cast_bytes
    scratch_bytes = 4 * (RET_HEADS * RET_QK_DIM * RET_V_DIM + POOL_HISTORY * d) + rows * (
        2 * 4 * qk_w + 2 * v_w + 4 * d + 2 * d + 2 * N_BRANCH * d + 4 * d)
    temps = 8 * 4 * rows * d
    return pl.pallas_call(
        functools.partial(_mixer_kernel, n_cast=len(cast_jobs)),
        grid=(b, steps),
        in_specs=in_specs,
        out_specs=out_specs,
        out_shape=out_shape,
        scratch_shapes=scratch,
        compiler_params=pltpu.CompilerParams(
            dimension_semantics=("arbitrary", "arbitrary"),
            vmem_limit_bytes=_vmem_limit(resident, streamed, scratch_bytes, temps)),
        name="token_mixer",
    )(dchunk, x, mod, prm, rope_t, w_in, w_branch, w_out, w_pool, ws, bs_t, dintra, dqk,
      *[w for w, _ in cast_jobs])


def _ffn_kernel(x_ref, mod_ref, n2_ref, w1_ref, w2_ref, fn_ref, o_ref, hid_ref, *, final):
    x = x_ref[...]
    batch_row = pl.ds(pl.program_id(0), 1)
    d = x_ref.shape[1]
    sh2, sc2, gt2 = (mod_ref[batch_row, k * d:(k + 1) * d] for k in (3, 4, 5))
    h = (_rms(x) * (n2_ref[...] * (1.0 + sc2)) + sh2).astype(BF16)
    d_ff = w1_ref.shape[1]
    for lo in range(0, d_ff, FF_BLOCK):
        hid = _dot(h, _unpack_rows(w1_ref[:, lo:lo + FF_BLOCK]))
        hid_ref[:, lo:lo + FF_BLOCK] = jnp.square(jnp.maximum(hid, 0.0)).astype(BF16)
    y = x + gt2 * _dot(hid_ref[...], _unpack_rows(w2_ref[...]))
    if final:
        y = _rms(y) * fn_ref[...]
    o_ref[...] = y


def _ffn(layer, x, mod, norm2, w1, w2, final_norm, final):
    b, s, d = x.shape
    rows = FFN_ROWS
    d_ff = w1.shape[1]
    row_spec = pl.BlockSpec((None, rows, d), lambda i, j: (i, j, 0))
    lres = functools.partial(_layer_resident, layer)
    resident = 2 * 2 * d * d_ff + 4 * 2 * d + 4 * mod[0].size
    streamed = 4 * 2 * rows * d
    temps = rows * (2 * d + 6 * FF_BLOCK + 3 * 4 * d)
    return pl.pallas_call(
        functools.partial(_ffn_kernel, final=final),
        grid=(b, s // rows),
        in_specs=[
            row_spec,
            lres(mod.shape[1:]),
            lres((1, d)), _resident(w1.shape), _resident(w2.shape), _resident((1, d)),
        ],
        out_specs=row_spec,
        out_shape=jax.ShapeDtypeStruct((b, s, d), F32),
        scratch_shapes=[pltpu.VMEM((rows, d_ff), BF16)],
        compiler_params=pltpu.CompilerParams(
            dimension_semantics=("arbitrary", "arbitrary"),
            vmem_limit_bytes=_vmem_limit(resident, streamed, 2 * rows * d_ff, temps)),
        name="channel_mlp",
    )(x, mod, norm2, w1, w2, final_norm)


def _retention_decay_tables(rows):
    log_gamma = jnp.log1p(-jnp.power(2.0, -5.0 - jnp.arange(RET_HEADS, dtype=F32)))
    pos = jnp.arange(CHUNK, dtype=F32)
    rel = pos[:, None] - pos[None, :]
    causal = rel >= 0
    intra = jnp.where(causal[None], jnp.exp(log_gamma[:, None, None] * jnp.where(causal, rel, 0.0)[None]), 0.0)
    decay_q = jnp.exp(log_gamma[:, None] * (pos + 1.0)[None])
    decay_k = jnp.exp(log_gamma[:, None] * (CHUNK - 1.0 - pos)[None])
    decay_chunk = jnp.exp(log_gamma * CHUNK)
    reps = rows // CHUNK
    widen = lambda t: jnp.broadcast_to(jnp.tile(t, (1, reps))[:, :, None], (RET_HEADS, rows, RET_QK_DIM))
    return intra, jnp.stack([widen(decay_q), widen(decay_k)]), decay_chunk


def kernel(x, c, positions, w_ada, b_ada, norm1, norm2, w_in, ws_gmlp, bs_gmlp, vnorm_gmlp, w_pool, b_pool,
           pool_scale, w_branch, w_out, w_ff1, w_ff2, final_norm):
    depth = w_in.shape[0]
    b, s, d = x.shape
    assert s % MIXER_ROWS == 0 and s % FFN_ROWS == 0 and MIXER_ROWS % CHUNK == 0
    assert w_ff1.shape[2] % FF_BLOCK == 0

    dintra, dqk, dchunk = _retention_decay_tables(MIXER_ROWS)
    rows_of = lambda v: v.reshape(depth, 1, d)
    prm = jnp.concatenate([rows_of(v) for v in (norm1, vnorm_gmlp, b_pool, pool_scale)], axis=1)
    norm2 = rows_of(norm2)
    bs_t = jnp.swapaxes(bs_gmlp, 1, 2)

    mixer_w = (w_in, w_branch, w_out, w_pool)
    as_rows = lambda w: w.reshape(depth, -1, w.shape[-1])
    mod, rope_t, *next_mixer = _prepare(c, w_ada, b_ada, positions, [(as_rows(w), 0) for w in mixer_w])
    for l in range(depth):
        mixer_bf16 = [w.reshape(ref.shape[1:-2] + (ref.shape[-2] // ROWS_PER_WORD, ref.shape[-1]))
                      for w, ref in zip(next_mixer, mixer_w)]
        jobs = [(as_rows(w_ff1), l), (as_rows(w_ff2), l)]
        if l + 1 < depth:
            jobs += [(as_rows(w), l + 1) for w in mixer_w]
        x, ff1, ff2, *next_mixer = _mixer(l, x, mod, prm, rope_t, *mixer_bf16, ws_gmlp, bs_t,
                                          dintra, dqk, dchunk, jobs)
        x = _ffn(l, x, mod, norm2, ff1, ff2, final_norm.reshape(1, d), final=(l == depth - 1))
    return x
```

```python
import functools

import jax
import jax.numpy as jnp
from jax import lax
from jax.experimental import pallas as pl
from jax.experimental.pallas import tpu as pltpu

CHUNK = 128
RET_HEADS = 4
RET_QK_DIM = 128
RET_V_DIM = 256
GMLP_GROUPS = 4
POOL_WINDOWS = (2, 4, 8, 16)
POOL_HISTORY = 16
N_BRANCH = 3
N_MOD = 6
ROPE_BASE = 10000.0
EPS = 1e-6

VMEM_BYTES_V7X = 64 * 1024 * 1024
VMEM_REQUEST_CAP_V7X = VMEM_BYTES_V7X - 4 * 1024 * 1024
BF16_ROW_TILE = 16

MIXER_ROWS = 256
FFN_ROWS = 1024
FF_BLOCK = 1024
GATE_BLOCK = 256
PREP_STEPS = 16

F32 = jnp.float32
BF16 = jnp.bfloat16


def _dot(a, b):
    return jnp.dot(a, b, preferred_element_type=F32)


def _rms(x):
    return x * lax.rsqrt(jnp.mean(x * x, axis=-1, keepdims=True) + EPS)


def _resident(shape):
    zeros = (0,) * len(shape)
    return pl.BlockSpec(shape, lambda *_: zeros, pipeline_mode=pl.Buffered(1))


def _layer_resident(layer, shape):
    index = (layer,) + (0,) * len(shape)
    return pl.BlockSpec((None,) + tuple(shape), lambda *_: index, pipeline_mode=pl.Buffered(1))


def _vmem_limit(resident_bytes, streamed_bytes, scratch_bytes, temp_bytes):
    need = resident_bytes + 2 * streamed_bytes + scratch_bytes + temp_bytes
    return min(int(need), VMEM_REQUEST_CAP_V7X)


PACKED = jnp.int32
ROWS_PER_WORD = 2


def _pack_rows(w_bf16):
    return pltpu.bitcast(w_bf16, PACKED)


def _unpack_rows(w_packed):
    return pltpu.bitcast(w_packed, BF16)


def _round_blocks(cast_in, cast_out):
    for src_ref, dst_ref in zip(cast_in, cast_out):
        dst_ref[...] = _pack_rows(src_ref[...].astype(BF16))


def _round_block_specs(cast_jobs, n_steps, flat_step):
    in_specs, out_specs, out_shapes, nbytes = [], [], [], 0
    for w, layer in cast_jobs:
        _, r, c = w.shape
        blk = r // n_steps
        assert blk * n_steps == r and blk % BF16_ROW_TILE == 0
        in_specs.append(pl.BlockSpec((None, blk, c), lambda *idx, layer=layer: (layer, flat_step(*idx), 0)))
        out_specs.append(pl.BlockSpec((blk // ROWS_PER_WORD, c), lambda *idx: (flat_step(*idx), 0)))
        out_shapes.append(jax.ShapeDtypeStruct((r // ROWS_PER_WORD, c), PACKED))
        nbytes += blk * c * (4 + 2)
    return in_specs, out_specs, out_shapes, nbytes


N_PREP_INPUTS = 6


def _prep_kernel(*refs, n_cast):
    c_ref, wada_ref, bada_ref, pos_ref, freq_ref, sign_ref = refs[:N_PREP_INPUTS]
    cast_in = refs[N_PREP_INPUTS:N_PREP_INPUTS + n_cast]
    mod_ref, tab_ref = refs[N_PREP_INPUTS + n_cast:N_PREP_INPUTS + n_cast + 2]
    cast_out = refs[N_PREP_INPUTS + n_cast + 2:]
    _round_blocks(cast_in, cast_out)

    c = c_ref[...]
    c_act = (c * jax.nn.sigmoid(c)).astype(BF16)
    mod_ref[...] = _dot(c_act, wada_ref[...].astype(BF16)) + bada_ref[...]

    rows = tab_ref.shape[0]
    groups, half_rows, half = rows // RET_QK_DIM, rows // 2, RET_QK_DIM // 2
    first = pl.multiple_of(pl.program_id(0) * groups, groups)
    pos = pos_ref[pl.ds(first, groups), :].astype(F32)
    square = (RET_QK_DIM, RET_QK_DIM)
    diag = lax.broadcasted_iota(jnp.int32, square, 0) == lax.broadcasted_iota(jnp.int32, square, 1)
    column = lambda g: jnp.sum(jnp.where(diag, pos[g:g + 1, :], 0.0), axis=1, keepdims=True)
    low_g = lax.broadcasted_iota(jnp.int32, square, 1) < half
    ang = jnp.concatenate([jnp.where(low_g, column(g), column(g + groups // 2)) for g in range(groups // 2)], axis=0)
    ang = ang * freq_ref[...]
    low = lax.broadcasted_iota(jnp.int32, (half_rows, RET_QK_DIM), 1) < half
    cos_p, sin_p = jnp.cos(ang), jnp.sin(ang)
    cos_s, sin_s = pltpu.roll(cos_p, half, 1), pltpu.roll(sin_p, half, 1)
    sign = sign_ref[...]
    tab_ref[:half_rows, :RET_QK_DIM] = jnp.where(low, cos_p, cos_s)
    tab_ref[half_rows:, :RET_QK_DIM] = jnp.where(low, cos_s, cos_p)
    tab_ref[:half_rows, RET_QK_DIM:] = jnp.where(low, sin_p, sin_s) * sign
    tab_ref[half_rows:, RET_QK_DIM:] = jnp.where(low, sin_s, sin_p) * sign


def _prepare(c, w_ada, b_ada, positions, cast_jobs):
    depth, d, n = w_ada.shape
    bsz, s = positions.shape
    steps = PREP_STEPS
    per_layer = steps // depth
    cols, rows = n // per_layer, bsz * s // steps
    assert per_layer * depth == steps and cols * per_layer == n and cols % RET_QK_DIM == 0
    assert rows * steps == bsz * s and rows % (8 * RET_QK_DIM) == 0
    half = RET_QK_DIM // 2
    inv_freq = ROPE_BASE ** (-jnp.arange(half, dtype=F32) / half)
    freq = jnp.concatenate([inv_freq, inv_freq]).reshape(1, RET_QK_DIM)
    sign = jnp.concatenate([-jnp.ones((half,), F32), jnp.ones((half,), F32)]).reshape(1, RET_QK_DIM)
    cast_in, cast_out, cast_shapes, cast_bytes = _round_block_specs(cast_jobs, steps, lambda t: t)
    ada_block = lambda lead: pl.BlockSpec((None, lead, cols), lambda t: (t // per_layer, 0, t % per_layer))
    const = lambda shape: pl.BlockSpec(shape, lambda t: (0,) * len(shape))
    streamed = cast_bytes + 4 * (d * cols + cols + bsz * cols + rows * (RET_QK_DIM + 2 * RET_QK_DIM))
    mod, table, *rounded = pl.pallas_call(
        functools.partial(_prep_kernel, n_cast=len(cast_jobs)),
        grid=(steps,),
        in_specs=[
            const((bsz, d)), ada_block(d), ada_block(1),
            const((bsz * s // RET_QK_DIM, RET_QK_DIM)), const((1, RET_QK_DIM)), const((1, RET_QK_DIM)),
        ] + cast_in,
        out_specs=[ada_block(bsz), pl.BlockSpec((rows, 2 * RET_QK_DIM), lambda t: (t, 0))] + cast_out,
        out_shape=[jax.ShapeDtypeStruct((depth, bsz, n), F32),
                   jax.ShapeDtypeStruct((bsz * s, 2 * RET_QK_DIM), F32)] + cast_shapes,
        compiler_params=pltpu.CompilerParams(
            dimension_semantics=("arbitrary",),
            vmem_limit_bytes=_vmem_limit(4 * bsz * d, streamed, 0, 2 * d * cols + 16 * 4 * rows * RET_QK_DIM)),
        name="prepare",
    )(c, w_ada, b_ada.reshape(depth, 1, n), positions.reshape(bsz * s // RET_QK_DIM, RET_QK_DIM), freq, sign,
      *[w for w, _ in cast_jobs])
    return (mod, table.reshape(bsz, s, 2 * RET_QK_DIM), *rounded)


GELU_C0 = 0.7978845608028654
GELU_C1 = GELU_C0 * 0.044715


def _sigmoid(x):
    return 0.5 * jnp.tanh(0.5 * x) + 0.5


def _gelu_tanh(x):
    half_x = 0.5 * x
    return half_x + half_x * jnp.tanh(x * (GELU_C0 + GELU_C1 * (x * x)))


N_MIXER_INPUTS = 13


def _mixer_kernel(*refs, n_cast):
    ins = refs[:N_MIXER_INPUTS]
    cast_in = refs[N_MIXER_INPUTS:N_MIXER_INPUTS + n_cast]
    o_ref = refs[N_MIXER_INPUTS + n_cast]
    cast_out = refs[N_MIXER_INPUTS + n_cast + 1:N_MIXER_INPUTS + 2 * n_cast + 1]
    scratch = refs[N_MIXER_INPUTS + 2 * n_cast + 1:]
    _mixer_step(*ins, o_ref, *scratch)
    _round_blocks(cast_in, cast_out)


def _mixer_step(dchunk_ref, x_ref, mod_ref, prm_ref, rope_ref, win_ref, wbr_ref, wout_ref,
                wpool_ref, ws_ref, bst_ref, dintra_ref, dqk_ref,
                o_ref,
                state_ref, ptail_ref, qb_ref, qd_ref, kb_ref, kd_ref, vb_ref, u_ref, vn_ref, y_ref, m_ref):
    rows, d = x_ref.shape
    n_chunks = rows // CHUNK
    qk_w = RET_HEADS * RET_QK_DIM
    v_w = RET_HEADS * RET_V_DIM
    gdim = d // GMLP_GROUPS
    pdim = d // len(POOL_WINDOWS)
    o_q, o_k, o_v = 0, qk_w, 2 * qk_w
    o_g = o_v + v_w
    o_u = o_g + v_w
    o_vs = o_u + d
    o_p = o_vs + d
    o_gate = o_p + d
    seq_step = pl.program_id(1)
    row_slices = [slice(c * CHUNK, (c + 1) * CHUNK) for c in range(n_chunks)]
    qk_slices = [slice(hd * RET_QK_DIM, (hd + 1) * RET_QK_DIM) for hd in range(RET_HEADS)]
    v_slices = [slice(hd * RET_V_DIM, (hd + 1) * RET_V_DIM) for hd in range(RET_HEADS)]

    @pl.when(seq_step == 0)
    def _():
        state_ref[...] = jnp.zeros_like(state_ref)
        ptail_ref[...] = jnp.zeros_like(ptail_ref)

    x = x_ref[...]
    batch_row = pl.ds(pl.program_id(0), 1)
    sh1, sc1, gt1 = (mod_ref[batch_row, k * d:(k + 1) * d] for k in (0, 1, 2))
    h = (_rms(x) * (prm_ref[0:1, :] * (1.0 + sc1)) + sh1).astype(BF16)

    def proj(lo, width):
        return _dot(h, _unpack_rows(win_ref[:, lo:lo + width]))

    def add_gated_branch(n, first=False):
        for lo in range(0, d, GATE_BLOCK):
            cols = slice(lo, lo + GATE_BLOCK)
            gated = _sigmoid(proj(o_gate + n * d + lo, GATE_BLOCK)) * _dot(y_ref[n], _unpack_rows(wbr_ref[n, :, cols]))
            m_ref[:, cols] = gated if first else m_ref[:, cols] + gated

    cosv, sinv = rope_ref[:, :RET_QK_DIM], rope_ref[:, RET_QK_DIM:]
    qf, kf = proj(o_q, qk_w), proj(o_k, qk_w)
    k_scale = RET_QK_DIM ** -0.5
    for hd, sl in enumerate(qk_slices):
        q_h, k_h = qf[:, sl], kf[:, sl]
        q_r = q_h * cosv + pltpu.roll(q_h, RET_QK_DIM // 2, 1) * sinv
        k_r = (k_h * cosv + pltpu.roll(k_h, RET_QK_DIM // 2, 1) * sinv) * k_scale
        qb_ref[:, sl] = q_r.astype(BF16)
        qd_ref[:, sl] = (q_r * dqk_ref[0, hd]).astype(BF16)
        kb_ref[:, sl] = k_r.astype(BF16)
        kd_ref[:, sl] = (k_r * dqk_ref[1, hd]).astype(BF16)
    vb_ref[...] = proj(o_v, v_w).astype(BF16)
    vs = _gelu_tanh(proj(o_vs, d))
    vn_ref[...] = (_rms(vs) * prm_ref[1:2, :]).astype(BF16)

    scores, kv = {}, {}
    for c, rs in enumerate(row_slices):
        for hd, sl in enumerate(qk_slices):
            s = lax.dot_general(qb_ref[rs, sl], kb_ref[rs, sl], (((1,), (1,)), ((), ())),
                                preferred_element_type=F32)
            scores[c, hd] = (s * dintra_ref[hd]).astype(BF16)
            kv[c, hd] = lax.dot_general(kd_ref[rs, sl], vb_ref[rs, v_slices[hd]], (((0,), (0,)), ((), ())),
                                        preferred_element_type=F32)

    u_ref[...] = _gelu_tanh(proj(o_u, d))

    for hd, (sl, vsl) in enumerate(zip(qk_slices, v_slices)):
        state = state_ref[hd]
        ret = []
        for c, rs in enumerate(row_slices):
            lhs = jnp.concatenate([scores[c, hd], qd_ref[rs, sl]], axis=1)
            rhs = jnp.concatenate([vb_ref[rs, vsl], state.astype(BF16)], axis=0)
            ret.append(_dot(lhs, rhs))
            state = dchunk_ref[hd] * state + kv[c, hd]
        state_ref[hd] = state
        half_g = 0.5 * proj(o_g + hd * RET_V_DIM, RET_V_DIM)
        silu_g = half_g + half_g * jnp.tanh(half_g)
        for c, rs in enumerate(row_slices):
            y_ref[0, rs, vsl] = (silu_g[rs, :] * _rms(ret[c])).astype(BF16)

    p = proj(o_p, d)
    ext = jnp.concatenate([ptail_ref[...], p], axis=0)
    ptail_ref[...] = p[rows - POOL_HISTORY:, :]
    pos = seq_step * rows + lax.broadcasted_iota(jnp.int32, (rows, pdim), 0)
    pooled = []
    for gi, window in enumerate(POOL_WINDOWS):
        psl = slice(gi * pdim, (gi + 1) * pdim)
        acc = ext[:, psl]
        span = 1
        while span < window:
            acc = acc + pltpu.roll(acc, span, 0)
            span *= 2
        count = jnp.minimum(pos + 1, window).astype(F32)
        pooled.append((acc[POOL_HISTORY:, :] / count - p[:, psl]).astype(BF16))

    tri = lax.broadcasted_iota(jnp.int32, (CHUNK, CHUNK), 0) >= lax.broadcasted_iota(jnp.int32, (CHUNK, CHUNK), 1)
    for gi in range(GMLP_GROUPS):
        gsl = slice(gi * gdim, (gi + 1) * gdim)
        w_tri = jnp.where(tri, ws_ref[gi], 0.0).astype(BF16)
        bias = jnp.broadcast_to(bst_ref[:, gi:gi + 1], (CHUNK, gdim))
        for rs in row_slices:
            mixed = _dot(w_tri, vn_ref[rs, gsl]) + bias
            y_ref[1, rs, gsl] = (u_ref[rs, gsl] * mixed).astype(BF16)

    add_gated_branch(0, first=True)

    for gi in range(len(POOL_WINDOWS)):
        psl = slice(gi * pdim, (gi + 1) * pdim)
        mixed = _dot(pooled[gi], _unpack_rows(wpool_ref[gi])) + prm_ref[2:3, psl]
        y_ref[2, :, psl] = (mixed * prm_ref[3:4, psl]).astype(BF16)

    add_gated_branch(1)
    add_gated_branch(2)
    o_ref[...] = x + gt1 * _dot(m_ref[...].astype(BF16), _unpack_rows(wout_ref[...]))


def _mixer(layer, x, mod, prm, rope_t, w_in, w_branch, w_out, w_pool, ws, bs_t, dintra, dqk, dchunk, cast_jobs):
    b, s, d = x.shape
    rows = MIXER_ROWS
    steps = s // rows
    d_in = w_in.shape[1]
    v_w = RET_HEADS * RET_V_DIM
    qk_w = RET_HEADS * RET_QK_DIM
    row_spec = lambda width: pl.BlockSpec((None, rows, width), lambda i, j: (i, j, 0))
    lres = functools.partial(_layer_resident, layer)
    in_specs = [
        pl.BlockSpec(memory_space=pltpu.SMEM),
        row_spec(d),
        lres(mod.shape[1:]),
        lres(prm.shape[1:]),
        row_spec(2 * RET_QK_DIM),
        _resident(w_in.shape), _resident(w_branch.shape), _resident(w_out.shape),
        _resident(w_pool.shape), lres(ws.shape[1:]), lres(bs_t.shape[1:]),
        _resident(dintra.shape), _resident(dqk.shape),
    ]
    assert len(in_specs) == N_MIXER_INPUTS
    cast_in, cast_out, cast_shapes, cast_bytes = _round_block_specs(cast_jobs, b * steps, lambda i, j: i * steps + j)
    in_specs += cast_in
    out_specs = [row_spec(d)] + cast_out
    out_shape = [jax.ShapeDtypeStruct((b, s, d), F32)] + cast_shapes
    scratch = [
        pltpu.VMEM((RET_HEADS, RET_QK_DIM, RET_V_DIM), F32),
        pltpu.VMEM((POOL_HISTORY, d), F32),
        pltpu.VMEM((rows, qk_w), BF16), pltpu.VMEM((rows, qk_w), BF16),
        pltpu.VMEM((rows, qk_w), BF16), pltpu.VMEM((rows, qk_w), BF16),
        pltpu.VMEM((rows, v_w), BF16),
        pltpu.VMEM((rows, d), F32),
        pltpu.VMEM((rows, d), BF16),
        pltpu.VMEM((N_BRANCH, rows, d), BF16),
        pltpu.VMEM((rows, d), F32),
    ]
    resident = 4 * (w_in.size + w_branch.size + w_out.size + w_pool.size) + 4 * (
        ws[0].size + bs_t[0].size + dintra.size + dqk.size + mod[0].size + prm[0].size)
    streamed = 4 * rows * (2 * d + 2 * RET_QK_DIM) + cast_bytes
    scratch_bytes = 4 * (RET_HEADS * RET_QK_DIM * RET_V_DIM + POOL_HISTORY * d) + rows * (
        2 * 4 * qk_w + 2 * v_w + 4 * d + 2 * d + 2 * N_BRANCH * d + 4 * d)
    temps = 8 * 4 * rows * d
    return pl.pallas_call(
        functools.partial(_mixer_kernel, n_cast=len(cast_jobs)),
        grid=(b, steps),
        in_specs=in_specs,
        out_specs=out_specs,
        out_shape=out_shape,
        scratch_shapes=scratch,
        compiler_params=pltpu.CompilerParams(
            dimension_semantics=("arbitrary", "arbitrary"),
            vmem_limit_bytes=_vmem_limit(resident, streamed, scratch_bytes, temps)),
        name="token_mixer",
    )(dchunk, x, mod, prm, rope_t, w_in, w_branch, w_out, w_pool, ws, bs_t, dintra, dqk,
      *[w for w, _ in cast_jobs])


def _ffn_kernel(x_ref, mod_ref, n2_ref, w1_ref, w2_ref, fn_ref, o_ref, hid_ref, *, final):
    x = x_ref[...]
    batch_row = pl.ds(pl.program_id(0), 1)
    d = x_ref.shape[1]
    sh2, sc2, gt2 = (mod_ref[batch_row, k * d:(k + 1) * d] for k in (3, 4, 5))
    h = (_rms(x) * (n2_ref[...] * (1.0 + sc2)) + sh2).astype(BF16)
    d_ff = w1_ref.shape[1]
    for lo in range(0, d_ff, FF_BLOCK):
        hid = _dot(h, _unpack_rows(w1_ref[:, lo:lo + FF_BLOCK]))
        hid_ref[:, lo:lo + FF_BLOCK] = jnp.square(jnp.maximum(hid, 0.0)).astype(BF16)
    y = x + gt2 * _dot(hid_ref[...], _unpack_rows(w2_ref[...]))
    if final:
        y = _rms(y) * fn_ref[...]
    o_ref[...] = y


def _ffn(layer, x, mod, norm2, w1, w2, final_norm, final):
    b, s, d = x.shape
    rows = FFN_ROWS
    d_ff = w1.shape[1]
    row_spec = pl.BlockSpec((None, rows, d), lambda i, j: (i, j, 0))
    lres = functools.partial(_layer_resident, layer)
    resident = 2 * 2 * d * d_ff + 4 * 2 * d + 4 * mod[0].size
    streamed = 4 * 2 * rows * d
    temps = rows * (2 * d + 6 * FF_BLOCK + 3 * 4 * d)
    return pl.pallas_call(
        functools.partial(_ffn_kernel, final=final),
        grid=(b, s // rows),
        in_specs=[
            row_spec,
            lres(mod.shape[1:]),
            lres((1, d)), _resident(w1.shape), _resident(w2.shape), _resident((1, d)),
        ],
        out_specs=row_spec,
        out_shape=jax.ShapeDtypeStruct((b, s, d), F32),
        scratch_shapes=[pltpu.VMEM((rows, d_ff), BF16)],
        compiler_params=pltpu.CompilerParams(
            dimension_semantics=("arbitrary", "arbitrary"),
            vmem_limit_bytes=_vmem_limit(resident, streamed, 2 * rows * d_ff, temps)),
        name="channel_mlp",
    )(x, mod, norm2, w1, w2, final_norm)


def _retention_decay_tables(rows):
    log_gamma = jnp.log1p(-jnp.power(2.0, -5.0 - jnp.arange(RET_HEADS, dtype=F32)))
    pos = jnp.arange(CHUNK, dtype=F32)
    rel = pos[:, None] - pos[None, :]
    causal = rel >= 0
    intra = jnp.where(causal[None], jnp.exp(log_gamma[:, None, None] * jnp.where(causal, rel, 0.0)[None]), 0.0)
    decay_q = jnp.exp(log_gamma[:, None] * (pos + 1.0)[None])
    decay_k = jnp.exp(log_gamma[:, None] * (CHUNK - 1.0 - pos)[None])
    decay_chunk = jnp.exp(log_gamma * CHUNK)
    reps = rows // CHUNK
    widen = lambda t: jnp.broadcast_to(jnp.tile(t, (1, reps))[:, :, None], (RET_HEADS, rows, RET_QK_DIM))
    return intra, jnp.stack([widen(decay_q), widen(decay_k)]), decay_chunk


def kernel(x, c, positions, w_ada, b_ada, norm1, norm2, w_in, ws_gmlp, bs_gmlp, vnorm_gmlp, w_pool, b_pool,
           pool_scale, w_branch, w_out, w_ff1, w_ff2, final_norm):
    depth = w_in.shape[0]
    b, s, d = x.shape
    assert s % MIXER_ROWS == 0 and s % FFN_ROWS == 0 and MIXER_ROWS % CHUNK == 0
    assert w_ff1.shape[2] % FF_BLOCK == 0

    dintra, dqk, dchunk = _retention_decay_tables(MIXER_ROWS)
    rows_of = lambda v: v.reshape(depth, 1, d)
    prm = jnp.concatenate([rows_of(v) for v in (norm1, vnorm_gmlp, b_pool, pool_scale)], axis=1)
    norm2 = rows_of(norm2)
    bs_t = jnp.swapaxes(bs_gmlp, 1, 2)

    mixer_w = (w_in, w_branch, w_out, w_pool)
    as_rows = lambda w: w.reshape(depth, -1, w.shape[-1])
    mod, rope_t, *next_mixer = _prepare(c, w_ada, b_ada, positions, [(as_rows(w), 0) for w in mixer_w])
    for l in range(depth):
        mixer_bf16 = [w.reshape(ref.shape[1:-2] + (ref.shape[-2] // ROWS_PER_WORD, ref.shape[-1]))
                      for w, ref in zip(next_mixer, mixer_w)]
        jobs = [(as_rows(w_ff1), l), (as_rows(w_ff2), l)]
        if l + 1 < depth:
            jobs += [(as_rows(w), l + 1) for w in mixer_w]
        x, ff1, ff2, *next_mixer = _mixer(l, x, mod, prm, rope_t, *mixer_bf16, ws_gmlp, bs_t,
                                          dintra, dqk, dchunk, jobs)
        x = _ffn(l, x, mod, norm2, ff1, ff2, final_norm.reshape(1, d), final=(l == depth - 1))
    return x
```

```python
import functools

import jax
import jax.numpy as jnp
from jax import lax
from jax.experimental import pallas as pl
from jax.experimental.pallas import tpu as pltpu

CHUNK = 128
RET_HEADS = 4
RET_QK_DIM = 128
RET_V_DIM = 256
GMLP_GROUPS = 4
POOL_WINDOWS = (2, 4, 8, 16)
POOL_HISTORY = 16
N_BRANCH = 3
N_MOD = 6
ROPE_BASE = 10000.0
EPS = 1e-6

VMEM_BYTES_V7X = 64 * 1024 * 1024
VMEM_REQUEST_CAP_V7X = VMEM_BYTES_V7X - 4 * 1024 * 1024
BF16_ROW_TILE = 16

MIXER_ROWS = 256
FFN_ROWS = 1024
FF_BLOCK = 1024
GATE_BLOCK = 256
PREP_STEPS = 16

F32 = jnp.float32
BF16 = jnp.bfloat16


def _dot(a, b):
    return jnp.dot(a, b, preferred_element_type=F32)


def _rms(x):
    return x * lax.rsqrt(jnp.mean(x * x, axis=-1, keepdims=True) + EPS)


def _resident(shape):
    zeros = (0,) * len(shape)
    return pl.BlockSpec(shape, lambda *_: zeros, pipeline_mode=pl.Buffered(1))


def _layer_resident(layer, shape):
    index = (layer,) + (0,) * len(shape)
    return pl.BlockSpec((None,) + tuple(shape), lambda *_: index, pipeline_mode=pl.Buffered(1))


def _vmem_limit(resident_bytes, streamed_bytes, scratch_bytes, temp_bytes):
    need = resident_bytes + 2 * streamed_bytes + scratch_bytes + temp_bytes
    return min(int(need), VMEM_REQUEST_CAP_V7X)


PACKED = jnp.int32
ROWS_PER_WORD = 2


def _pack_rows(w_bf16):
    return pltpu.bitcast(w_bf16, PACKED)


def _unpack_rows(w_packed):
    return pltpu.bitcast(w_packed, BF16)


def _round_blocks(cast_in, cast_out):
    for src_ref, dst_ref in zip(cast_in, cast_out):
        dst_ref[...] = _pack_rows(src_ref[...].astype(BF16))


def _round_block_specs(cast_jobs, n_steps, flat_step):
    in_specs, out_specs, out_shapes, nbytes = [], [], [], 0
    for w, layer in cast_jobs:
        _, r, c = w.shape
        blk = r // n_steps
        assert blk * n_steps == r and blk % BF16_ROW_TILE == 0
        in_specs.append(pl.BlockSpec((None, blk, c), lambda *idx, layer=layer: (layer, flat_step(*idx), 0)))
        out_specs.append(pl.BlockSpec((blk // ROWS_PER_WORD, c), lambda *idx: (flat_step(*idx), 0)))
        out_shapes.append(jax.ShapeDtypeStruct((r // ROWS_PER_WORD, c), PACKED))
        nbytes += blk * c * (4 + 2)
    return in_specs, out_specs, out_shapes, nbytes


N_PREP_INPUTS = 6


def _prep_kernel(*refs, n_cast, steps_per_layer, steps_per_row):
    c_ref, wada_ref, bada_ref, pos_ref, freq_ref, sign_ref = refs[:N_PREP_INPUTS]
    cast_in = refs[N_PREP_INPUTS:N_PREP_INPUTS + n_cast]
    mod_ref, tab_ref = refs[N_PREP_INPUTS + n_cast:N_PREP_INPUTS + n_cast + 2]
    cast_out = refs[N_PREP_INPUTS + n_cast + 2:]
    _round_blocks(cast_in, cast_out)

    c = c_ref[...]
    c_act = (c * jax.nn.sigmoid(c)).astype(BF16)
    step = pl.program_id(0)
    mod_ref[...] = _dot(c_act, wada_ref[...].astype(BF16)) + bada_ref[pl.ds(step // steps_per_layer, 1), :]

    rows = tab_ref.shape[0]
    groups, half_rows, half = rows // RET_QK_DIM, rows // 2, RET_QK_DIM // 2
    pos = pos_ref[pl.ds(step // steps_per_row, 1), :].astype(F32)
    square = (RET_QK_DIM, RET_QK_DIM)
    diag = lax.broadcasted_iota(jnp.int32, square, 0) == lax.broadcasted_iota(jnp.int32, square, 1)
    group = lambda g: pos[:, g * RET_QK_DIM:(g + 1) * RET_QK_DIM]
    column = lambda g: jnp.sum(jnp.where(diag, group(g), 0.0), axis=1, keepdims=True)
    low_g = lax.broadcasted_iota(jnp.int32, square, 1) < half
    ang = jnp.concatenate([jnp.where(low_g, column(g), column(g + groups // 2)) for g in range(groups // 2)], axis=0)
    ang = ang * freq_ref[...]
    low = lax.broadcasted_iota(jnp.int32, (half_rows, RET_QK_DIM), 1) < half
    cos_p, sin_p = jnp.cos(ang), jnp.sin(ang)
    cos_s, sin_s = pltpu.roll(cos_p, half, 1), pltpu.roll(sin_p, half, 1)
    sign = sign_ref[...]
    tab_ref[:half_rows, :RET_QK_DIM] = jnp.where(low, cos_p, cos_s)
    tab_ref[half_rows:, :RET_QK_DIM] = jnp.where(low, cos_s, cos_p)
    tab_ref[:half_rows, RET_QK_DIM:] = jnp.where(low, sin_p, sin_s) * sign
    tab_ref[half_rows:, RET_QK_DIM:] = jnp.where(low, sin_s, sin_p) * sign


def _prepare(c, w_ada, b_ada, positions, cast_jobs):
    depth, d, n = w_ada.shape
    bsz, s = positions.shape
    steps = PREP_STEPS
    per_layer = steps // depth
    cols, rows = n // per_layer, bsz * s // steps
    assert per_layer * depth == steps and cols * per_layer == n and cols % RET_QK_DIM == 0
    per_row = s // rows
    assert rows * per_row == s and bsz * per_row == steps and rows % (2 * RET_QK_DIM) == 0
    half = RET_QK_DIM // 2
    inv_freq = ROPE_BASE ** (-jnp.arange(half, dtype=F32) / half)
    freq = jnp.concatenate([inv_freq, inv_freq]).reshape(1, RET_QK_DIM)
    sign = jnp.concatenate([-jnp.ones((half,), F32), jnp.ones((half,), F32)]).reshape(1, RET_QK_DIM)
    cast_in, cast_out, cast_shapes, cast_bytes = _round_block_specs(cast_jobs, steps, lambda t: t)
    ada_block = lambda lead: pl.BlockSpec((None, lead, cols), lambda t: (t // per_layer, 0, t % per_layer))
    const = lambda shape: pl.BlockSpec(shape, lambda t: (0,) * len(shape))
    streamed = cast_bytes + 4 * (d * cols + cols + bsz * cols + rows * (RET_QK_DIM + 2 * RET_QK_DIM))
    mod, table, *rounded = pl.pallas_call(
        functools.partial(_prep_kernel, n_cast=len(cast_jobs), steps_per_layer=per_layer, steps_per_row=per_row),
        grid=(steps,),
        in_specs=[
            const((bsz, d)), ada_block(d), pl.BlockSpec((depth, cols), lambda t: (0, t % per_layer)),
            pl.BlockSpec((bsz, rows), lambda t: (0, t % per_row)), const((1, RET_QK_DIM)), const((1, RET_QK_DIM)),
        ] + cast_in,
        out_specs=[ada_block(bsz), pl.BlockSpec((rows, 2 * RET_QK_DIM), lambda t: (t, 0))] + cast_out,
        out_shape=[jax.ShapeDtypeStruct((depth, bsz, n), F32),
                   jax.ShapeDtypeStruct((bsz * s, 2 * RET_QK_DIM), F32)] + cast_shapes,
        compiler_params=pltpu.CompilerParams(
            dimension_semantics=("arbitrary",),
            vmem_limit_bytes=_vmem_limit(4 * bsz * d, streamed, 0, 2 * d * cols + 16 * 4 * rows * RET_QK_DIM)),
        name="prepare",
    )(c, w_ada, b_ada, positions, freq, sign, *[w for w, _ in cast_jobs])
    return (mod, table.reshape(bsz, s, 2 * RET_QK_DIM), *rounded)


GELU_C0 = 0.7978845608028654
GELU_C1 = GELU_C0 * 0.044715


def _sigmoid(x):
    return 0.5 * jnp.tanh(0.5 * x) + 0.5


def _gelu_tanh(x):
    half_x = 0.5 * x
    return half_x + half_x * jnp.tanh(x * (GELU_C0 + GELU_C1 * (x * x)))


N_MIXER_INPUTS = 16


def _mixer_kernel(*refs, n_cast, layer):
    ins = refs[:N_MIXER_INPUTS]
    cast_in = refs[N_MIXER_INPUTS:N_MIXER_INPUTS + n_cast]
    o_ref = refs[N_MIXER_INPUTS + n_cast]
    cast_out = refs[N_MIXER_INPUTS + n_cast + 1:N_MIXER_INPUTS + 2 * n_cast + 1]
    scratch = refs[N_MIXER_INPUTS + 2 * n_cast + 1:]
    _mixer_step(*ins, o_ref, *scratch, layer=layer)
    _round_blocks(cast_in, cast_out)


def _mixer_step(dchunk_ref, x_ref, mod_ref, n1_ref, vgain_ref, bpool_ref, pscale_ref, rope_ref,
                win_ref, wbr_ref, wout_ref, wpool_ref, ws_ref, bst_ref, dintra_ref, dqk_ref,
                o_ref,
                state_ref, ptail_ref, qb_ref, qd_ref, kb_ref, kd_ref, vb_ref, u_ref, vn_ref, y_ref, m_ref,
                *, layer):
    rows, d = x_ref.shape
    n_chunks = rows // CHUNK
    qk_w = RET_HEADS * RET_QK_DIM
    v_w = RET_HEADS * RET_V_DIM
    gdim = d // GMLP_GROUPS
    pdim = d // len(POOL_WINDOWS)
    o_q, o_k, o_v = 0, qk_w, 2 * qk_w
    o_g = o_v + v_w
    o_u = o_g + v_w
    o_vs = o_u + d
    o_p = o_vs + d
    o_gate = o_p + d
    seq_step = pl.program_id(1)
    row_slices = [slice(c * CHUNK, (c + 1) * CHUNK) for c in range(n_chunks)]
    qk_slices = [slice(hd * RET_QK_DIM, (hd + 1) * RET_QK_DIM) for hd in range(RET_HEADS)]
    v_slices = [slice(hd * RET_V_DIM, (hd + 1) * RET_V_DIM) for hd in range(RET_HEADS)]

    @pl.when(seq_step == 0)
    def _():
        state_ref[...] = jnp.zeros_like(state_ref)
        ptail_ref[...] = jnp.zeros_like(ptail_ref)

    x = x_ref[...]
    batch_row = pl.ds(pl.program_id(0), 1)
    sh1, sc1, gt1 = (mod_ref[batch_row, k * d:(k + 1) * d] for k in (0, 1, 2))
    h = (_rms(x) * (n1_ref[layer:layer + 1, :] * (1.0 + sc1)) + sh1).astype(BF16)

    def proj(lo, width):
        return _dot(h, _unpack_rows(win_ref[:, lo:lo + width]))

    def add_gated_branch(n, first=False):
        for lo in range(0, d, GATE_BLOCK):
            cols = slice(lo, lo + GATE_BLOCK)
            gated = _sigmoid(proj(o_gate + n * d + lo, GATE_BLOCK)) * _dot(y_ref[n], _unpack_rows(wbr_ref[n, :, cols]))
            m_ref[:, cols] = gated if first else m_ref[:, cols] + gated

    cosv, sinv = rope_ref[:, :RET_QK_DIM], rope_ref[:, RET_QK_DIM:]
    qf, kf = proj(o_q, qk_w), proj(o_k, qk_w)
    k_scale = RET_QK_DIM ** -0.5
    for hd, sl in enumerate(qk_slices):
        q_h, k_h = qf[:, sl], kf[:, sl]
        q_r = q_h * cosv + pltpu.roll(q_h, RET_QK_DIM // 2, 1) * sinv
        k_r = (k_h * cosv + pltpu.roll(k_h, RET_QK_DIM // 2, 1) * sinv) * k_scale
        qb_ref[:, sl] = q_r.astype(BF16)
        qd_ref[:, sl] = (q_r * dqk_ref[0, hd]).astype(BF16)
        kb_ref[:, sl] = k_r.astype(BF16)
        kd_ref[:, sl] = (k_r * dqk_ref[1, hd]).astype(BF16)
    vb_ref[...] = proj(o_v, v_w).astype(BF16)
    vs = _gelu_tanh(proj(o_vs, d))
    vn_ref[...] = (_rms(vs) * vgain_ref[layer:layer + 1, :]).astype(BF16)

    scores, kv = {}, {}
    for c, rs in enumerate(row_slices):
        for hd, sl in enumerate(qk_slices):
            s = lax.dot_general(qb_ref[rs, sl], kb_ref[rs, sl], (((1,), (1,)), ((), ())),
                                preferred_element_type=F32)
            scores[c, hd] = (s * dintra_ref[hd]).astype(BF16)
            kv[c, hd] = lax.dot_general(kd_ref[rs, sl], vb_ref[rs, v_slices[hd]], (((0,), (0,)), ((), ())),
                                        preferred_element_type=F32)

    u_ref[...] = _gelu_tanh(proj(o_u, d))

    for hd, (sl, vsl) in enumerate(zip(qk_slices, v_slices)):
        state = state_ref[hd]
        ret = []
        for c, rs in enumerate(row_slices):
            lhs = jnp.concatenate([scores[c, hd], qd_ref[rs, sl]], axis=1)
            rhs = jnp.concatenate([vb_ref[rs, vsl], state.astype(BF16)], axis=0)
            ret.append(_dot(lhs, rhs))
            state = dchunk_ref[hd] * state + kv[c, hd]
        state_ref[hd] = state
        half_g = 0.5 * proj(o_g + hd * RET_V_DIM, RET_V_DIM)
        silu_g = half_g + half_g * jnp.tanh(half_g)
        for c, rs in enumerate(row_slices):
            y_ref[0, rs, vsl] = (silu_g[rs, :] * _rms(ret[c])).astype(BF16)

    p = proj(o_p, d)
    ext = jnp.concatenate([ptail_ref[...], p], axis=0)
    ptail_ref[...] = p[rows - POOL_HISTORY:, :]
    pos = seq_step * rows + lax.broadcasted_iota(jnp.int32, (rows, pdim), 0)
    pooled = []
    for gi, window in enumerate(POOL_WINDOWS):
        psl = slice(gi * pdim, (gi + 1) * pdim)
        acc = ext[:, psl]
        span = 1
        while span < window:
            acc = acc + pltpu.roll(acc, span, 0)
            span *= 2
        count = jnp.minimum(pos + 1, window).astype(F32)
        pooled.append((acc[POOL_HISTORY:, :] / count - p[:, psl]).astype(BF16))

    tri = lax.broadcasted_iota(jnp.int32, (CHUNK, CHUNK), 0) >= lax.broadcasted_iota(jnp.int32, (CHUNK, CHUNK), 1)
    for gi in range(GMLP_GROUPS):
        gsl = slice(gi * gdim, (gi + 1) * gdim)
        w_tri = jnp.where(tri, ws_ref[gi], 0.0).astype(BF16)
        bias = jnp.broadcast_to(bst_ref[:, gi:gi + 1], (CHUNK, gdim))
        for rs in row_slices:
            mixed = _dot(w_tri, vn_ref[rs, gsl]) + bias
            y_ref[1, rs, gsl] = (u_ref[rs, gsl] * mixed).astype(BF16)

    add_gated_branch(0, first=True)

    for gi in range(len(POOL_WINDOWS)):
        psl = slice(gi * pdim, (gi + 1) * pdim)
        mixed = _dot(pooled[gi], _unpack_rows(wpool_ref[gi])) + bpool_ref[layer, gi:gi + 1, :]
        y_ref[2, :, psl] = (mixed * pscale_ref[layer:layer + 1, psl]).astype(BF16)

    add_gated_branch(1)
    add_gated_branch(2)
    o_ref[...] = x + gt1 * _dot(m_ref[...].astype(BF16), _unpack_rows(wout_ref[...]))


def _mixer(layer, x, mod, rows_prm, rope_t, w_in, w_branch, w_out, w_pool, ws, bs_t, dintra, dqk, dchunk, cast_jobs):
    b, s, d = x.shape
    rows = MIXER_ROWS
    steps = s // rows
    d_in = w_in.shape[1]
    v_w = RET_HEADS * RET_V_DIM
    qk_w = RET_HEADS * RET_QK_DIM
    row_spec = lambda width: pl.BlockSpec((None, rows, width), lambda i, j: (i, j, 0))
    lres = functools.partial(_layer_resident, layer)
    in_specs = [
        pl.BlockSpec(memory_space=pltpu.SMEM),
        row_spec(d),
        lres(mod.shape[1:]),
        *[_resident(v.shape) for v in rows_prm],
        row_spec(2 * RET_QK_DIM),
        _resident(w_in.shape), _resident(w_branch.shape), _resident(w_out.shape),
        _resident(w_pool.shape), lres(ws.shape[1:]), lres(bs_t.shape[1:]),
        _resident(dintra.shape), _resident(dqk.shape),
    ]
    assert len(in_specs) == N_MIXER_INPUTS
    cast_in, cast_out, cast_shapes, cast_bytes = _round_block_specs(cast_jobs, b * steps, lambda i, j: i * steps + j)
    in_specs += cast_in
    out_specs = [row_spec(d)] + cast_out
    out_shape = [jax.ShapeDtypeStruct((b, s, d), F32)] + cast_shapes
    scratch = [
        pltpu.VMEM((RET_HEADS, RET_QK_DIM, RET_V_DIM), F32),
        pltpu.VMEM((POOL_HISTORY, d), F32),
        pltpu.VMEM((rows, qk_w), BF16), pltpu.VMEM((rows, qk_w), BF16),
        pltpu.VMEM((rows, qk_w), BF16), pltpu.VMEM((rows, qk_w), BF16),
        pltpu.VMEM((rows, v_w), BF16),
        pltpu.VMEM((rows, d), F32),
        pltpu.VMEM((rows, d), BF16),
        pltpu.VMEM((N_BRANCH, rows, d), BF16),
        pltpu.VMEM((rows, d), F32),
    ]
    resident = 4 * (w_in.size + w_branch.size + w_out.size + w_pool.size) + 4 * (
        ws[0].size + bs_t[0].size + dintra.size + dqk.size + mod[0].size + sum(v.size for v in rows_prm))
    streamed = 4 * rows * (2 * d + 2 * RET_QK_DIM) + cast_bytes
    scratch_bytes = 4 * (RET_HEADS * RET_QK_DIM * RET_V_DIM + POOL_HISTORY * d) + rows * (
        2 * 4 * qk_w + 2 * v_w + 4 * d + 2 * d + 2 * N_BRANCH * d + 4 * d)
    temps = 8 * 4 * rows * d
    return pl.pallas_call(
        functools.partial(_mixer_kernel, n_cast=len(cast_jobs), layer=layer),
        grid=(b, steps),
        in_specs=in_specs,
        out_specs=out_specs,
        out_shape=out_shape,
        scratch_shapes=scratch,
        compiler_params=pltpu.CompilerParams(
            dimension_semantics=("arbitrary", "arbitrary"),
            vmem_limit_bytes=_vmem_limit(resident, streamed, scratch_bytes, temps)),
        name="token_mixer",
    )(dchunk, x, mod, *rows_prm, rope_t, w_in, w_branch, w_out, w_pool, ws, bs_t, dintra, dqk,
      *[w for w, _ in cast_jobs])


def _ffn_kernel(x_ref, mod_ref, n2_ref, w1_ref, w2_ref, fn_ref, o_ref, hid_ref, *, layer, final):
    x = x_ref[...]
    batch_row = pl.ds(pl.program_id(0), 1)
    d = x_ref.shape[1]
    sh2, sc2, gt2 = (mod_ref[batch_row, k * d:(k + 1) * d] for k in (3, 4, 5))
    h = (_rms(x) * (n2_ref[layer:layer + 1, :] * (1.0 + sc2)) + sh2).astype(BF16)
    d_ff = w1_ref.shape[1]
    for lo in range(0, d_ff, FF_BLOCK):
        hid = _dot(h, _unpack_rows(w1_ref[:, lo:lo + FF_BLOCK]))
        hid_ref[:, lo:lo + FF_BLOCK] = jnp.square(jnp.maximum(hid, 0.0)).astype(BF16)
    y = x + gt2 * _dot(hid_ref[...], _unpack_rows(w2_ref[...]))
    if final:
        y = _rms(y) * fn_ref[...]
    o_ref[...] = y


def _ffn(layer, x, mod, norm2, w1, w2, final_norm, final):
    b, s, d = x.shape
    rows = FFN_ROWS
    d_ff = w1.shape[1]
    row_spec = pl.BlockSpec((None, rows, d), lambda i, j: (i, j, 0))
    lres = functools.partial(_layer_resident, layer)
    resident = 2 * 2 * d * d_ff + 4 * 2 * d + 4 * mod[0].size
    streamed = 4 * 2 * rows * d
    temps = rows * (2 * d + 6 * FF_BLOCK + 3 * 4 * d)
    return pl.pallas_call(
        functools.partial(_ffn_kernel, layer=layer, final=final),
        grid=(b, s // rows),
        in_specs=[
            row_spec,
            lres(mod.shape[1:]),
            _resident(norm2.shape), _resident(w1.shape), _resident(w2.shape), _resident((1, d)),
        ],
        out_specs=row_spec,
        out_shape=jax.ShapeDtypeStruct((b, s, d), F32),
        scratch_shapes=[pltpu.VMEM((rows, d_ff), BF16)],
        compiler_params=pltpu.CompilerParams(
            dimension_semantics=("arbitrary", "arbitrary"),
            vmem_limit_bytes=_vmem_limit(resident, streamed, 2 * rows * d_ff, temps)),
        name="channel_mlp",
    )(x, mod, norm2, w1, w2, final_norm)


def _retention_decay_tables(rows):
    log_gamma = jnp.log1p(-jnp.power(2.0, -5.0 - jnp.arange(RET_HEADS, dtype=F32)))
    pos = jnp.arange(CHUNK, dtype=F32)
    rel = pos[:, None] - pos[None, :]
    causal = rel >= 0
    intra = jnp.where(causal[None], jnp.exp(log_gamma[:, None, None] * jnp.where(causal, rel, 0.0)[None]), 0.0)
    decay_q = jnp.exp(log_gamma[:, None] * (pos + 1.0)[None])
    decay_k = jnp.exp(log_gamma[:, None] * (CHUNK - 1.0 - pos)[None])
    decay_chunk = jnp.exp(log_gamma * CHUNK)
    reps = rows // CHUNK
    widen = lambda t: jnp.broadcast_to(jnp.tile(t, (1, reps))[:, :, None], (RET_HEADS, rows, RET_QK_DIM))
    return intra, jnp.stack([widen(decay_q), widen(decay_k)]), decay_chunk


def kernel(x, c, positions, w_ada, b_ada, norm1, norm2, w_in, ws_gmlp, bs_gmlp, vnorm_gmlp, w_pool, b_pool,
           pool_scale, w_branch, w_out, w_ff1, w_ff2, final_norm):
    depth = w_in.shape[0]
    b, s, d = x.shape
    assert s % MIXER_ROWS == 0 and s % FFN_ROWS == 0 and MIXER_ROWS % CHUNK == 0
    assert w_ff1.shape[2] % FF_BLOCK == 0

    dintra, dqk, dchunk = _retention_decay_tables(MIXER_ROWS)
    rows_prm = (norm1, vnorm_gmlp, b_pool, pool_scale)
    bs_t = jnp.swapaxes(bs_gmlp, 1, 2)

    mixer_w = (w_in, w_branch, w_out, w_pool)
    as_rows = lambda w: w.reshape(depth, -1, w.shape[-1])
    mod, rope_t, *next_mixer = _prepare(c, w_ada, b_ada, positions, [(as_rows(w), 0) for w in mixer_w])
    for l in range(depth):
        mixer_bf16 = [w.reshape(ref.shape[1:-2] + (ref.shape[-2] // ROWS_PER_WORD, ref.shape[-1]))
                      for w, ref in zip(next_mixer, mixer_w)]
        jobs = [(as_rows(w_ff1), l), (as_rows(w_ff2), l)]
        if l + 1 < depth:
            jobs += [(as_rows(w), l + 1) for w in mixer_w]
        x, ff1, ff2, *next_mixer = _mixer(l, x, mod, rows_prm, rope_t, *mixer_bf16, ws_gmlp, bs_t,
                                          dintra, dqk, dchunk, jobs)
        x = _ffn(l, x, mod, norm2, ff1, ff2, final_norm.reshape(1, d), final=(l == depth - 1))
    return x
```

```python
import functools

import jax
import jax.numpy as jnp
import numpy as np
from jax import lax
from jax.experimental import pallas as pl
from jax.experimental.pallas import tpu as pltpu

CHUNK = 128
RET_HEADS = 4
RET_QK_DIM = 128
RET_V_DIM = 256
GMLP_GROUPS = 4
POOL_WINDOWS = (2, 4, 8, 16)
POOL_HISTORY = 16
N_BRANCH = 3
N_MOD = 6
ROPE_BASE = 10000.0
EPS = 1e-6

VMEM_BYTES_V7X = 64 * 1024 * 1024
VMEM_REQUEST_CAP_V7X = VMEM_BYTES_V7X - 4 * 1024 * 1024
BF16_ROW_TILE = 16

MIXER_ROWS = 256
FFN_ROWS = 1024
FF_BLOCK = 1024
GATE_BLOCK = 256
PREP_STEPS = 16

F32 = jnp.float32
BF16 = jnp.bfloat16


def _dot(a, b):
    return jnp.dot(a, b, preferred_element_type=F32)


def _rms(x):
    return x * lax.rsqrt(jnp.mean(x * x, axis=-1, keepdims=True) + EPS)


def _resident(shape):
    zeros = (0,) * len(shape)
    return pl.BlockSpec(shape, lambda *_: zeros, pipeline_mode=pl.Buffered(1))


def _layer_resident(layer, shape):
    index = (layer,) + (0,) * len(shape)
    return pl.BlockSpec((None,) + tuple(shape), lambda *_: index, pipeline_mode=pl.Buffered(1))


def _vmem_limit(resident_bytes, streamed_bytes, scratch_bytes, temp_bytes):
    need = resident_bytes + 2 * streamed_bytes + scratch_bytes + temp_bytes
    return min(int(need), VMEM_REQUEST_CAP_V7X)


PACKED = jnp.int32
ROWS_PER_WORD = 2


def _pack_rows(w_bf16):
    return pltpu.bitcast(w_bf16, PACKED)


def _unpack_rows(w_packed):
    return pltpu.bitcast(w_packed, BF16)


def _round_blocks(cast_in, cast_out):
    for src_ref, dst_ref in zip(cast_in, cast_out):
        dst_ref[...] = _pack_rows(src_ref[...].astype(BF16))


def _round_block_specs(cast_jobs, n_steps, flat_step):
    in_specs, out_specs, out_shapes, nbytes = [], [], [], 0
    for w, layer in cast_jobs:
        _, r, c = w.shape
        blk = r // n_steps
        assert blk * n_steps == r and blk % BF16_ROW_TILE == 0
        in_specs.append(pl.BlockSpec((None, blk, c), lambda *idx, layer=layer: (layer, flat_step(*idx), 0)))
        out_specs.append(pl.BlockSpec((blk // ROWS_PER_WORD, c), lambda *idx: (flat_step(*idx), 0)))
        out_shapes.append(jax.ShapeDtypeStruct((r // ROWS_PER_WORD, c), PACKED))
        nbytes += blk * c * (4 + 2)
    return in_specs, out_specs, out_shapes, nbytes


N_PREP_INPUTS = 6


def _prep_kernel(*refs, n_cast, steps_per_layer, steps_per_row):
    c_ref, wada_ref, bada_ref, pos_ref, freq_ref, sign_ref = refs[:N_PREP_INPUTS]
    cast_in = refs[N_PREP_INPUTS:N_PREP_INPUTS + n_cast]
    mod_ref, tab_ref = refs[N_PREP_INPUTS + n_cast:N_PREP_INPUTS + n_cast + 2]
    cast_out = refs[N_PREP_INPUTS + n_cast + 2:]
    _round_blocks(cast_in, cast_out)

    c = c_ref[...]
    c_act = (c * jax.nn.sigmoid(c)).astype(BF16)
    step = pl.program_id(0)
    mod_ref[...] = _dot(c_act, wada_ref[...].astype(BF16)) + bada_ref[pl.ds(step // steps_per_layer, 1), :]

    rows = tab_ref.shape[0]
    groups, half_rows, half = rows // RET_QK_DIM, rows // 2, RET_QK_DIM // 2
    pos = pos_ref[pl.ds(step // steps_per_row, 1), :].astype(F32)
    square = (RET_QK_DIM, RET_QK_DIM)
    diag = lax.broadcasted_iota(jnp.int32, square, 0) == lax.broadcasted_iota(jnp.int32, square, 1)
    group = lambda g: pos[:, g * RET_QK_DIM:(g + 1) * RET_QK_DIM]
    column = lambda g: jnp.sum(jnp.where(diag, group(g), 0.0), axis=1, keepdims=True)
    low_g = lax.broadcasted_iota(jnp.int32, square, 1) < half
    ang = jnp.concatenate([jnp.where(low_g, column(g), column(g + groups // 2)) for g in range(groups // 2)], axis=0)
    ang = ang * freq_ref[...]
    low = lax.broadcasted_iota(jnp.int32, (half_rows, RET_QK_DIM), 1) < half
    cos_p, sin_p = jnp.cos(ang), jnp.sin(ang)
    cos_s, sin_s = pltpu.roll(cos_p, half, 1), pltpu.roll(sin_p, half, 1)
    sign = sign_ref[...]
    tab_ref[:half_rows, :RET_QK_DIM] = jnp.where(low, cos_p, cos_s)
    tab_ref[half_rows:, :RET_QK_DIM] = jnp.where(low, cos_s, cos_p)
    tab_ref[:half_rows, RET_QK_DIM:] = jnp.where(low, sin_p, sin_s) * sign
    tab_ref[half_rows:, RET_QK_DIM:] = jnp.where(low, sin_s, sin_p) * sign


def _prepare(c, w_ada, b_ada, positions, cast_jobs):
    depth, d, n = w_ada.shape
    bsz, s = positions.shape
    steps = PREP_STEPS
    per_layer = steps // depth
    cols, rows = n // per_layer, bsz * s // steps
    assert per_layer * depth == steps and cols * per_layer == n and cols % RET_QK_DIM == 0
    per_row = s // rows
    assert rows * per_row == s and bsz * per_row == steps and rows % (2 * RET_QK_DIM) == 0
    half = RET_QK_DIM // 2
    inv_freq = ROPE_BASE ** (-jnp.arange(half, dtype=F32) / half)
    freq = jnp.concatenate([inv_freq, inv_freq]).reshape(1, RET_QK_DIM)
    sign = jnp.concatenate([-jnp.ones((half,), F32), jnp.ones((half,), F32)]).reshape(1, RET_QK_DIM)
    cast_in, cast_out, cast_shapes, cast_bytes = _round_block_specs(cast_jobs, steps, lambda t: t)
    ada_block = lambda lead: pl.BlockSpec((None, lead, cols), lambda t: (t // per_layer, 0, t % per_layer))
    const = lambda shape: pl.BlockSpec(shape, lambda t: (0,) * len(shape))
    streamed = cast_bytes + 4 * (d * cols + cols + bsz * cols + rows * (RET_QK_DIM + 2 * RET_QK_DIM))
    mod, table, *rounded = pl.pallas_call(
        functools.partial(_prep_kernel, n_cast=len(cast_jobs), steps_per_layer=per_layer, steps_per_row=per_row),
        grid=(steps,),
        in_specs=[
            const((bsz, d)), ada_block(d), pl.BlockSpec((depth, cols), lambda t: (0, t % per_layer)),
            pl.BlockSpec((bsz, rows), lambda t: (0, t % per_row)), const((1, RET_QK_DIM)), const((1, RET_QK_DIM)),
        ] + cast_in,
        out_specs=[ada_block(bsz), pl.BlockSpec((rows, 2 * RET_QK_DIM), lambda t: (t, 0))] + cast_out,
        out_shape=[jax.ShapeDtypeStruct((depth, bsz, n), F32),
                   jax.ShapeDtypeStruct((bsz * s, 2 * RET_QK_DIM), F32)] + cast_shapes,
        compiler_params=pltpu.CompilerParams(
            dimension_semantics=("arbitrary",),
            vmem_limit_bytes=_vmem_limit(4 * bsz * d, streamed, 0, 2 * d * cols + 16 * 4 * rows * RET_QK_DIM)),
        name="prepare",
    )(c, w_ada, b_ada, positions, freq, sign, *[w for w, _ in cast_jobs])
    return (mod, table.reshape(bsz, s, 2 * RET_QK_DIM), *rounded)


GELU_C0 = 0.7978845608028654
GELU_C1 = GELU_C0 * 0.044715


def _sigmoid(x):
    return 0.5 * jnp.tanh(0.5 * x) + 0.5


def _gelu_tanh(x):
    half_x = 0.5 * x
    return half_x + half_x * jnp.tanh(x * (GELU_C0 + GELU_C1 * (x * x)))


N_MIXER_INPUTS = 16


def _mixer_kernel(*refs, n_cast, layer):
    ins = refs[:N_MIXER_INPUTS]
    cast_in = refs[N_MIXER_INPUTS:N_MIXER_INPUTS + n_cast]
    o_ref = refs[N_MIXER_INPUTS + n_cast]
    cast_out = refs[N_MIXER_INPUTS + n_cast + 1:N_MIXER_INPUTS + 2 * n_cast + 1]
    scratch = refs[N_MIXER_INPUTS + 2 * n_cast + 1:]
    _mixer_step(*ins, o_ref, *scratch, layer=layer)
    _round_blocks(cast_in, cast_out)


def _mixer_step(dchunk_ref, x_ref, mod_ref, n1_ref, vgain_ref, bpool_ref, pscale_ref, rope_ref,
                win_ref, wbr_ref, wout_ref, wpool_ref, ws_ref, bst_ref, dintra_ref, dqk_ref,
                o_ref,
                state_ref, ptail_ref, qb_ref, qd_ref, kb_ref, kd_ref, vb_ref, u_ref, vn_ref, y_ref, m_ref,
                *, layer):
    rows, d = x_ref.shape
    n_chunks = rows // CHUNK
    qk_w = RET_HEADS * RET_QK_DIM
    v_w = RET_HEADS * RET_V_DIM
    gdim = d // GMLP_GROUPS
    pdim = d // len(POOL_WINDOWS)
    o_q, o_k, o_v = 0, qk_w, 2 * qk_w
    o_g = o_v + v_w
    o_u = o_g + v_w
    o_vs = o_u + d
    o_p = o_vs + d
    o_gate = o_p + d
    seq_step = pl.program_id(1)
    row_slices = [slice(c * CHUNK, (c + 1) * CHUNK) for c in range(n_chunks)]
    qk_slices = [slice(hd * RET_QK_DIM, (hd + 1) * RET_QK_DIM) for hd in range(RET_HEADS)]
    v_slices = [slice(hd * RET_V_DIM, (hd + 1) * RET_V_DIM) for hd in range(RET_HEADS)]

    @pl.when(seq_step == 0)
    def _():
        state_ref[...] = jnp.zeros_like(state_ref)
        ptail_ref[...] = jnp.zeros_like(ptail_ref)

    x = x_ref[...]
    batch_row = pl.ds(pl.program_id(0), 1)
    sh1, sc1, gt1 = (mod_ref[batch_row, k * d:(k + 1) * d] for k in (0, 1, 2))
    h = (_rms(x) * (n1_ref[layer:layer + 1, :] * (1.0 + sc1)) + sh1).astype(BF16)

    def proj(lo, width):
        return _dot(h, _unpack_rows(win_ref[:, lo:lo + width]))

    def add_gated_branch(n, first=False):
        for lo in range(0, d, GATE_BLOCK):
            cols = slice(lo, lo + GATE_BLOCK)
            gated = _sigmoid(proj(o_gate + n * d + lo, GATE_BLOCK)) * _dot(y_ref[n], _unpack_rows(wbr_ref[n, :, cols]))
            m_ref[:, cols] = gated if first else m_ref[:, cols] + gated

    cosv, sinv = rope_ref[:, :RET_QK_DIM], rope_ref[:, RET_QK_DIM:]
    qf, kf = proj(o_q, qk_w), proj(o_k, qk_w)
    k_scale = RET_QK_DIM ** -0.5
    for hd, sl in enumerate(qk_slices):
        q_h, k_h = qf[:, sl], kf[:, sl]
        q_r = q_h * cosv + pltpu.roll(q_h, RET_QK_DIM // 2, 1) * sinv
        k_r = (k_h * cosv + pltpu.roll(k_h, RET_QK_DIM // 2, 1) * sinv) * k_scale
        qb_ref[:, sl] = q_r.astype(BF16)
        qd_ref[:, sl] = (q_r * dqk_ref[0, hd]).astype(BF16)
        kb_ref[:, sl] = k_r.astype(BF16)
        kd_ref[:, sl] = (k_r * dqk_ref[1, hd]).astype(BF16)
    vb_ref[...] = proj(o_v, v_w).astype(BF16)
    vs = _gelu_tanh(proj(o_vs, d))
    vn_ref[...] = (_rms(vs) * vgain_ref[layer:layer + 1, :]).astype(BF16)

    scores, kv = {}, {}
    for c, rs in enumerate(row_slices):
        for hd, sl in enumerate(qk_slices):
            s = lax.dot_general(qb_ref[rs, sl], kb_ref[rs, sl], (((1,), (1,)), ((), ())),
                                preferred_element_type=F32)
            scores[c, hd] = (s * dintra_ref[hd]).astype(BF16)
            kv[c, hd] = lax.dot_general(kd_ref[rs, sl], vb_ref[rs, v_slices[hd]], (((0,), (0,)), ((), ())),
                                        preferred_element_type=F32)

    u_ref[...] = _gelu_tanh(proj(o_u, d))

    for hd, (sl, vsl) in enumerate(zip(qk_slices, v_slices)):
        state = state_ref[hd]
        ret = []
        for c, rs in enumerate(row_slices):
            lhs = jnp.concatenate([scores[c, hd], qd_ref[rs, sl]], axis=1)
            rhs = jnp.concatenate([vb_ref[rs, vsl], state.astype(BF16)], axis=0)
            ret.append(_dot(lhs, rhs))
            state = dchunk_ref[hd] * state + kv[c, hd]
        state_ref[hd] = state
        half_g = 0.5 * proj(o_g + hd * RET_V_DIM, RET_V_DIM)
        silu_g = half_g + half_g * jnp.tanh(half_g)
        for c, rs in enumerate(row_slices):
            y_ref[0, rs, vsl] = (silu_g[rs, :] * _rms(ret[c])).astype(BF16)

    p = proj(o_p, d)
    ext = jnp.concatenate([ptail_ref[...], p], axis=0)
    ptail_ref[...] = p[rows - POOL_HISTORY:, :]
    pos = seq_step * rows + lax.broadcasted_iota(jnp.int32, (rows, pdim), 0)
    pooled = []
    for gi, window in enumerate(POOL_WINDOWS):
        psl = slice(gi * pdim, (gi + 1) * pdim)
        acc = ext[:, psl]
        span = 1
        while span < window:
            acc = acc + pltpu.roll(acc, span, 0)
            span *= 2
        count = jnp.minimum(pos + 1, window).astype(F32)
        pooled.append((acc[POOL_HISTORY:, :] / count - p[:, psl]).astype(BF16))

    tri = lax.broadcasted_iota(jnp.int32, (CHUNK, CHUNK), 0) >= lax.broadcasted_iota(jnp.int32, (CHUNK, CHUNK), 1)
    for gi in range(GMLP_GROUPS):
        gsl = slice(gi * gdim, (gi + 1) * gdim)
        w_tri = jnp.where(tri, ws_ref[gi], 0.0).astype(BF16)
        bias = jnp.broadcast_to(bst_ref[:, gi:gi + 1], (CHUNK, gdim))
        for rs in row_slices:
            mixed = _dot(w_tri, vn_ref[rs, gsl]) + bias
            y_ref[1, rs, gsl] = (u_ref[rs, gsl] * mixed).astype(BF16)

    add_gated_branch(0, first=True)

    for gi in range(len(POOL_WINDOWS)):
        psl = slice(gi * pdim, (gi + 1) * pdim)
        mixed = _dot(pooled[gi], _unpack_rows(wpool_ref[gi])) + bpool_ref[layer, gi:gi + 1, :]
        y_ref[2, :, psl] = (mixed * pscale_ref[layer:layer + 1, psl]).astype(BF16)

    add_gated_branch(1)
    add_gated_branch(2)
    o_ref[...] = x + gt1 * _dot(m_ref[...].astype(BF16), _unpack_rows(wout_ref[...]))


def _mixer(layer, x, mod, rows_prm, rope_t, w_in, w_branch, w_out, w_pool, ws, bs_t, dintra, dqk, dchunk, cast_jobs):
    b, s, d = x.shape
    rows = MIXER_ROWS
    steps = s // rows
    d_in = w_in.shape[1]
    v_w = RET_HEADS * RET_V_DIM
    qk_w = RET_HEADS * RET_QK_DIM
    row_spec = lambda width: pl.BlockSpec((None, rows, width), lambda i, j: (i, j, 0))
    lres = functools.partial(_layer_resident, layer)
    in_specs = [
        pl.BlockSpec(memory_space=pltpu.SMEM),
        row_spec(d),
        lres(mod.shape[1:]),
        *[_resident(v.shape) for v in rows_prm],
        row_spec(2 * RET_QK_DIM),
        _resident(w_in.shape), _resident(w_branch.shape), _resident(w_out.shape),
        _resident(w_pool.shape), lres(ws.shape[1:]), lres(bs_t.shape[1:]),
        _resident(dintra.shape), _resident(dqk.shape),
    ]
    assert len(in_specs) == N_MIXER_INPUTS
    cast_in, cast_out, cast_shapes, cast_bytes = _round_block_specs(cast_jobs, b * steps, lambda i, j: i * steps + j)
    in_specs += cast_in
    out_specs = [row_spec(d)] + cast_out
    out_shape = [jax.ShapeDtypeStruct((b, s, d), F32)] + cast_shapes
    scratch = [
        pltpu.VMEM((RET_HEADS, RET_QK_DIM, RET_V_DIM), F32),
        pltpu.VMEM((POOL_HISTORY, d), F32),
        pltpu.VMEM((rows, qk_w), BF16), pltpu.VMEM((rows, qk_w), BF16),
        pltpu.VMEM((rows, qk_w), BF16), pltpu.VMEM((rows, qk_w), BF16),
        pltpu.VMEM((rows, v_w), BF16),
        pltpu.VMEM((rows, d), F32),
        pltpu.VMEM((rows, d), BF16),
        pltpu.VMEM((N_BRANCH, rows, d), BF16),
        pltpu.VMEM((rows, d), F32),
    ]
    resident = 4 * (w_in.size + w_branch.size + w_out.size + w_pool.size) + 4 * (
        ws[0].size + bs_t[0].size + dintra.size + dqk.size + mod[0].size + sum(v.size for v in rows_prm))
    streamed = 4 * rows * (2 * d + 2 * RET_QK_DIM) + cast_bytes
    scratch_bytes = 4 * (RET_HEADS * RET_QK_DIM * RET_V_DIM + POOL_HISTORY * d) + rows * (
        2 * 4 * qk_w + 2 * v_w + 4 * d + 2 * d + 2 * N_BRANCH * d + 4 * d)
    temps = 8 * 4 * rows * d
    return pl.pallas_call(
        functools.partial(_mixer_kernel, n_cast=len(cast_jobs), layer=layer),
        grid=(b, steps),
        in_specs=in_specs,
        out_specs=out_specs,
        out_shape=out_shape,
        scratch_shapes=scratch,
        compiler_params=pltpu.CompilerParams(
            dimension_semantics=("arbitrary", "arbitrary"),
            vmem_limit_bytes=_vmem_limit(resident, streamed, scratch_bytes, temps)),
        name="token_mixer",
    )(dchunk, x, mod, *rows_prm, rope_t, w_in, w_branch, w_out, w_pool, ws, bs_t, dintra, dqk,
      *[w for w, _ in cast_jobs])


def _ffn_kernel(x_ref, mod_ref, n2_ref, w1_ref, w2_ref, fn_ref, o_ref, hid_ref, *, layer, final):
    x = x_ref[...]
    batch_row = pl.ds(pl.program_id(0), 1)
    d = x_ref.shape[1]
    sh2, sc2, gt2 = (mod_ref[batch_row, k * d:(k + 1) * d] for k in (3, 4, 5))
    h = (_rms(x) * (n2_ref[layer:layer + 1, :] * (1.0 + sc2)) + sh2).astype(BF16)
    d_ff = w1_ref.shape[1]
    for lo in range(0, d_ff, FF_BLOCK):
        hid = _dot(h, _unpack_rows(w1_ref[:, lo:lo + FF_BLOCK]))
        hid_ref[:, lo:lo + FF_BLOCK] = jnp.square(jnp.maximum(hid, 0.0)).astype(BF16)
    y = x + gt2 * _dot(hid_ref[...], _unpack_rows(w2_ref[...]))
    if final:
        y = _rms(y) * fn_ref[...]
    o_ref[...] = y


def _ffn(layer, x, mod, norm2, w1, w2, final_norm, final):
    b, s, d = x.shape
    rows = FFN_ROWS
    d_ff = w1.shape[1]
    row_spec = pl.BlockSpec((None, rows, d), lambda i, j: (i, j, 0))
    lres = functools.partial(_layer_resident, layer)
    resident = 2 * 2 * d * d_ff + 4 * 2 * d + 4 * mod[0].size
    streamed = 4 * 2 * rows * d
    temps = rows * (2 * d + 6 * FF_BLOCK + 3 * 4 * d)
    return pl.pallas_call(
        functools.partial(_ffn_kernel, layer=layer, final=final),
        grid=(b, s // rows),
        in_specs=[
            row_spec,
            lres(mod.shape[1:]),
            _resident(norm2.shape), _resident(w1.shape), _resident(w2.shape), _resident((1, d)),
        ],
        out_specs=row_spec,
        out_shape=jax.ShapeDtypeStruct((b, s, d), F32),
        scratch_shapes=[pltpu.VMEM((rows, d_ff), BF16)],
        compiler_params=pltpu.CompilerParams(
            dimension_semantics=("arbitrary", "arbitrary"),
            vmem_limit_bytes=_vmem_limit(resident, streamed, 2 * rows * d_ff, temps)),
        name="channel_mlp",
    )(x, mod, norm2, w1, w2, final_norm)


def _retention_decay_tables(rows):
    f32 = np.float32
    log_gamma = np.log1p(-np.power(f32(2.0), f32(-5.0) - np.arange(RET_HEADS, dtype=f32)))
    pos = np.arange(CHUNK, dtype=f32)
    rel = pos[:, None] - pos[None, :]
    causal = rel >= 0
    intra = np.where(causal[None], np.exp(log_gamma[:, None, None] * np.where(causal, rel, f32(0.0))[None]), f32(0.0))
    decay_q = np.exp(log_gamma[:, None] * (pos + f32(1.0))[None])
    decay_k = np.exp(log_gamma[:, None] * (f32(CHUNK - 1.0) - pos)[None])
    decay_chunk = np.exp(log_gamma * f32(CHUNK))
    reps = rows // CHUNK
    widen = lambda t: np.broadcast_to(np.tile(t, (1, reps))[:, :, None], (RET_HEADS, rows, RET_QK_DIM))
    as_f32 = lambda t: jnp.asarray(np.ascontiguousarray(t, dtype=f32))
    return as_f32(intra), as_f32(np.stack([widen(decay_q), widen(decay_k)])), as_f32(decay_chunk)


def kernel(x, c, positions, w_ada, b_ada, norm1, norm2, w_in, ws_gmlp, bs_gmlp, vnorm_gmlp, w_pool, b_pool,
           pool_scale, w_branch, w_out, w_ff1, w_ff2, final_norm):
    depth = w_in.shape[0]
    b, s, d = x.shape
    assert s % MIXER_ROWS == 0 and s % FFN_ROWS == 0 and MIXER_ROWS % CHUNK == 0
    assert w_ff1.shape[2] % FF_BLOCK == 0

    dintra, dqk, dchunk = _retention_decay_tables(MIXER_ROWS)
    rows_prm = (norm1, vnorm_gmlp, b_pool, pool_scale)
    bs_t = jnp.swapaxes(bs_gmlp, 1, 2)

    mixer_w = (w_in, w_branch, w_out, w_pool)
    as_rows = lambda w: w.reshape(depth, -1, w.shape[-1])
    mod, rope_t, *next_mixer = _prepare(c, w_ada, b_ada, positions, [(as_rows(w), 0) for w in mixer_w])
    for l in range(depth):
        mixer_bf16 = [w.reshape(ref.shape[1:-2] + (ref.shape[-2] // ROWS_PER_WORD, ref.shape[-1]))
                      for w, ref in zip(next_mixer, mixer_w)]
        jobs = [(as_rows(w_ff1), l), (as_rows(w_ff2), l)]
        if l + 1 < depth:
            jobs += [(as_rows(w), l + 1) for w in mixer_w]
        x, ff1, ff2, *next_mixer = _mixer(l, x, mod, rows_prm, rope_t, *mixer_bf16, ws_gmlp, bs_t,
                                          dintra, dqk, dchunk, jobs)
        x = _ffn(l, x, mod, norm2, ff1, ff2, final_norm.reshape(1, d), final=(l == depth - 1))
    return x
```
